```python
import math
import jax, jax.numpy as jnp
from jax import lax
import numpy as np

D_MODEL = 2048
BATCH = 2
SEQ = 16384
DEPTH = 4
DEC_BATCH = 32
DEC_SEQ = 64
PAST_LEN = 2048

CHUNK = 64
Q_BLOCK = 128
N_PAIR = DEPTH // 2
D_FF = 4 * D_MODEL
EPS = 1e-6
F32 = jnp.float32

POOL_DIM = D_MODEL // 4
POOL_WINDOWS = (2, 4, 8, 16)
POOL_GROUPS = len(POOL_WINDOWS)
POOL_GDIM = POOL_DIM // POOL_GROUPS
POOL_MAX = max(POOL_WINDOWS)

DIFF_DH = 64
DIFF_DV = 2 * DIFF_DH
DIFF_HEADS = (D_MODEL - POOL_DIM) // DIFF_DV
DIFF_WIDTH = DIFF_HEADS * DIFF_DV
ROPE_THETA = 500000.0
ROPE_DIM = DIFF_DH // 4

RET_HEADS = D_MODEL // 512
RET_DK = 256
RET_DV = 256
RET_WIDTH = RET_HEADS * RET_DV
RET_THETA = 10000.0
RET_LOG_GAMMA = tuple(math.log(1.0 - 2.0 ** (-5 - h)) for h in range(RET_HEADS))

FOX_DH = 128
FOX_HEADS = (D_MODEL - RET_WIDTH) // FOX_DH
FOX_WIDTH = FOX_HEADS * FOX_DH

EVEN_IN = POOL_DIM + 3 * DIFF_WIDTH
ODD_IN = 2 * RET_HEADS * RET_DK + 2 * RET_WIDTH + 3 * FOX_WIDTH + FOX_HEADS
EVEN_MIX = POOL_DIM + DIFF_WIDTH
ODD_MIX = RET_WIDTH + FOX_WIDTH

kernel_name = 'hybrid_stream_pool_diff_ret_fox_step'


def rms_norm(x, g):
    xf = x.astype(F32)
    y = xf * lax.rsqrt(jnp.mean(xf * xf, axis=-1, keepdims=True) + EPS)
    return (y * g.astype(F32)).astype(x.dtype)


def head_group_norm(x, g):
    xf = x.astype(F32)
    mu = jnp.mean(xf, axis=-1, keepdims=True)
    xc = xf - mu
    var = jnp.mean(xc * xc, axis=-1, keepdims=True)
    return xc * lax.rsqrt(var + EPS) * g.astype(F32)


def modulate(x, g, shift, scale):
    return rms_norm(x, g) * (1.0 + scale) + shift


def split_last(x, widths):
    idx = np.cumsum(widths)[:-1].tolist()
    return jnp.split(x, idx, axis=-1)


def rope_cos_sin(pos, dim, theta):
    inv_freq = theta ** (-jnp.arange(0, dim, 2, dtype=F32) / dim)
    ang = pos.astype(F32)[:, None] * inv_freq[None, :]
    return jnp.cos(ang), jnp.sin(ang)


def rotate(x, cos, sin):
    half = x.shape[-1] // 2
    x1 = x[..., :half].astype(F32)
    x2 = x[..., half:].astype(F32)
    return jnp.concatenate([x1 * cos - x2 * sin, x2 * cos + x1 * sin], axis=-1).astype(x.dtype)


def partial_rope(x, cos, sin):
    return jnp.concatenate([rotate(x[..., :ROPE_DIM], cos, sin), x[..., ROPE_DIM:]], axis=-1)


def query_block_sweep(fn, *q_side):
    b, l = q_side[0].shape[:2]
    if l <= Q_BLOCK:
        return fn(0, *q_side)
    nb = l // Q_BLOCK
    blocked = tuple(a.reshape((b, nb, Q_BLOCK) + a.shape[2:]).swapaxes(0, 1) for a in q_side)
    out = lax.map(lambda args: fn(args[0] * Q_BLOCK, *args[1:]), (jnp.arange(nb),) + blocked)
    return out.swapaxes(0, 1).reshape((b, l) + out.shape[3:])


def pool_mixer(u, prev, pos0, w_mix, scale):
    b, l, _ = u.shape
    full = jnp.concatenate([prev.astype(u.dtype), u], axis=1)
    cs = jnp.pad(jnp.cumsum(full.astype(F32), axis=1), ((0, 0), (1, 0), (0, 0)))
    pos = pos0 + jnp.arange(l)
    groups = []
    for g, w in enumerate(POOL_WINDOWS):
        sl = slice(g * POOL_GDIM, (g + 1) * POOL_GDIM)
        win = cs[:, POOL_MAX:POOL_MAX + l, sl] - cs[:, POOL_MAX - w:POOL_MAX - w + l, sl]
        cnt = jnp.minimum(pos + 1, w).astype(F32)[None, :, None]
        groups.append(win / cnt)
    d = (jnp.concatenate(groups, axis=-1) - u.astype(F32)).astype(u.dtype)
    d = d.reshape(b, l, POOL_GROUPS, POOL_GDIM)
    mixed = jnp.einsum('blgc,gcd->blgd', d, w_mix).reshape(b, l, POOL_DIM) * scale
    return mixed, full[:, -(POOL_MAX - 1):]


def diff_core(q, k, v, mask, lam):
    s = jnp.einsum('bqhcd,bkhcd->bhcqk', q, k).astype(F32) * DIFF_DH ** -0.5
    s = jnp.where(mask, s, -jnp.inf)
    p = jax.nn.softmax(s, axis=-1)
    a = p[:, :, 0] - lam * p[:, :, 1]
    return jnp.einsum('bhqk,bkhd->bqhd', a.astype(v.dtype), v)


def fox_core(q, k, v, cq, ck, mask):
    s = jnp.einsum('bqhd,bkhd->bhqk', q, k).astype(F32) * FOX_DH ** -0.5
    s = s + jnp.moveaxis(cq, 1, 2)[..., :, None] - jnp.moveaxis(ck, 1, 2)[..., None, :]
    s = jnp.where(mask, s, -jnp.inf)
    p = jax.nn.softmax(s, axis=-1)
    return jnp.einsum('bhqk,bkhd->bqhd', p.astype(v.dtype), v)


def retention(q, k, v, s0):
    b, l, nh = q.shape[:3]
    c = CHUNK if l % CHUNK == 0 else l
    nc = l // c
    lg = jnp.asarray(RET_LOG_GAMMA, F32)
    i = jnp.arange(c, dtype=F32)
    dist = i[:, None] - i[None, :]
    intra = jnp.where(dist >= 0, jnp.exp(lg[:, None, None] * jnp.maximum(dist, 0.0)), 0.0)
    q_dec = jnp.exp(lg[None, :] * (i[:, None] + 1.0))
    k_dec = jnp.exp(lg[None, :] * (c - 1.0 - i[:, None]))
    blk_dec = jnp.exp(lg * c)

    def to_chunks(a):
        return a.astype(F32).reshape((b, nc, c) + a.shape[2:]).swapaxes(0, 1)

    def step(s, inp):
        qi, ki, vi = inp
        att = jnp.einsum('bihd,bjhd->bhij', qi, ki) * intra
        o = (jnp.einsum('bhij,bjhe->bihe', att, vi)
             + jnp.einsum('bihd,bhde->bihe', qi * q_dec[None, :, :, None], s))
        s = s * blk_dec[None, :, None, None] + jnp.einsum('bjhd,bjhe->bhde', ki * k_dec[None, :, :, None], vi)
        return s, o

    s, o = lax.scan(step, s0.astype(F32), (to_chunks(q), to_chunks(k), to_chunks(v)))
    return o.swapaxes(0, 1).reshape(b, l, nh, v.shape[-1]), s


def even_mixer(h, pool_prev, past_k, past_v, w_in, w_out, pool_w, pool_scale, qn, kn, lam_vec, out_gn, lam_init):
    b, l, _ = h.shape
    p = past_k.shape[1]
    u, q, k, v = split_last(h @ w_in, (POOL_DIM, DIFF_WIDTH, DIFF_WIDTH, DIFF_WIDTH))
    pool_out, new_pool = pool_mixer(u, pool_prev, p, pool_w, pool_scale)
    cos, sin = rope_cos_sin(p + jnp.arange(l), ROPE_DIM, ROPE_THETA)
    cos, sin = cos[None, :, None, None, :], sin[None, :, None, None, :]
    q = partial_rope(rms_norm(q.reshape(b, l, DIFF_HEADS, 2, DIFF_DH), qn), cos, sin)
    k = partial_rope(rms_norm(k.reshape(b, l, DIFF_HEADS, 2, DIFF_DH), kn), cos, sin)
    v = v.reshape(b, l, DIFF_HEADS, DIFF_DV)
    k_rows = k.reshape(b, l, DIFF_HEADS, 2 * DIFF_DH)
    k_all = jnp.concatenate([past_k.astype(k.dtype), k_rows], axis=1).reshape(b, p + l, DIFF_HEADS, 2, DIFF_DH)
    v_all = jnp.concatenate([past_v.astype(v.dtype), v], axis=1)
    lv = lam_vec.astype(F32)
    lam = jnp.exp(jnp.sum(lv[0] * lv[1])) - jnp.exp(jnp.sum(lv[2] * lv[3])) + lam_init
    k_chunk = jnp.arange(p + l) // CHUNK

    def block(q0, qb):
        q_chunk = (p + q0 + jnp.arange(qb.shape[1])) // CHUNK
        return diff_core(qb, k_all, v_all, k_chunk[None, :] <= q_chunk[:, None], lam)

    o = query_block_sweep(block, q)
    o = rms_norm(o, out_gn) * (1.0 - lam_init)
    out = jnp.concatenate([pool_out, o.reshape(b, l, DIFF_WIDTH)], axis=-1) @ w_out
    return out, new_pool, k_rows, v


def odd_mixer(h, s0, past_k, past_v, past_logf, w_in, w_out, ret_gn, qn, kn, f_bias):
    b, l, _ = h.shape
    p = past_k.shape[1]
    rq, rk, rv, rg, fq, fk, fv, fl = split_last(
        h @ w_in, (RET_HEADS * RET_DK, RET_HEADS * RET_DK, RET_WIDTH, RET_WIDTH,
                   FOX_WIDTH, FOX_WIDTH, FOX_WIDTH, FOX_HEADS))
    cos, sin = rope_cos_sin(p + jnp.arange(l), RET_DK, RET_THETA)
    cos, sin = cos[None, :, None, :], sin[None, :, None, :]
    rq = rotate(rq.reshape(b, l, RET_HEADS, RET_DK), cos, sin)
    rk = rotate(rk.reshape(b, l, RET_HEADS, RET_DK), cos, sin) * RET_DK ** -0.5
    ret, s_new = retention(rq, rk, rv.reshape(b, l, RET_HEADS, RET_DV), s0)
    ret = head_group_norm(ret, ret_gn) * jax.nn.silu(rg.astype(F32)).reshape(b, l, RET_HEADS, RET_DV)
    ret = ret.astype(h.dtype).reshape(b, l, RET_WIDTH)
    fq = rms_norm(fq.reshape(b, l, FOX_HEADS, FOX_DH), qn)
    fk = rms_norm(fk.reshape(b, l, FOX_HEADS, FOX_DH), kn)
    fv = fv.reshape(b, l, FOX_HEADS, FOX_DH)
    logf = jax.nn.log_sigmoid(fl.astype(F32) + f_bias.astype(F32))
    k_all = jnp.concatenate([past_k.astype(fk.dtype), fk], axis=1)
    v_all = jnp.concatenate([past_v.astype(fv.dtype), fv], axis=1)
    cum = jnp.cumsum(jnp.concatenate([past_logf.astype(F32), logf], axis=1), axis=1)
    k_pos = jnp.arange(p + l)

    def block(q0, qb, cq):
        q_pos = p + q0 + jnp.arange(qb.shape[1])
        return fox_core(qb, k_all, v_all, cq, cum, k_pos[None, :] <= q_pos[:, None])

    o = query_block_sweep(block, fq, cum[:, p:])
    out = jnp.concatenate([ret, o.reshape(b, l, FOX_WIDTH)], axis=-1) @ w_out
    return out, s_new, fk, fv, logf


def sq_relu_mlp(h, w_up, w_down):
    a = jax.nn.relu(h @ w_up)
    return (a * a) @ w_down


def trunk(x, c, pool_prev, diff_k, diff_v, ret_s, fox_k, fox_v, fox_logf,
          w_ada, b_ada, norm1, norm2, w_up, w_down,
          w_in_even, w_out_even, pool_w, pool_scale, diff_qn, diff_kn, diff_lam, diff_gn,
          w_in_odd, w_out_odd, ret_gn, fox_qn, fox_kn, fox_fbias):
    n_pool, n_dk, n_dv, n_rs, n_fk, n_fv, n_fl = [], [], [], [], [], [], []
    c_act = jax.nn.silu(c.astype(F32)).astype(c.dtype)
    for l in range(DEPTH):
        j = l // 2
        mod = c_act @ w_ada[l] + b_ada[l]
        sh1, sc1, g1, sh2, sc2, g2 = jnp.split(mod[:, None, :], 6, axis=-1)
        h = modulate(x, norm1[l], sh1, sc1)
        if l % 2 == 0:
            m, p_new, k_new, v_new = even_mixer(
                h, pool_prev[j], diff_k[j], diff_v[j], w_in_even[j], w_out_even[j], pool_w[j],
                pool_scale[j], diff_qn[j], diff_kn[j], diff_lam[j], diff_gn[j],
                0.8 - 0.6 * math.exp(-0.3 * l))
            n_pool.append(p_new)
            n_dk.append(k_new)
            n_dv.append(v_new)
        else:
            m, s_new, k_new, v_new, lf_new = odd_mixer(
                h, ret_s[j], fox_k[j], fox_v[j], fox_logf[j], w_in_odd[j], w_out_odd[j],
                ret_gn[j], fox_qn[j], fox_kn[j], fox_fbias[j])
            n_rs.append(s_new)
            n_fk.append(k_new)
            n_fv.append(v_new)
            n_fl.append(lf_new)
        x = x + g1 * m
        h = modulate(x, norm2[l], sh2, sc2)
        x = x + g2 * sq_relu_mlp(h, w_up[l], w_down[l])
    st = jnp.stack
    return x, st(n_pool), st(n_dk), st(n_dv), st(n_rs), st(n_fk), st(n_fv), st(n_fl)


def setup_inputs(seed: int = 0) -> dict:
    key = jax.random.key(seed)
    ks = iter(jax.random.split(key, 40))

    def nrm(shape, scale=1.0):
        return jax.random.normal(next(ks), shape, F32) * scale

    def gain(shape):
        return 1.0 + nrm(shape, 0.05)

    return {
        'x_prompt': nrm((BATCH, SEQ, D_MODEL)),
        'x_sample': nrm((DEC_BATCH, DEC_SEQ, D_MODEL)),
        'c_prompt': nrm((BATCH, D_MODEL)),
        'c_sample': nrm((DEC_BATCH, D_MODEL)),
        'cache_pool': nrm((N_PAIR, DEC_BATCH, POOL_MAX - 1, POOL_DIM)),
        'cache_diff_k': nrm((N_PAIR, DEC_BATCH, PAST_LEN, DIFF_HEADS, 2 * DIFF_DH)),
        'cache_diff_v': nrm((N_PAIR, DEC_BATCH, PAST_LEN, DIFF_HEADS, DIFF_DV)),
        'state_ret': nrm((N_PAIR, DEC_BATCH, RET_HEADS, RET_DK, RET_DV), 0.1),
        'cache_fox_k': nrm((N_PAIR, DEC_BATCH, PAST_LEN, FOX_HEADS, FOX_DH)),
        'cache_fox_v': nrm((N_PAIR, DEC_BATCH, PAST_LEN, FOX_HEADS, FOX_DH)),
        'cache_fox_logf': jax.nn.log_sigmoid(jax.random.uniform(
            next(ks), (N_PAIR, DEC_BATCH, PAST_LEN, FOX_HEADS), F32, 1.0, 6.0)),
        'w_ada': nrm((DEPTH, D_MODEL, 6 * D_MODEL), 0.5 * D_MODEL ** -0.5),
        'b_ada': nrm((DEPTH, 6 * D_MODEL), 0.1),
        'norm1': gain((DEPTH, D_MODEL)),
        'norm2': gain((DEPTH, D_MODEL)),
        'w_up': nrm((DEPTH, D_MODEL, D_FF), D_MODEL ** -0.5),
        'w_down': nrm((DEPTH, D_FF, D_MODEL), D_FF ** -0.5),
        'w_in_even': nrm((N_PAIR, D_MODEL, EVEN_IN), D_MODEL ** -0.5),
        'w_out_even': nrm((N_PAIR, EVEN_MIX, D_MODEL), EVEN_MIX ** -0.5),
        'pool_w': nrm((N_PAIR, POOL_GROUPS, POOL_GDIM, POOL_GDIM), POOL_GDIM ** -0.5),
        'pool_scale': gain((N_PAIR, POOL_DIM)),
        'diff_qn': gain((N_PAIR, DIFF_DH)),
        'diff_kn': gain((N_PAIR, DIFF_DH)),
        'diff_lam': nrm((N_PAIR, 4, DIFF_DH), 0.1),
        'diff_gn': gain((N_PAIR, DIFF_DV)),
        'w_in_odd': nrm((N_PAIR, D_MODEL, ODD_IN), D_MODEL ** -0.5),
        'w_out_odd': nrm((N_PAIR, ODD_MIX, D_MODEL), ODD_MIX ** -0.5),
        'ret_gn': gain((N_PAIR, RET_HEADS, RET_DV)),
        'fox_qn': gain((N_PAIR, FOX_DH)),
        'fox_kn': gain((N_PAIR, FOX_DH)),
        'fox_fbias': jax.random.uniform(next(ks), (N_PAIR, FOX_HEADS), F32, 1.0, 6.0),
    }


def reference(x_prompt, x_sample, c_prompt, c_sample, cache_pool, cache_diff_k, cache_diff_v,
              state_ret, cache_fox_k, cache_fox_v, cache_fox_logf,
              w_ada, b_ada, norm1, norm2, w_up, w_down,
              w_in_even, w_out_even, pool_w, pool_scale, diff_qn, diff_kn, diff_lam, diff_gn,
              w_in_odd, w_out_odd, ret_gn, fox_qn, fox_kn, fox_fbias):
    weights = (w_ada, b_ada, norm1, norm2, w_up, w_down,
               w_in_even, w_out_even, pool_w, pool_scale, diff_qn, diff_kn, diff_lam, diff_gn,
               w_in_odd, w_out_odd, ret_gn, fox_qn, fox_kn, fox_fbias)
    b = x_prompt.shape[0]
    dt = x_prompt.dtype
    y_p, pool_p, dk_p, dv_p, rs_p, fk_p, fv_p, fl_p = trunk(
        x_prompt, c_prompt,
        jnp.zeros((N_PAIR, b, POOL_MAX - 1, POOL_DIM), dt),
        jnp.zeros((N_PAIR, b, 0, DIFF_HEADS, 2 * DIFF_DH), dt),
        jnp.zeros((N_PAIR, b, 0, DIFF_HEADS, DIFF_DV), dt),
        jnp.zeros((N_PAIR, b, RET_HEADS, RET_DK, RET_DV), F32),
        jnp.zeros((N_PAIR, b, 0, FOX_HEADS, FOX_DH), dt),
        jnp.zeros((N_PAIR, b, 0, FOX_HEADS, FOX_DH), dt),
        jnp.zeros((N_PAIR, b, 0, FOX_HEADS), F32),
        *weights)
    y_s, pool_s, dk_s, dv_s, rs_s, fk_s, fv_s, fl_s = trunk(
        x_sample, c_sample, cache_pool, cache_diff_k, cache_diff_v, state_ret,
        cache_fox_k, cache_fox_v, cache_fox_logf, *weights)
    return (y_p, y_s, pool_p, pool_s, dk_p, dk_s, dv_p, dv_s, rs_p, rs_s,
            fk_p, fk_s, fv_p, fv_s, fl_p, fl_s)
```

```python
import functools
import math

import jax
import jax.numpy as jnp
from jax import lax
from jax.experimental import pallas as pl
from jax.experimental.pallas import tpu as pltpu

F32 = jnp.float32
BF16 = jnp.bfloat16
EPS = 1e-6
CHUNK = 64

POOL_WINDOWS = (2, 4, 8, 16)
POOL_MAX = max(POOL_WINDOWS)
POOL_GDIM = 128
POOL_DIM = POOL_GDIM * len(POOL_WINDOWS)

HEAD_W = 128
DIFF_DH = 64
ROPE_DIM = DIFF_DH // 4
ROPE_THETA = 500000.0

RET_DK = 256
RET_DV = 256
RET_THETA = 10000.0

FOX_DH = 128

VMEM_LIMIT_BYTES = 56 * 1024 * 1024


def _cparams(*sem):
    return pltpu.CompilerParams(dimension_semantics=sem, vmem_limit_bytes=VMEM_LIMIT_BYTES)


def _pick(n, pref):
    t = min(pref, n)
    while n % t:
        t //= 2
    return t


def _token_tile(b, l, rows):
    if l >= rows:
        return 1, _pick(l, rows)
    return _pick(b, max(rows // l, 1)), l


def _silu(x):
    return x / (1.0 + jnp.exp(-x))


def _ada_kernel(c_ref, w_ref, b_ref, o_ref):
    c = c_ref[...]
    ca = _silu(c).astype(BF16)
    o_ref[...] = jnp.dot(ca, w_ref[...].astype(BF16), preferred_element_type=F32) + b_ref[...]


def ada_mod(c_all, w_ada, b_ada):
    depth, d, n = w_ada.shape
    mc = c_all.shape[0]
    bn = _pick(n, 1024)
    return pl.pallas_call(
        _ada_kernel,
        grid=(depth, n // bn),
        in_specs=[pl.BlockSpec((mc, d), lambda l, j: (0, 0)),
                  pl.BlockSpec((None, d, bn), lambda l, j: (l, 0, j)),
                  pl.BlockSpec((None, 1, bn), lambda l, j: (l, 0, j))],
        out_specs=pl.BlockSpec((None, mc, bn), lambda l, j: (l, 0, j)),
        out_shape=jax.ShapeDtypeStruct((depth, mc, n), F32),
        compiler_params=_cparams("arbitrary", "arbitrary"),
        name="ada_mod",
    )(c_all, w_ada, b_ada.reshape(depth, 1, n))


def _norm_mod_kernel(x_ref, g_ref, sh_ref, sc_ref, o_ref):
    x = x_ref[...]
    ms = jnp.mean(x * x, axis=-1, keepdims=True)
    y = x * lax.rsqrt(ms + EPS) * g_ref[...]
    o_ref[...] = (y * (1.0 + sc_ref[...]) + sh_ref[...]).astype(o_ref.dtype)


def norm_mod(x, g, mod, shift_idx, scale_idx):
    b, l, d = x.shape
    bb, bl = _token_tile(b, l, 512)
    return pl.pallas_call(
        _norm_mod_kernel,
        grid=(b // bb, l // bl),
        in_specs=[pl.BlockSpec((bb, bl, d), lambda ib, il: (ib, il, 0)),
                  pl.BlockSpec((1, d), lambda ib, il: (0, 0)),
                  pl.BlockSpec((bb, 1, d), lambda ib, il: (ib, 0, shift_idx)),
                  pl.BlockSpec((bb, 1, d), lambda ib, il: (ib, 0, scale_idx))],
        out_specs=pl.BlockSpec((bb, bl, d), lambda ib, il: (ib, il, 0)),
        out_shape=jax.ShapeDtypeStruct((b, l, d), BF16),
        compiler_params=_cparams("arbitrary", "arbitrary"),
        name="norm_mod",
    )(x, g.reshape(1, d), mod, mod)


def _mm_kernel(*refs, n_lhs, nk, epilogue):
    lhs = refs[:n_lhs]
    ws = refs[n_lhs:2 * n_lhs]
    pos = 2 * n_lhs
    if epilogue == "resid":
        xres_ref, gate_ref = refs[pos], refs[pos + 1]
        pos += 2
    o_ref = refs[pos]

    def compute():
        acc = None
        for a_ref, w_ref in zip(lhs, ws):
            a = a_ref[...]
            a = a.reshape(-1, a.shape[-1])
            part = jnp.dot(a, w_ref[...], preferred_element_type=F32)
            acc = part if acc is None else acc + part
        return acc

    def finish(acc):
        if epilogue == "relu2":
            r = jnp.maximum(acc, 0.0)
            y = r * r
        elif epilogue == "resid":
            y = xres_ref[...] + gate_ref[...] * acc.reshape(o_ref.shape)
        else:
            y = acc
        o_ref[...] = y.reshape(o_ref.shape).astype(o_ref.dtype)

    if nk == 1:
        finish(compute())
    else:
        acc_ref = refs[pos + 1]
        k = pl.program_id(3)

        @pl.when(k == 0)
        def _():
            acc_ref[...] = compute()

        @pl.when(k > 0)
        def _():
            acc_ref[...] += compute()

        @pl.when(k == nk - 1)
        def _():
            finish(acc_ref[...])


def matmul(lhs_list, w_list, out_dtype, epilogue="none", xres=None, mod=None, gate_idx=0,
           rows=1024, cols=1024, bk=None):
    b, l, _ = lhs_list[0].shape
    n = w_list[0].shape[1]
    bb, bl = _token_tile(b, l, rows)
    bn = _pick(n, cols)
    ks = [a.shape[-1] for a in lhs_list]
    nk = 1 if bk is None else ks[0] // bk
    assert nk == 1 or len(lhs_list) == 1
    bks = ks if nk == 1 else [bk]
    in_specs = [pl.BlockSpec((bb, bl, kk), lambda ib, il, j, k: (ib, il, k)) for kk in bks]
    in_specs += [pl.BlockSpec((kk, bn), lambda ib, il, j, k: (k, j)) for kk in bks]
    args = list(lhs_list) + list(w_list)
    if epilogue == "resid":
        gate_off = gate_idx * (n // bn)
        in_specs += [pl.BlockSpec((bb, bl, bn), lambda ib, il, j, k: (ib, il, j)),
                     pl.BlockSpec((bb, 1, bn), lambda ib, il, j, k: (ib, 0, gate_off + j))]
        args += [xres, mod]
    scratch = [pltpu.VMEM((bb * bl, bn), F32)] if nk > 1 else []
    return pl.pallas_call(
        functools.partial(_mm_kernel, n_lhs=len(lhs_list), nk=nk, epilogue=epilogue),
        grid=(b // bb, l // bl, n // bn, nk),
        in_specs=in_specs,
        out_specs=pl.BlockSpec((bb, bl, bn), lambda ib, il, j, k: (ib, il, j)),
        out_shape=jax.ShapeDtypeStruct((b, l, n), out_dtype),
        scratch_shapes=scratch,
        compiler_params=_cparams("arbitrary", "arbitrary", "arbitrary", "arbitrary"),
        name="matmul_" + epilogue,
    )(*args)


def _prep_even_kernel(proj_ref, prev_ref, c_ref, s1_ref, s2_ref, qg_ref, kg_ref, pw_ref, ps_ref,
                      pool_ref, npool_ref, q_ref, k32_ref, kb_ref, v32_ref, vb_ref, full_ref,
                      *, t, p, nl, heads):
    il = pl.program_id(1)

    @pl.when(il == 0)
    def _():
        full_ref[0:POOL_MAX, :] = prev_ref[...]

    @pl.when(il > 0)
    def _():
        full_ref[0:POOL_MAX, :] = full_ref[t:t + POOL_MAX, :]

    full_ref[POOL_MAX:POOL_MAX + t, :] = proj_ref[:, 0:POOL_DIM]
    pos1 = p + il * t + 1 + lax.broadcasted_iota(jnp.int32, (t, 1), 0)
    for g, w in enumerate(POOL_WINDOWS):
        sl = slice(g * POOL_GDIM, (g + 1) * POOL_GDIM)
        u = full_ref[POOL_MAX:POOL_MAX + t, sl]
        win = u
        for s in range(1, w):
            win = win + full_ref[POOL_MAX - s:POOL_MAX - s + t, sl]
        cnt = jnp.minimum(pos1, w).astype(F32)
        d = (win / cnt - u).astype(BF16)
        mixed = jnp.dot(d, pw_ref[g], preferred_element_type=F32) * ps_ref[:, sl]
        pool_ref[:, sl] = mixed.astype(pool_ref.dtype)

    @pl.when(il == nl - 1)
    def _():
        npool_ref[...] = full_ref[t + 1:t + POOL_MAX, :]

    seg = (lax.broadcasted_iota(jnp.int32, (HEAD_W, HEAD_W), 0) >> 6) == \
          (lax.broadcasted_iota(jnp.int32, (HEAD_W, HEAD_W), 1) >> 6)
    seg = seg.astype(F32).astype(BF16)
    rc, rs1, rs2 = c_ref[...], s1_ref[...], s2_ref[...]

    def norm_rope(x, g):
        x2 = x * x
        hi = x2.astype(BF16)
        lo = (x2 - hi.astype(F32)).astype(BF16)
        ss = (jnp.dot(lo, seg, preferred_element_type=F32)
              + jnp.dot(hi, seg, preferred_element_type=F32))
        y = x * lax.rsqrt(ss * (1.0 / DIFF_DH) + EPS) * g
        return y * rc + pltpu.roll(y, 8, 1) * rs2 + pltpu.roll(y, HEAD_W - 8, 1) * rs1

    width = heads * HEAD_W
    q_off, k_off, v_off = POOL_DIM, POOL_DIM + width, POOL_DIM + 2 * width
    qg, kg = qg_ref[...], kg_ref[...]
    for h in range(heads):
        sl = slice(h * HEAD_W, (h + 1) * HEAD_W)
        rq = norm_rope(proj_ref[:, q_off + h * HEAD_W:q_off + (h + 1) * HEAD_W], qg)
        q_ref[:, sl] = (rq * DIFF_DH ** -0.5).astype(BF16)
        rk = norm_rope(proj_ref[:, k_off + h * HEAD_W:k_off + (h + 1) * HEAD_W], kg)
        k32_ref[:, sl] = rk
        kb_ref[:, sl] = rk.astype(BF16)
    v = proj_ref[:, v_off:v_off + width]
    v32_ref[...] = v
    vb_ref[...] = v.astype(BF16)


def _diff_rope_tables(p, l):
    inv_freq = ROPE_THETA ** (-jnp.arange(0, ROPE_DIM, 2, dtype=F32) / ROPE_DIM)
    ang = (p + jnp.arange(l)).astype(F32)[:, None] * inv_freq[None, :]
    cos, sin = jnp.cos(ang), jnp.sin(ang)
    half = ROPE_DIM // 2
    rest = DIFF_DH - ROPE_DIM
    c = jnp.concatenate([cos, cos, jnp.ones((l, rest), F32)], axis=-1)
    s1 = jnp.concatenate([-sin, jnp.zeros((l, half + rest), F32)], axis=-1)
    s2 = jnp.concatenate([jnp.zeros((l, half), F32), sin, jnp.zeros((l, rest), F32)], axis=-1)
    return tuple(jnp.concatenate([a, a], axis=-1) for a in (c, s1, s2))


def prep_even(proj, pool_prev, p, qn, kn, pool_w, pool_scale):
    b, l, n_in = proj.shape
    width = (n_in - POOL_DIM) // 3
    heads = width // HEAD_W
    t = _pick(l, 256)
    nl = l // t
    prev16 = jnp.concatenate([jnp.zeros((b, 1, POOL_DIM), F32), pool_prev.astype(F32)], axis=1)
    rc, rs1, rs2 = _diff_rope_tables(p, l)
    qg = jnp.concatenate([qn, qn]).reshape(1, HEAD_W).astype(F32)
    kg = jnp.concatenate([kn, kn]).reshape(1, HEAD_W).astype(F32)
    tok = lambda w: pl.BlockSpec((None, t, w), lambda ib, il: (ib, il, 0))
    tab = pl.BlockSpec((t, HEAD_W), lambda ib, il: (il, 0))
    vec = pl.BlockSpec((1, HEAD_W), lambda ib, il: (0, 0))
    out_shapes = (
        jax.ShapeDtypeStruct((b, l, POOL_DIM), BF16),
        jax.ShapeDtypeStruct((b, POOL_MAX - 1, POOL_DIM), F32),
        jax.ShapeDtypeStruct((b, l, width), BF16),
        jax.ShapeDtypeStruct((b, l, width), F32),
        jax.ShapeDtypeStruct((b, l, width), BF16),
        jax.ShapeDtypeStruct((b, l, width), F32),
        jax.ShapeDtypeStruct((b, l, width), BF16),
    )
    return pl.pallas_call(
        functools.partial(_prep_even_kernel, t=t, p=p, nl=nl, heads=heads),
        grid=(b, nl),
        in_specs=[tok(n_in),
                  pl.BlockSpec((None, POOL_MAX, POOL_DIM), lambda ib, il: (ib, 0, 0)),
                  tab, tab, tab, vec, vec,
                  pl.BlockSpec((len(POOL_WINDOWS), POOL_GDIM, POOL_GDIM), lambda ib, il: (0, 0, 0)),
                  pl.BlockSpec((1, POOL_DIM), lambda ib, il: (0, 0))],
        out_specs=(tok(POOL_DIM),
                   pl.BlockSpec((None, POOL_MAX - 1, POOL_DIM), lambda ib, il: (ib, 0, 0)),
                   tok(width), tok(width), tok(width), tok(width), tok(width)),
        out_shape=out_shapes,
        scratch_shapes=[pltpu.VMEM((POOL_MAX + t, POOL_DIM), F32)],
        compiler_params=_cparams("arbitrary", "arbitrary"),
        name="prep_even",
    )(proj, prev16, rc, rs1, rs2, qg, kg, pool_w.astype(BF16), pool_scale.reshape(1, POOL_DIM))


def _prep_odd_kernel(fq_ref, fk_ref, fv_ref, fl_ref, qg_ref, kg_ref, fb_ref,
                     q_ref, k32_ref, kb_ref, v32_ref, vb_ref, lf_ref, *, heads):
    def rms(x, g):
        return x * lax.rsqrt(jnp.mean(x * x, axis=-1, keepdims=True) + EPS) * g

    qg, kg = qg_ref[...], kg_ref[...]
    for h in range(heads):
        sl = slice(h * HEAD_W, (h + 1) * HEAD_W)
        q_ref[:, sl] = (rms(fq_ref[:, sl], qg) * FOX_DH ** -0.5).astype(BF16)
        rk = rms(fk_ref[:, sl], kg)
        k32_ref[:, sl] = rk
        kb_ref[:, sl] = rk.astype(BF16)
    v = fv_ref[...]
    v32_ref[...] = v
    vb_ref[...] = v.astype(BF16)
    x = fl_ref[...] + fb_ref[...]
    logf = -(jnp.maximum(-x, 0.0) + jnp.log1p(jnp.exp(-jnp.abs(x))))
    lf_ref[...] = logf[:, 0:heads]


def prep_odd(proj, fl, fox_off, heads, qn, kn, f_bias):
    b, l, _ = proj.shape
    width = heads * HEAD_W
    t = _pick(l, 512)
    cb = fox_off // width
    tok = lambda w, c: pl.BlockSpec((None, t, w), lambda ib, il: (ib, il, c))
    vec = pl.BlockSpec((1, HEAD_W), lambda ib, il: (0, 0))
    fb = jnp.zeros((1, HEAD_W), F32).at[0, :heads].set(f_bias.astype(F32))
    out_shapes = (
        jax.ShapeDtypeStruct((b, l, width), BF16),
        jax.ShapeDtypeStruct((b, l, width), F32),
        jax.ShapeDtypeStruct((b, l, width), BF16),
        jax.ShapeDtypeStruct((b, l, width), F32),
        jax.ShapeDtypeStruct((b, l, width), BF16),
        jax.ShapeDtypeStruct((b, l, heads), F32),
    )
    return pl.pallas_call(
        functools.partial(_prep_odd_kernel, heads=heads),
        grid=(b, l // t),
        in_specs=[tok(width, cb), tok(width, cb + 1), tok(width, cb + 2), tok(HEAD_W, 0),
                  vec, vec, vec],
        out_specs=(tok(width, 0), tok(width, 0), tok(width, 0), tok(width, 0), tok(width, 0),
                   tok(heads, 0)),
        out_shape=out_shapes,
        compiler_params=_cparams("arbitrary", "arbitrary"),
        name="prep_odd",
    )(proj, proj, proj, fl, qn.reshape(1, HEAD_W), kn.reshape(1, HEAD_W), fb)


CUMSUM_BLOCK = 256


def _cumsum_kernel(x_ref, o_ref, carry_ref):
    @pl.when(pl.program_id(0) == 0)
    def _():
        carry_ref[...] = jnp.zeros_like(carry_ref)

    x = x_ref[...]
    x1 = x.astype(BF16)
    r1 = x - x1.astype(F32)
    x2 = r1.astype(BF16)
    x3 = (r1 - x2.astype(F32)).astype(BF16)
    n = x.shape[-1]
    tri = lax.broadcasted_iota(jnp.int32, (n, n), 0) <= lax.broadcasted_iota(jnp.int32, (n, n), 1)
    tri = tri.astype(F32).astype(BF16)
    c = (jnp.dot(x3, tri, preferred_element_type=F32)
         + jnp.dot(x2, tri, preferred_element_type=F32)
         + jnp.dot(x1, tri, preferred_element_type=F32)) + carry_ref[...]
    o_ref[...] = c
    carry_ref[...] = c[:, n - 1:n]


def cumsum_lanes(x):
    r, n = x.shape
    return pl.pallas_call(
        _cumsum_kernel,
        grid=(n // CUMSUM_BLOCK,),
        in_specs=[pl.BlockSpec((r, CUMSUM_BLOCK), lambda i: (0, i))],
        out_specs=pl.BlockSpec((r, CUMSUM_BLOCK), lambda i: (0, i)),
        out_shape=jax.ShapeDtypeStruct((r, n), F32),
        scratch_shapes=[pltpu.VMEM((r, 1), F32)],
        compiler_params=_cparams("arbitrary"),
        name="cumsum",
    )(x)


def _flash_kernel(*refs, kind, tq, tk, p, lam_init):
    if kind == "diff":
        q_ref, k_ref, v_ref, lv_ref, gn_ref, o_ref = refs
    else:
        q_ref, k_ref, v_ref, cq_ref, ck_ref, o_ref = refs
    q0 = pl.program_id(2) * tq
    q = q_ref[...]
    if kind == "diff":
        qf = q.astype(F32)
        lane = lax.broadcasted_iota(jnp.int32, (tq, HEAD_W), 1)
        qz = jnp.concatenate([jnp.where(lane < DIFF_DH, qf, 0.0),
                              jnp.where(lane >= DIFF_DH, qf, 0.0)], axis=0).astype(BF16)
        rows = 2 * tq
        n_full = lax.div(p + q0 + CHUNK, tk)
    else:
        qz = q
        rows = tq
        n_full = lax.div(p + q0 + 1, tk)
        cqb = jnp.broadcast_to(cq_ref[...], (tq, tk))
    n_tot = lax.div(p + q0 + tq + tk - 1, tk)

    def step(j, carry, masked):
        m, l, acc = carry
        start = pl.multiple_of(j * tk, tk)
        k = k_ref[pl.ds(start, tk), :]
        v = v_ref[pl.ds(start, tk), :]
        s = lax.dot_general(qz, k, (((1,), (1,)), ((), ())), preferred_element_type=F32)
        if kind == "fox":
            s = s + (cqb - ck_ref[pl.ds(j, 1), :])
        if masked:
            r = lax.broadcasted_iota(jnp.int32, (rows, tk), 0)
            kpos = j * tk + lax.broadcasted_iota(jnp.int32, (rows, tk), 1)
            if kind == "diff":
                qpos = p + q0 + jnp.where(r >= tq, r - tq, r)
                ok = (kpos >> 6) <= (qpos >> 6)
            else:
                ok = kpos <= p + q0 + r
            s = jnp.where(ok, s, -jnp.inf)
        m_new = jnp.maximum(m, jnp.max(s, axis=-1, keepdims=True))
        alpha = jnp.exp(m - m_new)
        pr = jnp.exp(s - m_new)
        l = alpha * l + jnp.sum(pr, axis=-1, keepdims=True)
        acc = alpha * acc + jnp.dot(pr.astype(BF16), v, preferred_element_type=F32)
        return m_new, l, acc

    init = (jnp.full((rows, 1), -jnp.inf, F32), jnp.zeros((rows, 1), F32),
            jnp.zeros((rows, HEAD_W), F32))
    carry = lax.fori_loop(0, n_full, functools.partial(step, masked=False), init)
    _, l, acc = lax.fori_loop(n_full, n_tot, functools.partial(step, masked=True), carry)
    o = acc / l
    if kind == "diff":
        lv = lv_ref[...]
        lam = (jnp.exp(jnp.sum(lv[0:1] * lv[1:2], axis=-1, keepdims=True))
               - jnp.exp(jnp.sum(lv[2:3] * lv[3:4], axis=-1, keepdims=True)) + lam_init)
        o = o[:tq] - lam * o[tq:]
        o = o * lax.rsqrt(jnp.mean(o * o, axis=-1, keepdims=True) + EPS) * gn_ref[...]
        o = o * (1.0 - lam_init)
    o_ref[...] = o.astype(o_ref.dtype)


def flash(kind, q, k_all, v_all, p, tq, tk, extra, lam_init=0.0):
    b, l, width = q.shape
    heads = width // HEAD_W
    ltot = k_all.shape[1]
    assert ltot % tk == 0 and l % tq == 0 and ltot >= p + l
    qspec = pl.BlockSpec((None, tq, HEAD_W), lambda ib, ih, iq: (ib, iq, ih))
    kvspec = pl.BlockSpec((None, ltot, HEAD_W), lambda ib, ih, iq: (ib, 0, ih))
    if kind == "diff":
        lam_vec, gn = extra
        especs = [pl.BlockSpec(lam_vec.shape, lambda ib, ih, iq: (0, 0)),
                  pl.BlockSpec((1, HEAD_W), lambda ib, ih, iq: (0, 0))]
        eargs = [lam_vec, gn.reshape(1, HEAD_W)]
    else:
        cq, ck = extra
        especs = [pl.BlockSpec((None, None, tq, 1), lambda ib, ih, iq: (ib, ih, iq, 0)),
                  pl.BlockSpec((None, None, ltot // tk, tk), lambda ib, ih, iq: (ib, ih, 0, 0))]
        eargs = [cq, ck]
    return pl.pallas_call(
        functools.partial(_flash_kernel, kind=kind, tq=tq, tk=tk, p=p, lam_init=lam_init),
        grid=(b, heads, l // tq),
        in_specs=[qspec, kvspec, kvspec] + especs,
        out_specs=qspec,
        out_shape=jax.ShapeDtypeStruct((b, l, width), BF16),
        compiler_params=_cparams("arbitrary", "arbitrary", "arbitrary"),
        name="flash_" + kind,
    )(q, k_all, v_all, *eargs)


def _ret_kernel(lg_ref, q_ref, k_ref, v_ref, g_ref, cos_ref, sin_ref, s0_ref, gn_ref,
                o_ref, sn_ref, s_scr, *, c, nc):
    t = pl.program_id(2)

    @pl.when(t == 0)
    def _():
        s_scr[...] = s0_ref[...]

    cos, sin = cos_ref[...], sin_ref[...]
    half = RET_DK // 2

    def rot(x):
        x1, x2 = x[:, :half], x[:, half:]
        return jnp.concatenate([x1 * cos - x2 * sin, x2 * cos + x1 * sin], axis=-1)

    q = rot(q_ref[...])
    k = rot(k_ref[...]) * RET_DK ** -0.5
    vb = v_ref[...].astype(BF16)
    lg = lg_ref[pl.program_id(1)]
    dist = (lax.broadcasted_iota(jnp.int32, (c, c), 0)
            - lax.broadcasted_iota(jnp.int32, (c, c), 1)).astype(F32)
    intra = jnp.where(dist >= 0, jnp.exp(lg * jnp.maximum(dist, 0.0)), 0.0)
    ic = lax.broadcasted_iota(jnp.int32, (c, 1), 0).astype(F32)
    q_dec = jnp.exp(lg * (ic + 1.0))
    k_dec = jnp.exp(lg * (c - 1.0 - ic))
    blk_dec = jnp.exp(lg * jnp.full((1, RET_DV), float(c), F32))
    att = lax.dot_general(q.astype(BF16), k.astype(BF16), (((1,), (1,)), ((), ())),
                          preferred_element_type=F32) * intra
    s = s_scr[...]
    o = (jnp.dot(att.astype(BF16), vb, preferred_element_type=F32)
         + jnp.dot((q * q_dec).astype(BF16), s.astype(BF16), preferred_element_type=F32))
    s_new = s * blk_dec + lax.dot_general((k * k_dec).astype(BF16), vb, (((0,), (0,)), ((), ())),
                                          preferred_element_type=F32)
    s_scr[...] = s_new
    mu = jnp.mean(o, axis=-1, keepdims=True)
    oc = o - mu
    var = jnp.mean(oc * oc, axis=-1, keepdims=True)
    y = oc * lax.rsqrt(var + EPS) * gn_ref[...]
    o_ref[...] = (y * _silu(g_ref[...])).astype(o_ref.dtype)

    @pl.when(t == nc - 1)
    def _():
        sn_ref[...] = s_new


def retention(proj, p, s0, ret_gn):
    b, l, _ = proj.shape
    heads = s0.shape[1]
    c = _pick(l, 256)
    nc = l // c
    inv_freq = RET_THETA ** (-jnp.arange(0, RET_DK, 2, dtype=F32) / RET_DK)
    ang = (p + jnp.arange(l)).astype(F32)[:, None] * inv_freq[None, :]
    cos, sin = jnp.cos(ang), jnp.sin(ang)
    lg = jnp.asarray([math.log(1.0 - 2.0 ** (-5 - h)) for h in range(heads)], F32)
    col = lambda off: pl.BlockSpec((None, c, RET_DK), lambda ib, ih, it: (ib, it, off + ih))
    tab = pl.BlockSpec((c, RET_DK // 2), lambda ib, ih, it: (it, 0))
    st = pl.BlockSpec((None, None, RET_DK, RET_DV), lambda ib, ih, it: (ib, ih, 0, 0))
    return pl.pallas_call(
        functools.partial(_ret_kernel, c=c, nc=nc),
        grid=(b, heads, nc),
        in_specs=[pl.BlockSpec(memory_space=pltpu.SMEM),
                  col(0), col(heads), col(2 * heads), col(3 * heads), tab, tab, st,
                  pl.BlockSpec((None, 1, RET_DV), lambda ib, ih, it: (ih, 0, 0))],
        out_specs=(pl.BlockSpec((None, c, RET_DV), lambda ib, ih, it: (ib, it, ih)), st),
        out_shape=(jax.ShapeDtypeStruct((b, l, heads * RET_DV), BF16),
                   jax.ShapeDtypeStruct(s0.shape, F32)),
        scratch_shapes=[pltpu.VMEM((RET_DK, RET_DV), F32)],
        compiler_params=_cparams("arbitrary", "arbitrary", "arbitrary"),
        name="retention",
    )(lg, proj, proj, proj, proj, cos, sin, s0.astype(F32), ret_gn.reshape(heads, 1, RET_DV))


def _with_past(past, new, ltot):
    b, l, width = new.shape
    parts = []
    if past.shape[1]:
        parts.append(past.reshape(b, past.shape[1], width).astype(BF16))
    parts.append(new)
    pad = ltot - l - past.shape[1]
    if pad:
        parts.append(jnp.zeros((b, pad, width), BF16))
    return parts[0] if len(parts) == 1 else jnp.concatenate(parts, axis=1)


def _attn_tiles(p, l, tq_pref, tk_pref):
    tq = _pick(l, tq_pref)
    tk = _pick(l, tk_pref) if p == 0 else 256
    ltot = -(-(p + l) // tk) * tk
    return tq, tk, ltot


def even_mixer(h, x, mod, pool_prev, past_k, past_v, w_in, w_out, pool_w, pool_scale,
               qn, kn, lam_vec, out_gn, lam_init):
    p = past_k.shape[1]
    b, l, _ = h.shape
    proj = matmul([h], [w_in], F32)
    pool_out, new_pool, q, k32, kb, v32, vb = prep_even(proj, pool_prev, p, qn, kn, pool_w, pool_scale)
    tq, tk, ltot = _attn_tiles(p, l, 256, 512)
    o = flash("diff", q, _with_past(past_k, kb, ltot), _with_past(past_v, vb, ltot), p, tq, tk,
              (lam_vec.astype(F32), out_gn.astype(F32)), lam_init)
    x = matmul([pool_out, o], [w_out[:POOL_DIM], w_out[POOL_DIM:]], F32, "resid", x, mod, 2)
    heads = k32.shape[-1] // HEAD_W
    return x, new_pool, k32.reshape(b, l, heads, HEAD_W), v32.reshape(b, l, heads, HEAD_W)


def odd_mixer(h, x, mod, s0, past_k, past_v, past_logf, w_in, w_fl, w_out, ret_gn, qn, kn, f_bias):
    p = past_k.shape[1]
    b, l, _ = h.shape
    ret_heads = s0.shape[1]
    fox_heads = f_bias.shape[0]
    proj = matmul([h], [w_in], F32)
    fl = matmul([h], [w_fl], F32)
    ret, s_new = retention(proj, p, s0, ret_gn)
    fox_off = 2 * ret_heads * RET_DK + 2 * ret_heads * RET_DV
    q, k32, kb, v32, vb, logf = prep_odd(proj, fl, fox_off, fox_heads, qn, kn, f_bias)
    tq, tk, ltot = _attn_tiles(p, l, 512, 512)
    ltot = -(-ltot // CUMSUM_BLOCK) * CUMSUM_BLOCK
    lf_all = jnp.concatenate([past_logf.astype(F32), logf,
                              jnp.zeros((b, ltot - p - l, fox_heads), F32)], axis=1)
    cum = cumsum_lanes(jnp.swapaxes(lf_all, 1, 2).reshape(b * fox_heads, ltot))
    cum = cum.reshape(b, fox_heads, ltot)
    cq = cum[:, :, p:p + l].reshape(b, fox_heads, l, 1)
    ck = cum.reshape(b, fox_heads, ltot // tk, tk)
    o = flash("fox", q, _with_past(past_k, kb, ltot), _with_past(past_v, vb, ltot), p, tq, tk, (cq, ck))
    x = matmul([ret, o], [w_out[:ret.shape[-1]], w_out[ret.shape[-1]:]], F32, "resid", x, mod, 2)
    return (x, s_new, k32.reshape(b, l, fox_heads, HEAD_W), v32.reshape(b, l, fox_heads, HEAD_W), logf)


def trunk(x, mods, pool_prev, diff_k, diff_v, ret_s, fox_k, fox_v, fox_logf, wts):
    (norm1, norm2, w_up, w_down, w_in_even, w_out_even, pool_w, pool_scale, diff_qn, diff_kn,
     diff_lam, diff_gn, w_in_odd, w_fl_odd, w_out_odd, ret_gn, fox_qn, fox_kn, fox_fbias) = wts
    depth = norm1.shape[0]
    d = x.shape[-1]
    n_pool, n_dk, n_dv, n_rs, n_fk, n_fv, n_fl = [], [], [], [], [], [], []
    for l in range(depth):
        j = l // 2
        mod = mods[l]
        h = norm_mod(x, norm1[l], mod, 0, 1)
        if l % 2 == 0:
            x, p_new, k_new, v_new = even_mixer(
                h, x, mod, pool_prev[j], diff_k[j], diff_v[j], w_in_even[j], w_out_even[j], pool_w[j],
                pool_scale[j], diff_qn[j], diff_kn[j], diff_lam[j], diff_gn[j],
                0.8 - 0.6 * math.exp(-0.3 * l))
            n_pool.append(p_new)
            n_dk.append(k_new)
            n_dv.append(v_new)
        else:
            x, s_new, k_new, v_new, lf_new = odd_mixer(
                h, x, mod, ret_s[j], fox_k[j], fox_v[j], fox_logf[j], w_in_odd[j], w_fl_odd[j],
                w_out_odd[j], ret_gn[j], fox_qn[j], fox_kn[j], fox_fbias[j])
            n_rs.append(s_new)
            n_fk.append(k_new)
            n_fv.append(v_new)
            n_fl.append(lf_new)
        h = norm_mod(x, norm2[l], mod, 3, 4)
        a = matmul([h], [w_up[l]], BF16, "relu2")
        x = matmul([a], [w_down[l]], F32, "resid", x, mod, 5, bk=_pick(a.shape[-1], 2048))
    st = jnp.stack
    return x, st(n_pool), st(n_dk), st(n_dv), st(n_rs), st(n_fk), st(n_fv), st(n_fl)


def kernel(x_prompt, x_sample, c_prompt, c_sample, cache_pool, cache_diff_k, cache_diff_v, state_ret,
           cache_fox_k, cache_fox_v, cache_fox_logf, w_ada, b_ada, norm1, norm2, w_up, w_down,
           w_in_even, w_out_even, pool_w, pool_scale, diff_qn, diff_kn, diff_lam, diff_gn,
           w_in_odd, w_out_odd, ret_gn, fox_qn, fox_kn, fox_fbias):
    bp, _, d = x_prompt.shape
    bs = x_sample.shape[0]
    n_pair = cache_pool.shape[0]
    depth = w_ada.shape[0]
    diff_heads = cache_diff_k.shape[3]
    ret_heads = state_ret.shape[2]
    fox_heads = cache_fox_k.shape[3]

    mc = -(-(bp + bs) // 16) * 16
    c_all = jnp.concatenate([c_prompt, c_sample, jnp.zeros((mc - bp - bs, d), F32)], axis=0)
    mod_all = ada_mod(c_all, w_ada, b_ada)
    mods_p = mod_all[:, :bp].reshape(depth, bp, 1, 6 * d)
    mods_s = mod_all[:, bp:bp + bs].reshape(depth, bs, 1, 6 * d)

    fox_main = w_in_odd.shape[-1] - fox_heads
    w_fl = jnp.pad(w_in_odd[:, :, fox_main:], ((0, 0), (0, 0), (0, HEAD_W - fox_heads))).astype(BF16)
    wts = (norm1, norm2, w_up.astype(BF16), w_down.astype(BF16), w_in_even.astype(BF16),
           w_out_even.astype(BF16), pool_w, pool_scale, diff_qn, diff_kn, diff_lam, diff_gn,
           w_in_odd[:, :, :fox_main].astype(BF16), w_fl, w_out_odd.astype(BF16),
           ret_gn, fox_qn, fox_kn, fox_fbias)

    dt = x_prompt.dtype
    y_p, pool_p, dk_p, dv_p, rs_p, fk_p, fv_p, fl_p = trunk(
        x_prompt, mods_p,
        jnp.zeros((n_pair, bp, POOL_MAX - 1, POOL_DIM), dt),
        jnp.zeros((n_pair, bp, 0, diff_heads, HEAD_W), dt),
        jnp.zeros((n_pair, bp, 0, diff_heads, HEAD_W), dt),
        jnp.zeros((n_pair, bp, ret_heads, RET_DK, RET_DV), F32),
        jnp.zeros((n_pair, bp, 0, fox_heads, HEAD_W), dt),
        jnp.zeros((n_pair, bp, 0, fox_heads, HEAD_W), dt),
        jnp.zeros((n_pair, bp, 0, fox_heads), F32),
        wts)
    y_s, pool_s, dk_s, dv_s, rs_s, fk_s, fv_s, fl_s = trunk(
        x_sample, mods_s, cache_pool, cache_diff_k, cache_diff_v, state_ret,
        cache_fox_k, cache_fox_v, cache_fox_logf, wts)
    return (y_p, y_s, pool_p, pool_s, dk_p, dk_s, dv_p, dv_s, rs_p, rs_s,
            fk_p, fk_s, fv_p, fv_s, fl_p, fl_s)
```

```python
import functools
import math

import jax
import jax.numpy as jnp
from jax import lax
from jax.experimental import pallas as pl
from jax.experimental.pallas import tpu as pltpu

F32 = jnp.float32
BF16 = jnp.bfloat16
EPS = 1e-6
CHUNK = 64

POOL_WINDOWS = (2, 4, 8, 16)
POOL_MAX = max(POOL_WINDOWS)
POOL_GDIM = 128
POOL_DIM = POOL_GDIM * len(POOL_WINDOWS)

HEAD_W = 128
DIFF_DH = 64
ROPE_DIM = DIFF_DH // 4
ROPE_THETA = 500000.0

RET_DK = 256
RET_DV = 256
RET_THETA = 10000.0

FOX_DH = 128

LOG2E = math.log2(math.e)
PACK_ROWS = 16

VMEM_LIMIT_BYTES = 56 * 1024 * 1024


def _cparams(*sem):
    return pltpu.CompilerParams(dimension_semantics=sem, vmem_limit_bytes=VMEM_LIMIT_BYTES)


def _pick(n, pref):
    t = min(pref, n)
    while n % t:
        t //= 2
    return t


def _token_tile(b, l, rows):
    if l >= rows:
        return 1, _pick(l, rows)
    return _pick(b, max(rows // l, 1)), l


def _silu(x):
    return x / (1.0 + jnp.exp(-x))


def _ada_kernel(c_ref, w_ref, b_ref, o_ref):
    c = c_ref[...]
    ca = _silu(c).astype(BF16)
    o_ref[...] = jnp.dot(ca, w_ref[...].astype(BF16), preferred_element_type=F32) + b_ref[...]


def ada_mod(c_all, w_ada, b_ada):
    depth, d, n = w_ada.shape
    mc = c_all.shape[0]
    bn = _pick(n, 1024)
    return pl.pallas_call(
        _ada_kernel,
        grid=(depth, n // bn),
        in_specs=[pl.BlockSpec((mc, d), lambda l, j: (0, 0)),
                  pl.BlockSpec((None, d, bn), lambda l, j: (l, 0, j)),
                  pl.BlockSpec((None, 1, bn), lambda l, j: (l, 0, j))],
        out_specs=pl.BlockSpec((None, mc, bn), lambda l, j: (l, 0, j)),
        out_shape=jax.ShapeDtypeStruct((depth, mc, n), F32),
        compiler_params=_cparams("arbitrary", "arbitrary"),
        name="ada_mod",
    )(c_all, w_ada, b_ada.reshape(depth, 1, n))


def _norm_mod_kernel(x_ref, g_ref, sh_ref, sc_ref, o_ref):
    x = x_ref[...]
    ms = jnp.mean(x * x, axis=-1, keepdims=True)
    y = x * lax.rsqrt(ms + EPS) * g_ref[...]
    o_ref[...] = (y * (1.0 + sc_ref[...]) + sh_ref[...]).astype(o_ref.dtype)


def norm_mod(x, g, mod, shift_idx, scale_idx):
    b, l, d = x.shape
    bb, bl = _token_tile(b, l, 512)
    return pl.pallas_call(
        _norm_mod_kernel,
        grid=(b // bb, l // bl),
        in_specs=[pl.BlockSpec((bb, bl, d), lambda ib, il: (ib, il, 0)),
                  pl.BlockSpec((1, d), lambda ib, il: (0, 0)),
                  pl.BlockSpec((bb, 1, d), lambda ib, il: (ib, 0, shift_idx)),
                  pl.BlockSpec((bb, 1, d), lambda ib, il: (ib, 0, scale_idx))],
        out_specs=pl.BlockSpec((bb, bl, d), lambda ib, il: (ib, il, 0)),
        out_shape=jax.ShapeDtypeStruct((b, l, d), BF16),
        compiler_params=_cparams("arbitrary", "arbitrary"),
        name="norm_mod",
    )(x, g.reshape(1, d), mod, mod)


def _mm_kernel(*refs, n_lhs, nk, epilogue):
    lhs = refs[:n_lhs]
    ws = refs[n_lhs:2 * n_lhs]
    pos = 2 * n_lhs
    if epilogue == "resid":
        xres_ref, gate_ref = refs[pos], refs[pos + 1]
        pos += 2
    o_ref = refs[pos]

    def compute():
        acc = None
        for a_ref, w_ref in zip(lhs, ws):
            a = a_ref[...]
            a = a.reshape(-1, a.shape[-1])
            part = jnp.dot(a, w_ref[...], preferred_element_type=F32)
            acc = part if acc is None else acc + part
        return acc

    def finish(acc):
        if epilogue == "relu2":
            r = jnp.maximum(acc, 0.0)
            y = r * r
        elif epilogue == "resid":
            y = xres_ref[...] + gate_ref[...] * acc.reshape(o_ref.shape)
        else:
            y = acc
        o_ref[...] = y.reshape(o_ref.shape).astype(o_ref.dtype)

    if nk == 1:
        finish(compute())
    else:
        acc_ref = refs[pos + 1]
        k = pl.program_id(3)

        @pl.when(k == 0)
        def _():
            acc_ref[...] = compute()

        @pl.when(k > 0)
        def _():
            acc_ref[...] += compute()

        @pl.when(k == nk - 1)
        def _():
            finish(acc_ref[...])


def matmul(lhs_list, w_list, out_dtype, epilogue="none", xres=None, mod=None, gate_idx=0,
           rows=1024, cols=1024, bk=None):
    b, l, _ = lhs_list[0].shape
    n = w_list[0].shape[1]
    bb, bl = _token_tile(b, l, rows)
    bn = _pick(n, cols)
    ks = [a.shape[-1] for a in lhs_list]
    nk = 1 if bk is None else ks[0] // bk
    assert nk == 1 or len(lhs_list) == 1
    bks = ks if nk == 1 else [bk]
    in_specs = [pl.BlockSpec((bb, bl, kk), lambda ib, il, j, k: (ib, il, k)) for kk in bks]
    in_specs += [pl.BlockSpec((kk, bn), lambda ib, il, j, k: (k, j)) for kk in bks]
    args = list(lhs_list) + list(w_list)
    if epilogue == "resid":
        gate_off = gate_idx * (n // bn)
        in_specs += [pl.BlockSpec((bb, bl, bn), lambda ib, il, j, k: (ib, il, j)),
                     pl.BlockSpec((bb, 1, bn), lambda ib, il, j, k: (ib, 0, gate_off + j))]
        args += [xres, mod]
    scratch = [pltpu.VMEM((bb * bl, bn), F32)] if nk > 1 else []
    return pl.pallas_call(
        functools.partial(_mm_kernel, n_lhs=len(lhs_list), nk=nk, epilogue=epilogue),
        grid=(b // bb, l // bl, n // bn, nk),
        in_specs=in_specs,
        out_specs=pl.BlockSpec((bb, bl, bn), lambda ib, il, j, k: (ib, il, j)),
        out_shape=jax.ShapeDtypeStruct((b, l, n), out_dtype),
        scratch_shapes=scratch,
        compiler_params=_cparams("arbitrary", "arbitrary", "arbitrary", "arbitrary"),
        name="matmul_" + epilogue,
    )(*args)


def _prep_even_kernel(proj_ref, prev_ref, c_ref, s1_ref, s2_ref, qg_ref, kg_ref, pw_ref, ps_ref,
                      pool_ref, npool_ref, q_ref, k32_ref, kb_ref, v32_ref, vb_ref, full_ref,
                      *, t, p, nl, heads):
    il = pl.program_id(1)

    @pl.when(il == 0)
    def _():
        full_ref[0:POOL_MAX, :] = prev_ref[...]

    @pl.when(il > 0)
    def _():
        full_ref[0:POOL_MAX, :] = full_ref[t:t + POOL_MAX, :]

    full_ref[POOL_MAX:POOL_MAX + t, :] = proj_ref[:, 0:POOL_DIM]
    pos1 = p + il * t + 1 + lax.broadcasted_iota(jnp.int32, (t, 1), 0)
    for g, w in enumerate(POOL_WINDOWS):
        sl = slice(g * POOL_GDIM, (g + 1) * POOL_GDIM)
        u = full_ref[POOL_MAX:POOL_MAX + t, sl]
        win = u
        for s in range(1, w):
            win = win + full_ref[POOL_MAX - s:POOL_MAX - s + t, sl]
        cnt = jnp.minimum(pos1, w).astype(F32)
        d = (win / cnt - u).astype(BF16)
        mixed = jnp.dot(d, pw_ref[g], preferred_element_type=F32) * ps_ref[:, sl]
        pool_ref[:, sl] = mixed.astype(pool_ref.dtype)

    @pl.when(il == nl - 1)
    def _():
        npool_ref[...] = full_ref[t + 1:t + POOL_MAX, :]

    seg = (lax.broadcasted_iota(jnp.int32, (HEAD_W, HEAD_W), 0) >> 6) == \
          (lax.broadcasted_iota(jnp.int32, (HEAD_W, HEAD_W), 1) >> 6)
    seg = seg.astype(F32).astype(BF16)
    rc, rs1, rs2 = c_ref[...], s1_ref[...], s2_ref[...]

    def norm_rope(x, g):
        x2 = x * x
        hi = x2.astype(BF16)
        lo = (x2 - hi.astype(F32)).astype(BF16)
        ss = (jnp.dot(lo, seg, preferred_element_type=F32)
              + jnp.dot(hi, seg, preferred_element_type=F32))
        y = x * lax.rsqrt(ss * (1.0 / DIFF_DH) + EPS) * g
        return y * rc + pltpu.roll(y, 8, 1) * rs2 + pltpu.roll(y, HEAD_W - 8, 1) * rs1

    width = heads * HEAD_W
    q_off, k_off, v_off = POOL_DIM, POOL_DIM + width, POOL_DIM + 2 * width
    qg, kg = qg_ref[...], kg_ref[...]
    for h in range(heads):
        sl = slice(h * HEAD_W, (h + 1) * HEAD_W)
        rq = norm_rope(proj_ref[:, q_off + h * HEAD_W:q_off + (h + 1) * HEAD_W], qg)
        q_ref[:, sl] = (rq * (DIFF_DH ** -0.5 * LOG2E)).astype(BF16)
        rk = norm_rope(proj_ref[:, k_off + h * HEAD_W:k_off + (h + 1) * HEAD_W], kg)
        k32_ref[:, sl] = rk
        kb_ref[:, sl] = rk.astype(BF16)
    v = proj_ref[:, v_off:v_off + width]
    v32_ref[...] = v
    vb_ref[...] = v.astype(BF16)


def _diff_rope_tables(p, l):
    inv_freq = ROPE_THETA ** (-jnp.arange(0, ROPE_DIM, 2, dtype=F32) / ROPE_DIM)
    ang = (p + jnp.arange(l)).astype(F32)[:, None] * inv_freq[None, :]
    cos, sin = jnp.cos(ang), jnp.sin(ang)
    half = ROPE_DIM // 2
    rest = DIFF_DH - ROPE_DIM
    c = jnp.concatenate([cos, cos, jnp.ones((l, rest), F32)], axis=-1)
    s1 = jnp.concatenate([-sin, jnp.zeros((l, half + rest), F32)], axis=-1)
    s2 = jnp.concatenate([jnp.zeros((l, half), F32), sin, jnp.zeros((l, rest), F32)], axis=-1)
    return tuple(jnp.concatenate([a, a], axis=-1) for a in (c, s1, s2))


def prep_even(proj, pool_prev, p, qn, kn, pool_w, pool_scale):
    b, l, n_in = proj.shape
    width = (n_in - POOL_DIM) // 3
    heads = width // HEAD_W
    t = _pick(l, 256)
    nl = l // t
    prev16 = jnp.concatenate([jnp.zeros((b, 1, POOL_DIM), F32), pool_prev.astype(F32)], axis=1)
    rc, rs1, rs2 = _diff_rope_tables(p, l)
    qg = jnp.concatenate([qn, qn]).reshape(1, HEAD_W).astype(F32)
    kg = jnp.concatenate([kn, kn]).reshape(1, HEAD_W).astype(F32)
    tok = lambda w: pl.BlockSpec((None, t, w), lambda ib, il: (ib, il, 0))
    tab = pl.BlockSpec((t, HEAD_W), lambda ib, il: (il, 0))
    vec = pl.BlockSpec((1, HEAD_W), lambda ib, il: (0, 0))
    out_shapes = (
        jax.ShapeDtypeStruct((b, l, POOL_DIM), BF16),
        jax.ShapeDtypeStruct((b, POOL_MAX - 1, POOL_DIM), F32),
        jax.ShapeDtypeStruct((b, l, width), BF16),
        jax.ShapeDtypeStruct((b, l, width), F32),
        jax.ShapeDtypeStruct((b, l, width), BF16),
        jax.ShapeDtypeStruct((b, l, width), F32),
        jax.ShapeDtypeStruct((b, l, width), BF16),
    )
    return pl.pallas_call(
        functools.partial(_prep_even_kernel, t=t, p=p, nl=nl, heads=heads),
        grid=(b, nl),
        in_specs=[tok(n_in),
                  pl.BlockSpec((None, POOL_MAX, POOL_DIM), lambda ib, il: (ib, 0, 0)),
                  tab, tab, tab, vec, vec,
                  pl.BlockSpec((len(POOL_WINDOWS), POOL_GDIM, POOL_GDIM), lambda ib, il: (0, 0, 0)),
                  pl.BlockSpec((1, POOL_DIM), lambda ib, il: (0, 0))],
        out_specs=(tok(POOL_DIM),
                   pl.BlockSpec((None, POOL_MAX - 1, POOL_DIM), lambda ib, il: (ib, 0, 0)),
                   tok(width), tok(width), tok(width), tok(width), tok(width)),
        out_shape=out_shapes,
        scratch_shapes=[pltpu.VMEM((POOL_MAX + t, POOL_DIM), F32)],
        compiler_params=_cparams("arbitrary", "arbitrary"),
        name="prep_even",
    )(proj, prev16, rc, rs1, rs2, qg, kg, pool_w.astype(BF16), pool_scale.reshape(1, POOL_DIM))


def _prep_odd_kernel(fq_ref, fk_ref, fv_ref, fl_ref, qg_ref, kg_ref, fb_ref,
                     q_ref, k32_ref, kb_ref, v32_ref, vb_ref, lf_ref, *, heads):
    def rms(x, g):
        return x * lax.rsqrt(jnp.mean(x * x, axis=-1, keepdims=True) + EPS) * g

    qg, kg = qg_ref[...], kg_ref[...]
    for h in range(heads):
        sl = slice(h * HEAD_W, (h + 1) * HEAD_W)
        q_ref[:, sl] = (rms(fq_ref[:, sl], qg) * (FOX_DH ** -0.5 * LOG2E)).astype(BF16)
        rk = rms(fk_ref[:, sl], kg)
        k32_ref[:, sl] = rk
        kb_ref[:, sl] = rk.astype(BF16)
    v = fv_ref[...]
    v32_ref[...] = v
    vb_ref[...] = v.astype(BF16)
    x = fl_ref[...] + fb_ref[...]
    logf = -(jnp.maximum(-x, 0.0) + jnp.log1p(jnp.exp(-jnp.abs(x))))
    lf_ref[...] = logf[:, 0:heads]


def prep_odd(proj, fl, fox_off, heads, qn, kn, f_bias):
    b, l, _ = proj.shape
    width = heads * HEAD_W
    t = _pick(l, 512)
    cb = fox_off // width
    tok = lambda w, c: pl.BlockSpec((None, t, w), lambda ib, il: (ib, il, c))
    vec = pl.BlockSpec((1, HEAD_W), lambda ib, il: (0, 0))
    fb = jnp.zeros((1, HEAD_W), F32).at[0, :heads].set(f_bias.astype(F32))
    out_shapes = (
        jax.ShapeDtypeStruct((b, l, width), BF16),
        jax.ShapeDtypeStruct((b, l, width), F32),
        jax.ShapeDtypeStruct((b, l, width), BF16),
        jax.ShapeDtypeStruct((b, l, width), F32),
        jax.ShapeDtypeStruct((b, l, width), BF16),
        jax.ShapeDtypeStruct((b, l, heads), F32),
    )
    return pl.pallas_call(
        functools.partial(_prep_odd_kernel, heads=heads),
        grid=(b, l // t),
        in_specs=[tok(width, cb), tok(width, cb + 1), tok(width, cb + 2), tok(HEAD_W, 0),
                  vec, vec, vec],
        out_specs=(tok(width, 0), tok(width, 0), tok(width, 0), tok(width, 0), tok(width, 0),
                   tok(heads, 0)),
        out_shape=out_shapes,
        compiler_params=_cparams("arbitrary", "arbitrary"),
        name="prep_odd",
    )(proj, proj, proj, fl, qn.reshape(1, HEAD_W), kn.reshape(1, HEAD_W), fb)


CUMSUM_BLOCK = 256


def _cumsum_kernel(x_ref, o_ref, carry_ref):
    @pl.when(pl.program_id(0) == 0)
    def _():
        carry_ref[...] = jnp.zeros_like(carry_ref)

    x = x_ref[...]
    x1 = x.astype(BF16)
    r1 = x - x1.astype(F32)
    x2 = r1.astype(BF16)
    x3 = (r1 - x2.astype(F32)).astype(BF16)
    n = x.shape[-1]
    tri = lax.broadcasted_iota(jnp.int32, (n, n), 0) <= lax.broadcasted_iota(jnp.int32, (n, n), 1)
    tri = tri.astype(F32).astype(BF16)
    c = (jnp.dot(x3, tri, preferred_element_type=F32)
         + jnp.dot(x2, tri, preferred_element_type=F32)
         + jnp.dot(x1, tri, preferred_element_type=F32)) + carry_ref[...]
    o_ref[...] = c
    carry_ref[...] = c[:, n - 1:n]


def cumsum_lanes(x):
    r, n = x.shape
    return pl.pallas_call(
        _cumsum_kernel,
        grid=(n // CUMSUM_BLOCK,),
        in_specs=[pl.BlockSpec((r, CUMSUM_BLOCK), lambda i: (0, i))],
        out_specs=pl.BlockSpec((r, CUMSUM_BLOCK), lambda i: (0, i)),
        out_shape=jax.ShapeDtypeStruct((r, n), F32),
        scratch_shapes=[pltpu.VMEM((r, 1), F32)],
        compiler_params=_cparams("arbitrary"),
        name="cumsum",
    )(x)


def _flash_kernel(*refs, kind, tq, tk, p, lam_init, rq, single_tile):
    if kind == "diff":
        (q_ref, k_ref, v_ref, lv_ref, gn_ref, o_ref,
         qzt_scr, s_scr, mx_scr, p_scr, acc_scr, l_scr) = refs
    else:
        (q_ref, k_ref, v_ref, cq_ref, ck_ref, o_ref,
         qzt_scr, s_scr, mx_scr, p_scr, acc_scr, l_scr, ckcol_ref) = refs
    iq = pl.program_id(2)
    q0 = iq * tq
    qf = q_ref[...].astype(F32)
    if kind == "diff":
        lane = lax.broadcasted_iota(jnp.int32, (tq, HEAD_W), 1)
        qz = jnp.concatenate([jnp.where(lane < DIFF_DH, qf, 0.0),
                              jnp.where(lane >= DIFF_DH, qf, 0.0)], axis=0)
    else:
        qz = qf if rq == tq else jnp.concatenate([qf, jnp.zeros((rq - tq, HEAD_W), F32)], axis=0)
        cq = cq_ref[pl.ds(iq, 1), :] * LOG2E

        @pl.when(iq == 0)
        def _():
            def fill(c, carry):
                row = jnp.broadcast_to(ck_ref[pl.ds(c, 1), :] * LOG2E, (HEAD_W, HEAD_W))
                ckcol_ref[pl.ds(pl.multiple_of(c * HEAD_W, HEAD_W), HEAD_W), :] = row.T
                return carry
            lax.fori_loop(0, ck_ref.shape[0], fill, 0)

    qzt_scr[...] = qz.T.astype(BF16)
    acc_scr[...] = jnp.zeros_like(acc_scr)

    def produce(slot, j, masked):
        start = pl.multiple_of(j * tk, tk)
        s = jnp.dot(k_ref[pl.ds(start, tk), :], qzt_scr[...], preferred_element_type=F32)
        if kind == "fox":
            ckc = ckcol_ref[pl.ds(start, tk), :]
            s = jnp.concatenate([s[:, c * HEAD_W:(c + 1) * HEAD_W] - ckc
                                 for c in range(rq // HEAD_W)], axis=1)
        if masked:
            kpos = start + lax.broadcasted_iota(jnp.int32, (tk, rq), 0)
            r = lax.broadcasted_iota(jnp.int32, (tk, rq), 1)
            if kind == "diff":
                qpos = p + q0 + jnp.where(r >= tq, r - tq, r)
                ok = (kpos >> 6) <= (qpos >> 6)
            else:
                ok = kpos <= p + q0 + r
            s = jnp.where(ok, s, -jnp.inf)
        s_scr[slot] = s
        mx_scr[slot] = jnp.max(s, axis=0, keepdims=True)

    def consume(slot, j, m, l):
        tmax = mx_scr[slot]
        if kind == "fox":
            m_new = jnp.maximum(m, tmax + cq)
            shift = cq - m_new
        else:
            m_new = jnp.maximum(m, tmax)
            shift = -m_new
        alpha = jnp.exp2(m - m_new)
        shift_b = jnp.broadcast_to(shift, (PACK_ROWS, rq))
        lsum = jnp.zeros((PACK_ROWS, rq), F32)
        for c in range(tk // PACK_ROWS):
            rows = slice(c * PACK_ROWS, (c + 1) * PACK_ROWS)
            pr = jnp.exp2(s_scr[slot, rows, :] + shift_b)
            lsum = lsum + pr
            p_scr[slot, rows, :] = pr.astype(BF16)
        l = alpha * l + jnp.sum(lsum, axis=0, keepdims=True)
        start = pl.multiple_of(j * tk, tk)
        pv = lax.dot_general(v_ref[pl.ds(start, tk), :], p_scr[slot], (((0,), (0,)), ((), ())),
                             preferred_element_type=F32)
        acc_scr[...] = acc_scr[...] * alpha + pv
        return m_new, l

    m0 = jnp.full((1, rq), -jnp.inf, F32)
    l0 = jnp.zeros((1, rq), F32)
    if single_tile:
        produce(0, 0, True)
        l_scr[...] = consume(0, 0, m0, l0)[1]
    else:
        n_full = lax.div(p + q0 + tq + tk - 1, tk) - 1
        n_pairs = lax.div(jnp.maximum(n_full - 1, 0), 2)

        @pl.when(n_full > 0)
        def _():
            produce(0, 0, False)

        def pair(t, carry):
            m, l = carry
            j = 2 * t
            produce(1, j + 1, False)
            m, l = consume(0, j, m, l)
            produce(0, j + 2, False)
            return consume(1, j + 1, m, l)

        m, l = lax.fori_loop(0, n_pairs, pair, (m0, l0))
        odd = (n_full & 1) == 1

        @pl.when(n_full == 0)
        def _():
            produce(0, 0, True)
            l_scr[...] = consume(0, 0, m, l)[1]

        @pl.when(jnp.logical_and(n_full > 0, odd))
        def _():
            produce(1, n_full, True)
            m1, l1 = consume(0, n_full - 1, m, l)
            l_scr[...] = consume(1, n_full, m1, l1)[1]

        @pl.when(jnp.logical_and(n_full > 0, jnp.logical_not(odd)))
        def _():
            produce(1, n_full - 1, False)
            m1, l1 = consume(0, n_full - 2, m, l)
            produce(0, n_full, True)
            m2, l2 = consume(1, n_full - 1, m1, l1)
            l_scr[...] = consume(0, n_full, m2, l2)[1]

    o = (acc_scr[...] / l_scr[...]).T
    if kind == "diff":
        lv = lv_ref[...]
        lam = (jnp.exp(jnp.sum(lv[0:1] * lv[1:2], axis=-1, keepdims=True))
               - jnp.exp(jnp.sum(lv[2:3] * lv[3:4], axis=-1, keepdims=True)) + lam_init)
        o = o[:tq] - lam * o[tq:]
        o = o * lax.rsqrt(jnp.mean(o * o, axis=-1, keepdims=True) + EPS) * gn_ref[...]
        o = o * (1.0 - lam_init)
    else:
        o = o[:tq]
    o_ref[...] = o.astype(o_ref.dtype)


def flash(kind, q, k_all, v_all, p, tq, tk, extra, lam_init=0.0):
    b, l, width = q.shape
    heads = width // HEAD_W
    ltot = k_all.shape[1]
    assert ltot % tk == 0 and l % tq == 0 and ltot >= p + l and ltot % HEAD_W == 0
    assert ltot == tk or (p == 0 and tk % tq == 0)
    qspec = pl.BlockSpec((None, tq, HEAD_W), lambda ib, ih, iq: (ib, iq, ih))
    kvspec = pl.BlockSpec((None, ltot, HEAD_W), lambda ib, ih, iq: (ib, 0, ih))
    rq = 2 * tq if kind == "diff" else max(tq, HEAD_W)
    scratch = [pltpu.VMEM((HEAD_W, rq), BF16),
               pltpu.VMEM((2, tk, rq), F32),
               pltpu.VMEM((2, 1, rq), F32),
               pltpu.VMEM((2, tk, rq), BF16),
               pltpu.VMEM((HEAD_W, rq), F32),
               pltpu.VMEM((1, rq), F32)]
    if kind == "diff":
        lam_vec, gn = extra
        especs = [pl.BlockSpec(lam_vec.shape, lambda ib, ih, iq: (0, 0)),
                  pl.BlockSpec((1, HEAD_W), lambda ib, ih, iq: (0, 0))]
        eargs = [lam_vec, gn.reshape(1, HEAD_W)]
    else:
        cum = extra
        cq = cum[:, :, p:p + l].reshape(b, heads, l // tq, tq)
        if rq > tq:
            cq = jnp.pad(cq, ((0, 0), (0, 0), (0, 0), (0, rq - tq)))
        ck = cum.reshape(b, heads, ltot // HEAD_W, HEAD_W)
        especs = [pl.BlockSpec((None, None, l // tq, rq), lambda ib, ih, iq: (ib, ih, 0, 0)),
                  pl.BlockSpec((None, None, ltot // HEAD_W, HEAD_W), lambda ib, ih, iq: (ib, ih, 0, 0))]
        eargs = [cq, ck]
        scratch.append(pltpu.VMEM((ltot, HEAD_W), F32))
    return pl.pallas_call(
        functools.partial(_flash_kernel, kind=kind, tq=tq, tk=tk, p=p, lam_init=lam_init, rq=rq,
                          single_tile=(ltot == tk)),
        grid=(b, heads, l // tq),
        in_specs=[qspec, kvspec, kvspec] + especs,
        out_specs=qspec,
        out_shape=jax.ShapeDtypeStruct((b, l, width), BF16),
        scratch_shapes=scratch,
        compiler_params=_cparams("arbitrary", "arbitrary", "arbitrary"),
        name="flash_" + kind,
    )(q, k_all, v_all, *eargs)


def _ret_kernel(lg_ref, q_ref, k_ref, v_ref, g_ref, cos_ref, sin_ref, s0_ref, gn_ref,
                o_ref, sn_ref, s_scr, *, c, nc):
    t = pl.program_id(2)

    @pl.when(t == 0)
    def _():
        s_scr[...] = s0_ref[...]

    cos, sin = cos_ref[...], sin_ref[...]
    half = RET_DK // 2

    def rot(x):
        x1, x2 = x[:, :half], x[:, half:]
        return jnp.concatenate([x1 * cos - x2 * sin, x2 * cos + x1 * sin], axis=-1)

    q = rot(q_ref[...])
    k = rot(k_ref[...]) * RET_DK ** -0.5
    vb = v_ref[...].astype(BF16)
    lg = lg_ref[pl.program_id(1)]
    dist = (lax.broadcasted_iota(jnp.int32, (c, c), 0)
            - lax.broadcasted_iota(jnp.int32, (c, c), 1)).astype(F32)
    intra = jnp.where(dist >= 0, jnp.exp(lg * jnp.maximum(dist, 0.0)), 0.0)
    ic = lax.broadcasted_iota(jnp.int32, (c, 1), 0).astype(F32)
    q_dec = jnp.exp(lg * (ic + 1.0))
    k_dec = jnp.exp(lg * (c - 1.0 - ic))
    blk_dec = jnp.exp(lg * jnp.full((1, RET_DV), float(c), F32))
    att = lax.dot_general(q.astype(BF16), k.astype(BF16), (((1,), (1,)), ((), ())),
                          preferred_element_type=F32) * intra
    s = s_scr[...]
    o = (jnp.dot(att.astype(BF16), vb, preferred_element_type=F32)
         + jnp.dot((q * q_dec).astype(BF16), s.astype(BF16), preferred_element_type=F32))
    s_new = s * blk_dec + lax.dot_general((k * k_dec).astype(BF16), vb, (((0,), (0,)), ((), ())),
                                          preferred_element_type=F32)
    s_scr[...] = s_new
    mu = jnp.mean(o, axis=-1, keepdims=True)
    oc = o - mu
    var = jnp.mean(oc * oc, axis=-1, keepdims=True)
    y = oc * lax.rsqrt(var + EPS) * gn_ref[...]
    o_ref[...] = (y * _silu(g_ref[...])).astype(o_ref.dtype)

    @pl.when(t == nc - 1)
    def _():
        sn_ref[...] = s_new


def retention(proj, p, s0, ret_gn):
    b, l, _ = proj.shape
    heads = s0.shape[1]
    c = _pick(l, 256)
    nc = l // c
    inv_freq = RET_THETA ** (-jnp.arange(0, RET_DK, 2, dtype=F32) / RET_DK)
    ang = (p + jnp.arange(l)).astype(F32)[:, None] * inv_freq[None, :]
    cos, sin = jnp.cos(ang), jnp.sin(ang)
    lg = jnp.asarray([math.log(1.0 - 2.0 ** (-5 - h)) for h in range(heads)], F32)
    col = lambda off: pl.BlockSpec((None, c, RET_DK), lambda ib, ih, it: (ib, it, off + ih))
    tab = pl.BlockSpec((c, RET_DK // 2), lambda ib, ih, it: (it, 0))
    st = pl.BlockSpec((None, None, RET_DK, RET_DV), lambda ib, ih, it: (ib, ih, 0, 0))
    return pl.pallas_call(
        functools.partial(_ret_kernel, c=c, nc=nc),
        grid=(b, heads, nc),
        in_specs=[pl.BlockSpec(memory_space=pltpu.SMEM),
                  col(0), col(heads), col(2 * heads), col(3 * heads), tab, tab, st,
                  pl.BlockSpec((None, 1, RET_DV), lambda ib, ih, it: (ih, 0, 0))],
        out_specs=(pl.BlockSpec((None, c, RET_DV), lambda ib, ih, it: (ib, it, ih)), st),
        out_shape=(jax.ShapeDtypeStruct((b, l, heads * RET_DV), BF16),
                   jax.ShapeDtypeStruct(s0.shape, F32)),
        scratch_shapes=[pltpu.VMEM((RET_DK, RET_DV), F32)],
        compiler_params=_cparams("arbitrary", "arbitrary", "arbitrary"),
        name="retention",
    )(lg, proj, proj, proj, proj, cos, sin, s0.astype(F32), ret_gn.reshape(heads, 1, RET_DV))


def _with_past(past, new, ltot):
    b, l, width = new.shape
    parts = []
    if past.shape[1]:
        parts.append(past.reshape(b, past.shape[1], width).astype(BF16))
    parts.append(new)
    pad = ltot - l - past.shape[1]
    if pad:
        parts.append(jnp.zeros((b, pad, width), BF16))
    return parts[0] if len(parts) == 1 else jnp.concatenate(parts, axis=1)


def _attn_tiles(p, l, tq_pref, tk_pref):
    tq = _pick(l, tq_pref)
    if p == 0:
        return tq, _pick(l, tk_pref), l
    ltot = -(-(p + l) // CUMSUM_BLOCK) * CUMSUM_BLOCK
    return tq, ltot, ltot


def even_mixer(h, x, mod, pool_prev, past_k, past_v, w_in, w_out, pool_w, pool_scale,
               qn, kn, lam_vec, out_gn, lam_init):
    p = past_k.shape[1]
    b, l, _ = h.shape
    proj = matmul([h], [w_in], F32)
    pool_out, new_pool, q, k32, kb, v32, vb = prep_even(proj, pool_prev, p, qn, kn, pool_w, pool_scale)
    tq, tk, ltot = _attn_tiles(p, l, 256, 512)
    o = flash("diff", q, _with_past(past_k, kb, ltot), _with_past(past_v, vb, ltot), p, tq, tk,
              (lam_vec.astype(F32), out_gn.astype(F32)), lam_init)
    x = matmul([pool_out, o], [w_out[:POOL_DIM], w_out[POOL_DIM:]], F32, "resid", x, mod, 2)
    heads = k32.shape[-1] // HEAD_W
    return x, new_pool, k32.reshape(b, l, heads, HEAD_W), v32.reshape(b, l, heads, HEAD_W)


def odd_mixer(h, x, mod, s0, past_k, past_v, past_logf, w_in, w_fl, w_out, ret_gn, qn, kn, f_bias):
    p = past_k.shape[1]
    b, l, _ = h.shape
    ret_heads = s0.shape[1]
    fox_heads = f_bias.shape[0]
    proj = matmul([h], [w_in], F32)
    fl = matmul([h], [w_fl], F32)
    ret, s_new = retention(proj, p, s0, ret_gn)
    fox_off = 2 * ret_heads * RET_DK + 2 * ret_heads * RET_DV
    q, k32, kb, v32, vb, logf = prep_odd(proj, fl, fox_off, fox_heads, qn, kn, f_bias)
    tq, tk, ltot = _attn_tiles(p, l, 512, 512)
    lf_all = jnp.concatenate([past_logf.astype(F32), logf,
                              jnp.zeros((b, ltot - p - l, fox_heads), F32)], axis=1)
    cum = cumsum_lanes(jnp.swapaxes(lf_all, 1, 2).reshape(b * fox_heads, ltot))
    cum = cum.reshape(b, fox_heads, ltot)
    o = flash("fox", q, _with_past(past_k, kb, ltot), _with_past(past_v, vb, ltot), p, tq, tk, cum)
    x = matmul([ret, o], [w_out[:ret.shape[-1]], w_out[ret.shape[-1]:]], F32, "resid", x, mod, 2)
    return (x, s_new, k32.reshape(b, l, fox_heads, HEAD_W), v32.reshape(b, l, fox_heads, HEAD_W), logf)


def trunk(x, mods, pool_prev, diff_k, diff_v, ret_s, fox_k, fox_v, fox_logf, wts):
    (norm1, norm2, w_up, w_down, w_in_even, w_out_even, pool_w, pool_scale, diff_qn, diff_kn,
     diff_lam, diff_gn, w_in_odd, w_fl_odd, w_out_odd, ret_gn, fox_qn, fox_kn, fox_fbias) = wts
    depth = norm1.shape[0]
    d = x.shape[-1]
    n_pool, n_dk, n_dv, n_rs, n_fk, n_fv, n_fl = [], [], [], [], [], [], []
    for l in range(depth):
        j = l // 2
        mod = mods[l]
        h = norm_mod(x, norm1[l], mod, 0, 1)
        if l % 2 == 0:
            x, p_new, k_new, v_new = even_mixer(
                h, x, mod, pool_prev[j], diff_k[j], diff_v[j], w_in_even[j], w_out_even[j], pool_w[j],
                pool_scale[j], diff_qn[j], diff_kn[j], diff_lam[j], diff_gn[j],
                0.8 - 0.6 * math.exp(-0.3 * l))
            n_pool.append(p_new)
            n_dk.append(k_new)
            n_dv.append(v_new)
        else:
            x, s_new, k_new, v_new, lf_new = odd_mixer(
                h, x, mod, ret_s[j], fox_k[j], fox_v[j], fox_logf[j], w_in_odd[j], w_fl_odd[j],
                w_out_odd[j], ret_gn[j], fox_qn[j], fox_kn[j], fox_fbias[j])
            n_rs.append(s_new)
            n_fk.append(k_new)
            n_fv.append(v_new)
            n_fl.append(lf_new)
        h = norm_mod(x, norm2[l], mod, 3, 4)
        a = matmul([h], [w_up[l]], BF16, "relu2")
        x = matmul([a], [w_down[l]], F32, "resid", x, mod, 5, bk=_pick(a.shape[-1], 2048))
    st = jnp.stack
    return x, st(n_pool), st(n_dk), st(n_dv), st(n_rs), st(n_fk), st(n_fv), st(n_fl)


def kernel(x_prompt, x_sample, c_prompt, c_sample, cache_pool, cache_diff_k, cache_diff_v, state_ret,
           cache_fox_k, cache_fox_v, cache_fox_logf, w_ada, b_ada, norm1, norm2, w_up, w_down,
           w_in_even, w_out_even, pool_w, pool_scale, diff_qn, diff_kn, diff_lam, diff_gn,
           w_in_odd, w_out_odd, ret_gn, fox_qn, fox_kn, fox_fbias):
    bp, _, d = x_prompt.shape
    bs = x_sample.shape[0]
    n_pair = cache_pool.shape[0]
    depth = w_ada.shape[0]
    diff_heads = cache_diff_k.shape[3]
    ret_heads = state_ret.shape[2]
    fox_heads = cache_fox_k.shape[3]

    mc = -(-(bp + bs) // 16) * 16
    c_all = jnp.concatenate([c_prompt, c_sample, jnp.zeros((mc - bp - bs, d), F32)], axis=0)
    mod_all = ada_mod(c_all, w_ada, b_ada)
    mods_p = mod_all[:, :bp].reshape(depth, bp, 1, 6 * d)
    mods_s = mod_all[:, bp:bp + bs].reshape(depth, bs, 1, 6 * d)

    fox_main = w_in_odd.shape[-1] - fox_heads
    w_fl = jnp.pad(w_in_odd[:, :, fox_main:], ((0, 0), (0, 0), (0, HEAD_W - fox_heads))).astype(BF16)
    wts = (norm1, norm2, w_up.astype(BF16), w_down.astype(BF16), w_in_even.astype(BF16),
           w_out_even.astype(BF16), pool_w, pool_scale, diff_qn, diff_kn, diff_lam, diff_gn,
           w_in_odd[:, :, :fox_main].astype(BF16), w_fl, w_out_odd.astype(BF16),
           ret_gn, fox_qn, fox_kn, fox_fbias)

    dt = x_prompt.dtype
    y_p, pool_p, dk_p, dv_p, rs_p, fk_p, fv_p, fl_p = trunk(
        x_prompt, mods_p,
        jnp.zeros((n_pair, bp, POOL_MAX - 1, POOL_DIM), dt),
        jnp.zeros((n_pair, bp, 0, diff_heads, HEAD_W), dt),
        jnp.zeros((n_pair, bp, 0, diff_heads, HEAD_W), dt),
        jnp.zeros((n_pair, bp, ret_heads, RET_DK, RET_DV), F32),
        jnp.zeros((n_pair, bp, 0, fox_heads, HEAD_W), dt),
        jnp.zeros((n_pair, bp, 0, fox_heads, HEAD_W), dt),
        jnp.zeros((n_pair, bp, 0, fox_heads), F32),
        wts)
    y_s, pool_s, dk_s, dv_s, rs_s, fk_s, fv_s, fl_s = trunk(
        x_sample, mods_s, cache_pool, cache_diff_k, cache_diff_v, state_ret,
        cache_fox_k, cache_fox_v, cache_fox_logf, wts)
    return (y_p, y_s, pool_p, pool_s, dk_p, dk_s, dv_p, dv_s, rs_p, rs_s,
            fk_p, fk_s, fv_p, fv_s, fl_p, fl_s)
```

```python
import functools
import math

import jax
import jax.numpy as jnp
from jax import lax
from jax.experimental import pallas as pl
from jax.experimental.pallas import tpu as pltpu

F32 = jnp.float32
BF16 = jnp.bfloat16
EPS = 1e-6
CHUNK = 64

POOL_WINDOWS = (2, 4, 8, 16)
POOL_MAX = max(POOL_WINDOWS)
POOL_GDIM = 128
POOL_DIM = POOL_GDIM * len(POOL_WINDOWS)

HEAD_W = 128
DIFF_DH = 64
ROPE_DIM = DIFF_DH // 4
ROPE_THETA = 500000.0

RET_DK = 256
RET_DV = 256
RET_THETA = 10000.0

FOX_DH = 128

LOG2E = math.log2(math.e)
PACK_ROWS = 16
FLASH_UNROLL = 2

VMEM_LIMIT_BYTES = 56 * 1024 * 1024


def _cparams(*sem):
    return pltpu.CompilerParams(dimension_semantics=sem, vmem_limit_bytes=VMEM_LIMIT_BYTES)


def _pick(n, pref):
    t = min(pref, n)
    while n % t:
        t //= 2
    return t


def _token_tile(b, l, rows):
    if l >= rows:
        return 1, _pick(l, rows)
    return _pick(b, max(rows // l, 1)), l


def _silu(x):
    return x / (1.0 + jnp.exp(-x))


def _ada_kernel(c_ref, w_ref, b_ref, o_ref):
    c = c_ref[...]
    ca = _silu(c).astype(BF16)
    o_ref[...] = jnp.dot(ca, w_ref[...].astype(BF16), preferred_element_type=F32) + b_ref[...]


def ada_mod(c_all, w_ada, b_ada):
    depth, d, n = w_ada.shape
    mc = c_all.shape[0]
    bn = _pick(n, 1024)
    return pl.pallas_call(
        _ada_kernel,
        grid=(depth, n // bn),
        in_specs=[pl.BlockSpec((mc, d), lambda l, j: (0, 0)),
                  pl.BlockSpec((None, d, bn), lambda l, j: (l, 0, j)),
                  pl.BlockSpec((None, 1, bn), lambda l, j: (l, 0, j))],
        out_specs=pl.BlockSpec((None, mc, bn), lambda l, j: (l, 0, j)),
        out_shape=jax.ShapeDtypeStruct((depth, mc, n), F32),
        compiler_params=_cparams("arbitrary", "arbitrary"),
        name="ada_mod",
    )(c_all, w_ada, b_ada.reshape(depth, 1, n))


def _norm_mod_kernel(x_ref, g_ref, sh_ref, sc_ref, o_ref):
    x = x_ref[...]
    ms = jnp.mean(x * x, axis=-1, keepdims=True)
    y = x * lax.rsqrt(ms + EPS) * g_ref[...]
    o_ref[...] = (y * (1.0 + sc_ref[...]) + sh_ref[...]).astype(o_ref.dtype)


def norm_mod(x, g, mod, shift_idx, scale_idx):
    b, l, d = x.shape
    bb, bl = _token_tile(b, l, 512)
    return pl.pallas_call(
        _norm_mod_kernel,
        grid=(b // bb, l // bl),
        in_specs=[pl.BlockSpec((bb, bl, d), lambda ib, il: (ib, il, 0)),
                  pl.BlockSpec((1, d), lambda ib, il: (0, 0)),
                  pl.BlockSpec((bb, 1, d), lambda ib, il: (ib, 0, shift_idx)),
                  pl.BlockSpec((bb, 1, d), lambda ib, il: (ib, 0, scale_idx))],
        out_specs=pl.BlockSpec((bb, bl, d), lambda ib, il: (ib, il, 0)),
        out_shape=jax.ShapeDtypeStruct((b, l, d), BF16),
        compiler_params=_cparams("arbitrary", "arbitrary"),
        name="norm_mod",
    )(x, g.reshape(1, d), mod, mod)


def _mm_kernel(*refs, n_lhs, nk, epilogue):
    lhs = refs[:n_lhs]
    ws = refs[n_lhs:2 * n_lhs]
    pos = 2 * n_lhs
    if epilogue == "resid":
        xres_ref, gate_ref = refs[pos], refs[pos + 1]
        pos += 2
    o_ref = refs[pos]

    def compute():
        acc = None
        for a_ref, w_ref in zip(lhs, ws):
            a = a_ref[...]
            a = a.reshape(-1, a.shape[-1])
            part = jnp.dot(a, w_ref[...], preferred_element_type=F32)
            acc = part if acc is None else acc + part
        return acc

    def finish(acc):
        if epilogue == "relu2":
            r = jnp.maximum(acc, 0.0)
            y = r * r
        elif epilogue == "resid":
            y = xres_ref[...] + gate_ref[...] * acc.reshape(o_ref.shape)
        else:
            y = acc
        o_ref[...] = y.reshape(o_ref.shape).astype(o_ref.dtype)

    if nk == 1:
        finish(compute())
    else:
        acc_ref = refs[pos + 1]
        k = pl.program_id(3)

        @pl.when(k == 0)
        def _():
            acc_ref[...] = compute()

        @pl.when(k > 0)
        def _():
            acc_ref[...] += compute()

        @pl.when(k == nk - 1)
        def _():
            finish(acc_ref[...])


def matmul(lhs_list, w_list, out_dtype, epilogue="none", xres=None, mod=None, gate_idx=0,
           rows=1024, cols=1024, bk=None):
    b, l, _ = lhs_list[0].shape
    n = w_list[0].shape[1]
    bb, bl = _token_tile(b, l, rows)
    bn = _pick(n, cols)
    ks = [a.shape[-1] for a in lhs_list]
    nk = 1 if bk is None else ks[0] // bk
    assert nk == 1 or len(lhs_list) == 1
    bks = ks if nk == 1 else [bk]
    in_specs = [pl.BlockSpec((bb, bl, kk), lambda ib, il, j, k: (ib, il, k)) for kk in bks]
    in_specs += [pl.BlockSpec((kk, bn), lambda ib, il, j, k: (k, j)) for kk in bks]
    args = list(lhs_list) + list(w_list)
    if epilogue == "resid":
        gate_off = gate_idx * (n // bn)
        in_specs += [pl.BlockSpec((bb, bl, bn), lambda ib, il, j, k: (ib, il, j)),
                     pl.BlockSpec((bb, 1, bn), lambda ib, il, j, k: (ib, 0, gate_off + j))]
        args += [xres, mod]
    scratch = [pltpu.VMEM((bb * bl, bn), F32)] if nk > 1 else []
    return pl.pallas_call(
        functools.partial(_mm_kernel, n_lhs=len(lhs_list), nk=nk, epilogue=epilogue),
        grid=(b // bb, l // bl, n // bn, nk),
        in_specs=in_specs,
        out_specs=pl.BlockSpec((bb, bl, bn), lambda ib, il, j, k: (ib, il, j)),
        out_shape=jax.ShapeDtypeStruct((b, l, n), out_dtype),
        scratch_shapes=scratch,
        compiler_params=_cparams("arbitrary", "arbitrary", "arbitrary", "arbitrary"),
        name="matmul_" + epilogue,
    )(*args)


def _prep_even_kernel(*refs, t, p, nl, heads, n_carry):
    proj_ref, prev_ref, c_ref, s1_ref, s2_ref, qg_ref, kg_ref, pw_ref, ps_ref = refs[:9]
    pool_ref, npool_ref, q_ref, k32_ref, kb_ref, v32_ref, vb_ref, full_ref = refs[9 + n_carry:]
    il = pl.program_id(1)

    @pl.when(il == 0)
    def _():
        full_ref[0:POOL_MAX, :] = prev_ref[...]

    @pl.when(il > 0)
    def _():
        full_ref[0:POOL_MAX, :] = full_ref[t:t + POOL_MAX, :]

    full_ref[POOL_MAX:POOL_MAX + t, :] = proj_ref[:, 0:POOL_DIM]
    pos1 = p + il * t + 1 + lax.broadcasted_iota(jnp.int32, (t, 1), 0)
    for g, w in enumerate(POOL_WINDOWS):
        sl = slice(g * POOL_GDIM, (g + 1) * POOL_GDIM)
        u = full_ref[POOL_MAX:POOL_MAX + t, sl]
        win = u
        for s in range(1, w):
            win = win + full_ref[POOL_MAX - s:POOL_MAX - s + t, sl]
        cnt = jnp.minimum(pos1, w).astype(F32)
        d = (win / cnt - u).astype(BF16)
        mixed = jnp.dot(d, pw_ref[g], preferred_element_type=F32) * ps_ref[:, sl]
        pool_ref[:, sl] = mixed.astype(pool_ref.dtype)

    @pl.when(il == nl - 1)
    def _():
        npool_ref[...] = full_ref[t + 1:t + POOL_MAX, :]

    seg = (lax.broadcasted_iota(jnp.int32, (HEAD_W, HEAD_W), 0) >> 6) == \
          (lax.broadcasted_iota(jnp.int32, (HEAD_W, HEAD_W), 1) >> 6)
    seg = seg.astype(F32).astype(BF16)
    rc, rs1, rs2 = c_ref[...], s1_ref[...], s2_ref[...]

    def norm_rope(x, g):
        x2 = x * x
        hi = x2.astype(BF16)
        lo = (x2 - hi.astype(F32)).astype(BF16)
        ss = (jnp.dot(lo, seg, preferred_element_type=F32)
              + jnp.dot(hi, seg, preferred_element_type=F32))
        y = x * lax.rsqrt(ss * (1.0 / DIFF_DH) + EPS) * g
        return y * rc + pltpu.roll(y, 8, 1) * rs2 + pltpu.roll(y, HEAD_W - 8, 1) * rs1

    width = heads * HEAD_W
    q_off, k_off, v_off = POOL_DIM, POOL_DIM + width, POOL_DIM + 2 * width
    qg, kg = qg_ref[...], kg_ref[...]
    for h in range(heads):
        sl = slice(h * HEAD_W, (h + 1) * HEAD_W)
        rq = norm_rope(proj_ref[:, q_off + h * HEAD_W:q_off + (h + 1) * HEAD_W], qg)
        q_ref[:, sl] = (rq * (DIFF_DH ** -0.5 * LOG2E)).astype(BF16)
        rk = norm_rope(proj_ref[:, k_off + h * HEAD_W:k_off + (h + 1) * HEAD_W], kg)
        k32_ref[:, sl] = rk
        kb_ref[:, sl] = rk.astype(BF16)
    v = proj_ref[:, v_off:v_off + width]
    v32_ref[...] = v
    vb_ref[...] = v.astype(BF16)


def _diff_rope_tables(p, l):
    inv_freq = ROPE_THETA ** (-jnp.arange(0, ROPE_DIM, 2, dtype=F32) / ROPE_DIM)
    ang = (p + jnp.arange(l)).astype(F32)[:, None] * inv_freq[None, :]
    cos, sin = jnp.cos(ang), jnp.sin(ang)
    half = ROPE_DIM // 2
    rest = DIFF_DH - ROPE_DIM
    c = jnp.concatenate([cos, cos, jnp.ones((l, rest), F32)], axis=-1)
    s1 = jnp.concatenate([-sin, jnp.zeros((l, half + rest), F32)], axis=-1)
    s2 = jnp.concatenate([jnp.zeros((l, half), F32), sin, jnp.zeros((l, rest), F32)], axis=-1)
    return tuple(jnp.concatenate([a, a], axis=-1) for a in (c, s1, s2))


def prep_even(proj, pool_prev, p, qn, kn, pool_w, pool_scale, layer_slot, n_slots, carry):
    b, l, n_in = proj.shape
    width = (n_in - POOL_DIM) // 3
    heads = width // HEAD_W
    t = _pick(l, 256)
    nl = l // t
    prev16 = jnp.concatenate([jnp.zeros((b, 1, POOL_DIM), F32), pool_prev.astype(F32)], axis=1)
    rc, rs1, rs2 = _diff_rope_tables(p, l)
    qg = jnp.concatenate([qn, qn]).reshape(1, HEAD_W).astype(F32)
    kg = jnp.concatenate([kn, kn]).reshape(1, HEAD_W).astype(F32)
    tok = lambda w: pl.BlockSpec((None, t, w), lambda ib, il: (ib, il, 0))
    tab = pl.BlockSpec((t, HEAD_W), lambda ib, il: (il, 0))
    vec = pl.BlockSpec((1, HEAD_W), lambda ib, il: (0, 0))
    out_shapes = (
        jax.ShapeDtypeStruct((b, l, POOL_DIM), BF16),
        jax.ShapeDtypeStruct((b, POOL_MAX - 1, POOL_DIM), F32),
        jax.ShapeDtypeStruct((b, l, width), BF16),
        jax.ShapeDtypeStruct((n_slots, b, l, width), F32),
        jax.ShapeDtypeStruct((b, l, width), BF16),
        jax.ShapeDtypeStruct((n_slots, b, l, width), F32),
        jax.ShapeDtypeStruct((b, l, width), BF16),
    )
    stacked = pl.BlockSpec((None, None, t, width), lambda ib, il: (layer_slot, ib, il, 0))
    return pl.pallas_call(
        functools.partial(_prep_even_kernel, t=t, p=p, nl=nl, heads=heads, n_carry=len(carry)),
        grid=(b, nl),
        in_specs=[tok(n_in),
                  pl.BlockSpec((None, POOL_MAX, POOL_DIM), lambda ib, il: (ib, 0, 0)),
                  tab, tab, tab, vec, vec,
                  pl.BlockSpec((len(POOL_WINDOWS), POOL_GDIM, POOL_GDIM), lambda ib, il: (0, 0, 0)),
                  pl.BlockSpec((1, POOL_DIM), lambda ib, il: (0, 0))]
                 + [pl.BlockSpec(memory_space=pl.ANY)] * len(carry),
        out_specs=(tok(POOL_DIM),
                   pl.BlockSpec((None, POOL_MAX - 1, POOL_DIM), lambda ib, il: (ib, 0, 0)),
                   tok(width), stacked, tok(width), stacked, tok(width)),
        out_shape=out_shapes,
        scratch_shapes=[pltpu.VMEM((POOL_MAX + t, POOL_DIM), F32)],
        input_output_aliases={9 + i: o for i, o in zip(range(len(carry)), (3, 5))},
        compiler_params=_cparams("arbitrary", "arbitrary"),
        name="prep_even",
    )(proj, prev16, rc, rs1, rs2, qg, kg, pool_w.astype(BF16), pool_scale.reshape(1, POOL_DIM), *carry)


def _prep_odd_kernel(*refs, heads, n_carry):
    fq_ref, fk_ref, fv_ref, fl_ref, qg_ref, kg_ref, fb_ref = refs[:7]
    q_ref, k32_ref, kb_ref, v32_ref, vb_ref, lf_ref = refs[7 + n_carry:]
    def rms(x, g):
        return x * lax.rsqrt(jnp.mean(x * x, axis=-1, keepdims=True) + EPS) * g

    qg, kg = qg_ref[...], kg_ref[...]
    for h in range(heads):
        sl = slice(h * HEAD_W, (h + 1) * HEAD_W)
        q_ref[:, sl] = (rms(fq_ref[:, sl], qg) * (FOX_DH ** -0.5 * LOG2E)).astype(BF16)
        rk = rms(fk_ref[:, sl], kg)
        k32_ref[:, sl] = rk
        kb_ref[:, sl] = rk.astype(BF16)
    v = fv_ref[...]
    v32_ref[...] = v
    vb_ref[...] = v.astype(BF16)
    x = fl_ref[...] + fb_ref[...]
    logf = -(jnp.maximum(-x, 0.0) + jnp.log1p(jnp.exp(-jnp.abs(x))))
    lf_ref[...] = logf[:, 0:heads]


def prep_odd(proj, fl, fox_off, heads, qn, kn, f_bias, layer_slot, n_slots, carry):
    b, l, _ = proj.shape
    width = heads * HEAD_W
    t = _pick(l, 512)
    cb = fox_off // width
    tok = lambda w, c: pl.BlockSpec((None, t, w), lambda ib, il: (ib, il, c))
    vec = pl.BlockSpec((1, HEAD_W), lambda ib, il: (0, 0))
    fb = jnp.zeros((1, HEAD_W), F32).at[0, :heads].set(f_bias.astype(F32))
    out_shapes = (
        jax.ShapeDtypeStruct((b, l, width), BF16),
        jax.ShapeDtypeStruct((n_slots, b, l, width), F32),
        jax.ShapeDtypeStruct((b, l, width), BF16),
        jax.ShapeDtypeStruct((n_slots, b, l, width), F32),
        jax.ShapeDtypeStruct((b, l, width), BF16),
        jax.ShapeDtypeStruct((b, l, heads), F32),
    )
    stacked = pl.BlockSpec((None, None, t, width), lambda ib, il: (layer_slot, ib, il, 0))
    return pl.pallas_call(
        functools.partial(_prep_odd_kernel, heads=heads, n_carry=len(carry)),
        grid=(b, l // t),
        in_specs=[tok(width, cb), tok(width, cb + 1), tok(width, cb + 2), tok(HEAD_W, 0),
                  vec, vec, vec] + [pl.BlockSpec(memory_space=pl.ANY)] * len(carry),
        out_specs=(tok(width, 0), stacked, tok(width, 0), stacked, tok(width, 0),
                   tok(heads, 0)),
        out_shape=out_shapes,
        input_output_aliases={7 + i: o for i, o in zip(range(len(carry)), (1, 3))},
        compiler_params=_cparams("arbitrary", "arbitrary"),
        name="prep_odd",
    )(proj, proj, proj, fl, qn.reshape(1, HEAD_W), kn.reshape(1, HEAD_W), fb, *carry)


CUMSUM_BLOCK = 256


def _cumsum_kernel(x_ref, o_ref, carry_ref):
    @pl.when(pl.program_id(0) == 0)
    def _():
        carry_ref[...] = jnp.zeros_like(carry_ref)

    x = x_ref[...]
    x1 = x.astype(BF16)
    r1 = x - x1.astype(F32)
    x2 = r1.astype(BF16)
    x3 = (r1 - x2.astype(F32)).astype(BF16)
    n = x.shape[-1]
    tri = lax.broadcasted_iota(jnp.int32, (n, n), 0) <= lax.broadcasted_iota(jnp.int32, (n, n), 1)
    tri = tri.astype(F32).astype(BF16)
    c = (jnp.dot(x3, tri, preferred_element_type=F32)
         + jnp.dot(x2, tri, preferred_element_type=F32)
         + jnp.dot(x1, tri, preferred_element_type=F32)) + carry_ref[...]
    o_ref[...] = c
    carry_ref[...] = c[:, n - 1:n]


def cumsum_lanes(x):
    r, n = x.shape
    return pl.pallas_call(
        _cumsum_kernel,
        grid=(n // CUMSUM_BLOCK,),
        in_specs=[pl.BlockSpec((r, CUMSUM_BLOCK), lambda i: (0, i))],
        out_specs=pl.BlockSpec((r, CUMSUM_BLOCK), lambda i: (0, i)),
        out_shape=jax.ShapeDtypeStruct((r, n), F32),
        scratch_shapes=[pltpu.VMEM((r, 1), F32)],
        compiler_params=_cparams("arbitrary"),
        name="cumsum",
    )(x)


def _flash_kernel(*refs, kind, tq, tk, p, lam_init, rq, single_tile, has_past):
    if has_past:
        pk_ref, pv_ref, *refs = refs
    if kind == "diff":
        (q_ref, k_ref, v_ref, lv_ref, gn_ref, o_ref,
         qzt_scr, s0_scr, s1_scr, mx0_scr, mx1_scr, p0_scr, p1_scr, acc_scr, l_scr) = refs
    else:
        (q_ref, k_ref, v_ref, cq_ref, ck_ref, o_ref,
         qzt_scr, s0_scr, s1_scr, mx0_scr, mx1_scr, p0_scr, p1_scr, acc_scr, l_scr,
         ckcol_ref) = refs
    s_scr, mx_scr, p_scr = (s0_scr, s1_scr), (mx0_scr, mx1_scr), (p0_scr, p1_scr)
    iq = pl.program_id(2)
    q0 = iq * tq
    qf = q_ref[...].astype(F32)
    if kind == "diff":
        lane = lax.broadcasted_iota(jnp.int32, (tq, HEAD_W), 1)
        qz = jnp.concatenate([jnp.where(lane < DIFF_DH, qf, 0.0),
                              jnp.where(lane >= DIFF_DH, qf, 0.0)], axis=0)
    else:
        qz = qf if rq == tq else jnp.concatenate([qf, jnp.zeros((rq - tq, HEAD_W), F32)], axis=0)
        cq = cq_ref[pl.ds(iq, 1), :] * LOG2E

        @pl.when(iq == 0)
        def _():
            def fill(c, carry):
                row = jnp.broadcast_to(ck_ref[pl.ds(c, 1), :] * LOG2E, (HEAD_W, HEAD_W))
                ckcol_ref[pl.ds(pl.multiple_of(c * HEAD_W, HEAD_W), HEAD_W), :] = row.T
                return carry
            lax.fori_loop(0, ck_ref.shape[0], fill, 0)

    qzt_scr[...] = qz.T.astype(BF16)
    acc_scr[...] = jnp.zeros_like(acc_scr)

    def produce(slot, j, masked):
        start = pl.multiple_of(j * tk, tk)
        if has_past:
            s = jnp.concatenate(
                [jnp.dot(pk_ref[...].astype(BF16), qzt_scr[...], preferred_element_type=F32),
                 jnp.dot(k_ref[...], qzt_scr[...], preferred_element_type=F32)], axis=0)
        else:
            s = jnp.dot(k_ref[pl.ds(start, tk), :], qzt_scr[...], preferred_element_type=F32)
        if kind == "fox":
            ckc = ckcol_ref[pl.ds(start, tk), :]
            s = jnp.concatenate([s[:, c * HEAD_W:(c + 1) * HEAD_W] - ckc
                                 for c in range(rq // HEAD_W)], axis=1)
        if masked:
            kpos = start + lax.broadcasted_iota(jnp.int32, (tk, rq), 0)
            r = lax.broadcasted_iota(jnp.int32, (tk, rq), 1)
            if kind == "diff":
                qpos = p + q0 + jnp.where(r >= tq, r - tq, r)
                ok = (kpos >> 6) <= (qpos >> 6)
            else:
                ok = kpos <= p + q0 + r
            s = jnp.where(ok, s, -jnp.inf)
        s_scr[slot][...] = s
        mx_scr[slot][...] = jnp.max(s, axis=0, keepdims=True)

    def consume(slot, j, m, l):
        tmax = mx_scr[slot][...]
        if kind == "fox":
            m_new = jnp.maximum(m, tmax + cq)
            shift = cq - m_new
        else:
            m_new = jnp.maximum(m, tmax)
            shift = -m_new
        alpha = jnp.exp2(m - m_new)
        shift_b = jnp.broadcast_to(shift, (PACK_ROWS, rq))
        lsum = jnp.zeros((PACK_ROWS, rq), F32)
        for c in range(tk // PACK_ROWS):
            rows = slice(c * PACK_ROWS, (c + 1) * PACK_ROWS)
            pr = jnp.exp2(s_scr[slot][rows, :] + shift_b)
            lsum = lsum + pr
            p_scr[slot][rows, :] = pr.astype(BF16)
        start = pl.multiple_of(j * tk, tk)
        tn = (((0,), (0,)), ((), ()))
        if has_past:
            pv = (lax.dot_general(pv_ref[...].astype(BF16), p_scr[slot][0:p, :], tn,
                                  preferred_element_type=F32)
                  + lax.dot_general(v_ref[...], p_scr[slot][p:tk, :], tn,
                                    preferred_element_type=F32))
        else:
            pv = lax.dot_general(v_ref[pl.ds(start, tk), :], p_scr[slot][...], tn,
                                 preferred_element_type=F32)
        acc_scr[...] = acc_scr[...] * alpha + pv
        return m_new, alpha * l + jnp.sum(lsum, axis=0, keepdims=True)

    def run(j0, count, m, l, ends_masked):
        for i in range(count):
            slot = i % 2
            if i + 1 < count:
                produce(1 - slot, j0 + i + 1, ends_masked and i + 2 == count)
            elif not ends_masked:
                produce(1 - slot, j0 + count, False)
            m, l = consume(slot, j0 + i, m, l)
        return m, l

    m0 = jnp.full((1, rq), -jnp.inf, F32)
    l0 = jnp.zeros((1, rq), F32)
    if single_tile:
        produce(0, 0, True)
        l_scr[...] = run(0, 1, m0, l0, True)[1]
    else:
        n_full = lax.div(p + q0 + tq + tk - 1, tk) - 1
        n_iter = lax.div(jnp.maximum(n_full - 1, 0), FLASH_UNROLL)

        @pl.when(n_full > 0)
        def _():
            produce(0, 0, False)

        m, l = lax.fori_loop(
            0, n_iter, lambda t, c: run(FLASH_UNROLL * t, FLASH_UNROLL, c[0], c[1], False), (m0, l0))
        first = FLASH_UNROLL * n_iter
        rest = n_full - first

        @pl.when(n_full == 0)
        def _():
            produce(0, 0, True)
            l_scr[...] = run(0, 1, m, l, True)[1]

        for r in range(1, FLASH_UNROLL + 1):
            @pl.when(jnp.logical_and(n_full > 0, rest == r))
            def _(r=r):
                l_scr[...] = run(first, r + 1, m, l, True)[1]

    o = (acc_scr[...] / l_scr[...]).T
    if kind == "diff":
        lv = lv_ref[...]
        lam = (jnp.exp(jnp.sum(lv[0:1] * lv[1:2], axis=-1, keepdims=True))
               - jnp.exp(jnp.sum(lv[2:3] * lv[3:4], axis=-1, keepdims=True)) + lam_init)
        o = o[:tq] - lam * o[tq:]
        o = o * lax.rsqrt(jnp.mean(o * o, axis=-1, keepdims=True) + EPS) * gn_ref[...]
        o = o * (1.0 - lam_init)
    else:
        o = o[:tq]
    o_ref[...] = o.astype(o_ref.dtype)


def flash(kind, q, k_new, v_new, past_k, past_v, tq_pref, tk_pref, extra, lam_init=0.0):
    b, l, width = q.shape
    heads = width // HEAD_W
    p = past_k.shape[1]
    has_past = p > 0
    tq = _pick(l, tq_pref)
    tk = p + l if has_past else _pick(l, tk_pref)
    assert tk % PACK_ROWS == 0 and (tk == p + l or tk % tq == 0)
    qspec = pl.BlockSpec((None, tq, HEAD_W), lambda ib, ih, iq: (ib, iq, ih))
    kvspec = pl.BlockSpec((None, l, HEAD_W), lambda ib, ih, iq: (ib, 0, ih))
    pastspec = pl.BlockSpec((None, p, HEAD_W), lambda ib, ih, iq: (ib, 0, ih))
    past_specs, past_args = [], []
    if has_past:
        past_specs = [pastspec, pastspec]
        past_args = [past_k.reshape(b, p, width).astype(BF16), past_v.reshape(b, p, width).astype(BF16)]
    rq = 2 * tq if kind == "diff" else max(tq, HEAD_W)
    scratch = ([pltpu.VMEM((HEAD_W, rq), BF16)]
               + [pltpu.VMEM((tk, rq), F32)] * 2
               + [pltpu.VMEM((1, rq), F32)] * 2
               + [pltpu.VMEM((tk, rq), BF16)] * 2
               + [pltpu.VMEM((HEAD_W, rq), F32),
                  pltpu.VMEM((1, rq), F32)])
    if kind == "diff":
        lam_vec, gn = extra
        especs = [pl.BlockSpec(lam_vec.shape, lambda ib, ih, iq: (0, 0)),
                  pl.BlockSpec((1, HEAD_W), lambda ib, ih, iq: (0, 0))]
        eargs = [lam_vec, gn.reshape(1, HEAD_W)]
    else:
        cum = extra
        lc = cum.shape[-1]
        assert lc % HEAD_W == 0 and lc >= p + l
        cq = cum[:, :, p:p + l].reshape(b, heads, l // tq, tq)
        if rq > tq:
            cq = jnp.pad(cq, ((0, 0), (0, 0), (0, 0), (0, rq - tq)))
        ck = cum.reshape(b, heads, lc // HEAD_W, HEAD_W)
        especs = [pl.BlockSpec((None, None, l // tq, rq), lambda ib, ih, iq: (ib, ih, 0, 0)),
                  pl.BlockSpec((None, None, lc // HEAD_W, HEAD_W), lambda ib, ih, iq: (ib, ih, 0, 0))]
        eargs = [cq, ck]
        scratch.append(pltpu.VMEM((lc, HEAD_W), F32))
    return pl.pallas_call(
        functools.partial(_flash_kernel, kind=kind, tq=tq, tk=tk, p=p, lam_init=lam_init, rq=rq,
                          single_tile=(tk == p + l), has_past=has_past),
        grid=(b, heads, l // tq),
        in_specs=past_specs + [qspec, kvspec, kvspec] + especs,
        out_specs=qspec,
        out_shape=jax.ShapeDtypeStruct((b, l, width), BF16),
        scratch_shapes=scratch,
        compiler_params=_cparams("arbitrary", "arbitrary", "arbitrary"),
        name="flash_" + kind,
    )(*past_args, q, k_new, v_new, *eargs)


def _ret_kernel(lg_ref, q_ref, k_ref, v_ref, g_ref, cos_ref, sin_ref, s0_ref, gn_ref,
                o_ref, sn_ref, s_scr, *, c, nc):
    t = pl.program_id(2)

    @pl.when(t == 0)
    def _():
        s_scr[...] = s0_ref[...]

    cos, sin = cos_ref[...], sin_ref[...]
    half = RET_DK // 2

    def rot(x):
        x1, x2 = x[:, :half], x[:, half:]
        return jnp.concatenate([x1 * cos - x2 * sin, x2 * cos + x1 * sin], axis=-1)

    q = rot(q_ref[...])
    k = rot(k_ref[...]) * RET_DK ** -0.5
    vb = v_ref[...].astype(BF16)
    lg = lg_ref[pl.program_id(1)]
    dist = (lax.broadcasted_iota(jnp.int32, (c, c), 0)
            - lax.broadcasted_iota(jnp.int32, (c, c), 1)).astype(F32)
    intra = jnp.where(dist >= 0, jnp.exp(lg * jnp.maximum(dist, 0.0)), 0.0)
    ic = lax.broadcasted_iota(jnp.int32, (c, 1), 0).astype(F32)
    q_dec = jnp.exp(lg * (ic + 1.0))
    k_dec = jnp.exp(lg * (c - 1.0 - ic))
    blk_dec = jnp.exp(lg * jnp.full((1, RET_DV), float(c), F32))
    att = lax.dot_general(q.astype(BF16), k.astype(BF16), (((1,), (1,)), ((), ())),
                          preferred_element_type=F32) * intra
    s = s_scr[...]
    o = (jnp.dot(att.astype(BF16), vb, preferred_element_type=F32)
         + jnp.dot((q * q_dec).astype(BF16), s.astype(BF16), preferred_element_type=F32))
    s_new = s * blk_dec + lax.dot_general((k * k_dec).astype(BF16), vb, (((0,), (0,)), ((), ())),
                                          preferred_element_type=F32)
    s_scr[...] = s_new
    mu = jnp.mean(o, axis=-1, keepdims=True)
    oc = o - mu
    var = jnp.mean(oc * oc, axis=-1, keepdims=True)
    y = oc * lax.rsqrt(var + EPS) * gn_ref[...]
    o_ref[...] = (y * _silu(g_ref[...])).astype(o_ref.dtype)

    @pl.when(t == nc - 1)
    def _():
        sn_ref[...] = s_new


def retention(proj, p, s0, ret_gn):
    b, l, _ = proj.shape
    heads = s0.shape[1]
    c = _pick(l, 256)
    nc = l // c
    inv_freq = RET_THETA ** (-jnp.arange(0, RET_DK, 2, dtype=F32) / RET_DK)
    ang = (p + jnp.arange(l)).astype(F32)[:, None] * inv_freq[None, :]
    cos, sin = jnp.cos(ang), jnp.sin(ang)
    lg = jnp.asarray([math.log(1.0 - 2.0 ** (-5 - h)) for h in range(heads)], F32)
    col = lambda off: pl.BlockSpec((None, c, RET_DK), lambda ib, ih, it: (ib, it, off + ih))
    tab = pl.BlockSpec((c, RET_DK // 2), lambda ib, ih, it: (it, 0))
    st = pl.BlockSpec((None, None, RET_DK, RET_DV), lambda ib, ih, it: (ib, ih, 0, 0))
    return pl.pallas_call(
        functools.partial(_ret_kernel, c=c, nc=nc),
        grid=(b, heads, nc),
        in_specs=[pl.BlockSpec(memory_space=pltpu.SMEM),
                  col(0), col(heads), col(2 * heads), col(3 * heads), tab, tab, st,
                  pl.BlockSpec((None, 1, RET_DV), lambda ib, ih, it: (ih, 0, 0))],
        out_specs=(pl.BlockSpec((None, c, RET_DV), lambda ib, ih, it: (ib, it, ih)), st),
        out_shape=(jax.ShapeDtypeStruct((b, l, heads * RET_DV), BF16),
                   jax.ShapeDtypeStruct(s0.shape, F32)),
        scratch_shapes=[pltpu.VMEM((RET_DK, RET_DV), F32)],
        compiler_params=_cparams("arbitrary", "arbitrary", "arbitrary"),
        name="retention",
    )(lg, proj, proj, proj, proj, cos, sin, s0.astype(F32), ret_gn.reshape(heads, 1, RET_DV))


def even_mixer(h, x, mod, pool_prev, past_k, past_v, w_in, w_out, pool_w, pool_scale,
               qn, kn, lam_vec, out_gn, lam_init, layer_slot, n_slots, carry):
    p = past_k.shape[1]
    proj = matmul([h], [w_in], F32)
    pool_out, new_pool, q, k32, kb, v32, vb = prep_even(
        proj, pool_prev, p, qn, kn, pool_w, pool_scale, layer_slot, n_slots, carry)
    o = flash("diff", q, kb, vb, past_k, past_v, 256, 512,
              (lam_vec.astype(F32), out_gn.astype(F32)), lam_init)
    x = matmul([pool_out, o], [w_out[:POOL_DIM], w_out[POOL_DIM:]], F32, "resid", x, mod, 2)
    return x, new_pool, k32, v32


def odd_mixer(h, x, mod, s0, past_k, past_v, past_logf, w_in, w_fl, w_out, ret_gn, qn, kn, f_bias,
              layer_slot, n_slots, carry):
    p = past_k.shape[1]
    b, l, _ = h.shape
    ret_heads = s0.shape[1]
    fox_heads = f_bias.shape[0]
    proj = matmul([h], [w_in], F32)
    fl = matmul([h], [w_fl], F32)
    ret, s_new = retention(proj, p, s0, ret_gn)
    fox_off = 2 * ret_heads * RET_DK + 2 * ret_heads * RET_DV
    q, k32, kb, v32, vb, logf = prep_odd(proj, fl, fox_off, fox_heads, qn, kn, f_bias,
                                         layer_slot, n_slots, carry)
    lc = -(-(p + l) // CUMSUM_BLOCK) * CUMSUM_BLOCK
    lf_all = jnp.concatenate([past_logf.astype(F32), logf,
                              jnp.zeros((b, lc - p - l, fox_heads), F32)], axis=1)
    cum = cumsum_lanes(jnp.swapaxes(lf_all, 1, 2).reshape(b * fox_heads, lc))
    o = flash("fox", q, kb, vb, past_k, past_v, 512, 512, cum.reshape(b, fox_heads, lc))
    x = matmul([ret, o], [w_out[:ret.shape[-1]], w_out[ret.shape[-1]:]], F32, "resid", x, mod, 2)
    return x, s_new, k32, v32, logf


def trunk(x, mods, pool_prev, diff_k, diff_v, ret_s, fox_k, fox_v, fox_logf, wts):
    (norm1, norm2, w_up, w_down, w_in_even, w_out_even, pool_w, pool_scale, diff_qn, diff_kn,
     diff_lam, diff_gn, w_in_odd, w_fl_odd, w_out_odd, ret_gn, fox_qn, fox_kn, fox_fbias) = wts
    depth = norm1.shape[0]
    n_pair = depth // 2
    b, seq, _ = x.shape
    n_pool, n_rs, n_fl = [], [], []
    diff_kv, fox_kv = (), ()
    for l in range(depth):
        j = l // 2
        mod = mods[l]
        h = norm_mod(x, norm1[l], mod, 0, 1)
        if l % 2 == 0:
            x, p_new, *diff_kv = even_mixer(
                h, x, mod, pool_prev[j], diff_k[j], diff_v[j], w_in_even[j], w_out_even[j], pool_w[j],
                pool_scale[j], diff_qn[j], diff_kn[j], diff_lam[j], diff_gn[j],
                0.8 - 0.6 * math.exp(-0.3 * l), j, n_pair, tuple(diff_kv))
            n_pool.append(p_new)
        else:
            x, s_new, *fox_kv, lf_new = odd_mixer(
                h, x, mod, ret_s[j], fox_k[j], fox_v[j], fox_logf[j], w_in_odd[j], w_fl_odd[j],
                w_out_odd[j], ret_gn[j], fox_qn[j], fox_kn[j], fox_fbias[j], j, n_pair, tuple(fox_kv))
            n_rs.append(s_new)
            n_fl.append(lf_new)
        h = norm_mod(x, norm2[l], mod, 3, 4)
        a = matmul([h], [w_up[l]], BF16, "relu2")
        x = matmul([a], [w_down[l]], F32, "resid", x, mod, 5, bk=_pick(a.shape[-1], 2048))
    st = jnp.stack
    by_head = lambda a: a.reshape(n_pair, b, seq, a.shape[-1] // HEAD_W, HEAD_W)
    return (x, st(n_pool), by_head(diff_kv[0]), by_head(diff_kv[1]), st(n_rs),
            by_head(fox_kv[0]), by_head(fox_kv[1]), st(n_fl))


def kernel(x_prompt, x_sample, c_prompt, c_sample, cache_pool, cache_diff_k, cache_diff_v, state_ret,
           cache_fox_k, cache_fox_v, cache_fox_logf, w_ada, b_ada, norm1, norm2, w_up, w_down,
           w_in_even, w_out_even, pool_w, pool_scale, diff_qn, diff_kn, diff_lam, diff_gn,
           w_in_odd, w_out_odd, ret_gn, fox_qn, fox_kn, fox_fbias):
    bp, _, d = x_prompt.shape
    bs = x_sample.shape[0]
    n_pair = cache_pool.shape[0]
    depth = w_ada.shape[0]
    diff_heads = cache_diff_k.shape[3]
    ret_heads = state_ret.shape[2]
    fox_heads = cache_fox_k.shape[3]

    mc = -(-(bp + bs) // 16) * 16
    c_all = jnp.concatenate([c_prompt, c_sample, jnp.zeros((mc - bp - bs, d), F32)], axis=0)
    mod_all = ada_mod(c_all, w_ada, b_ada)
    mods_p = mod_all[:, :bp].reshape(depth, bp, 1, 6 * d)
    mods_s = mod_all[:, bp:bp + bs].reshape(depth, bs, 1, 6 * d)

    fox_main = w_in_odd.shape[-1] - fox_heads
    w_fl = jnp.pad(w_in_odd[:, :, fox_main:], ((0, 0), (0, 0), (0, HEAD_W - fox_heads))).astype(BF16)
    wts = (norm1, norm2, w_up.astype(BF16), w_down.astype(BF16), w_in_even.astype(BF16),
           w_out_even.astype(BF16), pool_w, pool_scale, diff_qn, diff_kn, diff_lam, diff_gn,
           w_in_odd[:, :, :fox_main].astype(BF16), w_fl, w_out_odd.astype(BF16),
           ret_gn, fox_qn, fox_kn, fox_fbias)

    dt = x_prompt.dtype
    y_p, pool_p, dk_p, dv_p, rs_p, fk_p, fv_p, fl_p = trunk(
        x_prompt, mods_p,
        jnp.zeros((n_pair, bp, POOL_MAX - 1, POOL_DIM), dt),
        jnp.zeros((n_pair, bp, 0, diff_heads, HEAD_W), dt),
        jnp.zeros((n_pair, bp, 0, diff_heads, HEAD_W), dt),
        jnp.zeros((n_pair, bp, ret_heads, RET_DK, RET_DV), F32),
        jnp.zeros((n_pair, bp, 0, fox_heads, HEAD_W), dt),
        jnp.zeros((n_pair, bp, 0, fox_heads, HEAD_W), dt),
        jnp.zeros((n_pair, bp, 0, fox_heads), F32),
        wts)
    y_s, pool_s, dk_s, dv_s, rs_s, fk_s, fv_s, fl_s = trunk(
        x_sample, mods_s, cache_pool, cache_diff_k, cache_diff_v, state_ret,
        cache_fox_k, cache_fox_v, cache_fox_logf, wts)
    return (y_p, y_s, pool_p, pool_s, dk_p, dk_s, dv_p, dv_s, rs_p, rs_s,
            fk_p, fk_s, fv_p, fv_s, fl_p, fl_s)
```

```python
import functools
import math

import jax
import jax.numpy as jnp
from jax import lax
from jax.experimental import pallas as pl
from jax.experimental.pallas import tpu as pltpu

F32 = jnp.float32
BF16 = jnp.bfloat16
EPS = 1e-6
CHUNK = 64

POOL_WINDOWS = (2, 4, 8, 16)
POOL_MAX = max(POOL_WINDOWS)
POOL_GDIM = 128
POOL_DIM = POOL_GDIM * len(POOL_WINDOWS)

HEAD_W = 128
DIFF_DH = 64
ROPE_DIM = DIFF_DH // 4
ROPE_THETA = 500000.0

RET_DK = 256
RET_DV = 256
RET_THETA = 10000.0

FOX_DH = 128

LOG2E = math.log2(math.e)
PACK_ROWS = 16
FLASH_UNROLL = 2

VMEM_LIMIT_BYTES = 56 * 1024 * 1024


def _cparams(*sem):
    return pltpu.CompilerParams(dimension_semantics=sem, vmem_limit_bytes=VMEM_LIMIT_BYTES)


def _pick(n, pref):
    t = min(pref, n)
    while n % t:
        t //= 2
    return t


def _token_tile(b, l, rows):
    if l >= rows:
        return 1, _pick(l, rows)
    return _pick(b, max(rows // l, 1)), l


def _silu(x):
    return x / (1.0 + jnp.exp(-x))


def _ada_kernel(c_ref, w_ref, b_ref, o_ref):
    c = c_ref[...]
    ca = _silu(c).astype(BF16)
    o_ref[...] = jnp.dot(ca, w_ref[...].astype(BF16), preferred_element_type=F32) + b_ref[...]


def ada_mod(c_all, w_ada, b_ada):
    depth, d, n = w_ada.shape
    mc = c_all.shape[0]
    bn = _pick(n, 1024)
    return pl.pallas_call(
        _ada_kernel,
        grid=(depth, n // bn),
        in_specs=[pl.BlockSpec((mc, d), lambda l, j: (0, 0)),
                  pl.BlockSpec((None, d, bn), lambda l, j: (l, 0, j)),
                  pl.BlockSpec((None, 1, bn), lambda l, j: (l, 0, j))],
        out_specs=pl.BlockSpec((None, mc, bn), lambda l, j: (l, 0, j)),
        out_shape=jax.ShapeDtypeStruct((depth, mc, n), F32),
        compiler_params=_cparams("arbitrary", "arbitrary"),
        name="ada_mod",
    )(c_all, w_ada, b_ada.reshape(depth, 1, n))


def _norm_mod_kernel(x_ref, g_ref, sh_ref, sc_ref, o_ref):
    x = x_ref[...]
    ms = jnp.mean(x * x, axis=-1, keepdims=True)
    y = x * lax.rsqrt(ms + EPS) * g_ref[...]
    o_ref[...] = (y * (1.0 + sc_ref[...]) + sh_ref[...]).astype(o_ref.dtype)


def norm_mod(x, g, mod, shift_idx, scale_idx):
    b, l, d = x.shape
    bb, bl = _token_tile(b, l, 512)
    return pl.pallas_call(
        _norm_mod_kernel,
        grid=(b // bb, l // bl),
        in_specs=[pl.BlockSpec((bb, bl, d), lambda ib, il: (ib, il, 0)),
                  pl.BlockSpec((1, d), lambda ib, il: (0, 0)),
                  pl.BlockSpec((bb, 1, d), lambda ib, il: (ib, 0, shift_idx)),
                  pl.BlockSpec((bb, 1, d), lambda ib, il: (ib, 0, scale_idx))],
        out_specs=pl.BlockSpec((bb, bl, d), lambda ib, il: (ib, il, 0)),
        out_shape=jax.ShapeDtypeStruct((b, l, d), BF16),
        compiler_params=_cparams("arbitrary", "arbitrary"),
        name="norm_mod",
    )(x, g.reshape(1, d), mod, mod)


def _mm_kernel(*refs, n_lhs, nk, epilogue):
    lhs = refs[:n_lhs]
    ws = refs[n_lhs:2 * n_lhs]
    pos = 2 * n_lhs
    if epilogue == "resid":
        xres_ref, gate_ref = refs[pos], refs[pos + 1]
        pos += 2
    o_ref = refs[pos]

    def compute():
        acc = None
        for a_ref, w_ref in zip(lhs, ws):
            a = a_ref[...]
            a = a.reshape(-1, a.shape[-1])
            part = jnp.dot(a, w_ref[...], preferred_element_type=F32)
            acc = part if acc is None else acc + part
        return acc

    def finish(acc):
        if epilogue == "relu2":
            r = jnp.maximum(acc, 0.0)
            y = r * r
        elif epilogue == "resid":
            y = xres_ref[...] + gate_ref[...] * acc.reshape(o_ref.shape)
        else:
            y = acc
        o_ref[...] = y.reshape(o_ref.shape).astype(o_ref.dtype)

    if nk == 1:
        finish(compute())
    else:
        acc_ref = refs[pos + 1]
        k = pl.program_id(3)

        @pl.when(k == 0)
        def _():
            acc_ref[...] = compute()

        @pl.when(k > 0)
        def _():
            acc_ref[...] += compute()

        @pl.when(k == nk - 1)
        def _():
            finish(acc_ref[...])


def matmul(lhs_list, w_list, out_dtype, epilogue="none", xres=None, mod=None, gate_idx=0,
           rows=1024, cols=1024, bk=None):
    b, l, _ = lhs_list[0].shape
    n = w_list[0].shape[1]
    bb, bl = _token_tile(b, l, rows)
    bn = _pick(n, cols)
    ks = [a.shape[-1] for a in lhs_list]
    nk = 1 if bk is None else ks[0] // bk
    assert nk == 1 or len(lhs_list) == 1
    bks = ks if nk == 1 else [bk]
    in_specs = [pl.BlockSpec((bb, bl, kk), lambda ib, il, j, k: (ib, il, k)) for kk in bks]
    in_specs += [pl.BlockSpec((kk, bn), lambda ib, il, j, k: (k, j)) for kk in bks]
    args = list(lhs_list) + list(w_list)
    if epilogue == "resid":
        gate_off = gate_idx * (n // bn)
        in_specs += [pl.BlockSpec((bb, bl, bn), lambda ib, il, j, k: (ib, il, j)),
                     pl.BlockSpec((bb, 1, bn), lambda ib, il, j, k: (ib, 0, gate_off + j))]
        args += [xres, mod]
    scratch = [pltpu.VMEM((bb * bl, bn), F32)] if nk > 1 else []
    return pl.pallas_call(
        functools.partial(_mm_kernel, n_lhs=len(lhs_list), nk=nk, epilogue=epilogue),
        grid=(b // bb, l // bl, n // bn, nk),
        in_specs=in_specs,
        out_specs=pl.BlockSpec((bb, bl, bn), lambda ib, il, j, k: (ib, il, j)),
        out_shape=jax.ShapeDtypeStruct((b, l, n), out_dtype),
        scratch_shapes=scratch,
        compiler_params=_cparams("arbitrary", "arbitrary", "arbitrary", "arbitrary"),
        name="matmul_" + epilogue,
    )(*args)


def _prep_even_kernel(*refs, t, p, nl, heads, n_carry):
    proj_ref, prev_ref, c_ref, s1_ref, s2_ref, qg_ref, kg_ref, pw_ref, ps_ref = refs[:9]
    pool_ref, npool_ref, q_ref, k32_ref, kb_ref, v32_ref, vb_ref, full_ref = refs[9 + n_carry:]
    il = pl.program_id(1)

    @pl.when(il == 0)
    def _():
        full_ref[0:POOL_MAX, :] = prev_ref[...]

    @pl.when(il > 0)
    def _():
        full_ref[0:POOL_MAX, :] = full_ref[t:t + POOL_MAX, :]

    full_ref[POOL_MAX:POOL_MAX + t, :] = proj_ref[:, 0:POOL_DIM]
    pos1 = p + il * t + 1 + lax.broadcasted_iota(jnp.int32, (t, 1), 0)
    for g, w in enumerate(POOL_WINDOWS):
        sl = slice(g * POOL_GDIM, (g + 1) * POOL_GDIM)
        u = full_ref[POOL_MAX:POOL_MAX + t, sl]
        win = u
        for s in range(1, w):
            win = win + full_ref[POOL_MAX - s:POOL_MAX - s + t, sl]
        cnt = jnp.minimum(pos1, w).astype(F32)
        d = (win / cnt - u).astype(BF16)
        mixed = jnp.dot(d, pw_ref[g], preferred_element_type=F32) * ps_ref[:, sl]
        pool_ref[:, sl] = mixed.astype(pool_ref.dtype)

    @pl.when(il == nl - 1)
    def _():
        npool_ref[...] = full_ref[t + 1:t + POOL_MAX, :]

    seg = (lax.broadcasted_iota(jnp.int32, (HEAD_W, HEAD_W), 0) >> 6) == \
          (lax.broadcasted_iota(jnp.int32, (HEAD_W, HEAD_W), 1) >> 6)
    seg = seg.astype(F32).astype(BF16)
    rc, rs1, rs2 = c_ref[...], s1_ref[...], s2_ref[...]

    def norm_rope(x, g):
        x2 = x * x
        hi = x2.astype(BF16)
        lo = (x2 - hi.astype(F32)).astype(BF16)
        ss = (jnp.dot(lo, seg, preferred_element_type=F32)
              + jnp.dot(hi, seg, preferred_element_type=F32))
        y = x * lax.rsqrt(ss * (1.0 / DIFF_DH) + EPS) * g
        return y * rc + pltpu.roll(y, 8, 1) * rs2 + pltpu.roll(y, HEAD_W - 8, 1) * rs1

    width = heads * HEAD_W
    q_off, k_off, v_off = POOL_DIM, POOL_DIM + width, POOL_DIM + 2 * width
    qg, kg = qg_ref[...], kg_ref[...]
    ks, vs = [], []
    for h in range(heads):
        sl = slice(h * HEAD_W, (h + 1) * HEAD_W)
        rq = norm_rope(proj_ref[:, q_off + h * HEAD_W:q_off + (h + 1) * HEAD_W], qg)
        q_ref[:, sl] = (rq * (DIFF_DH ** -0.5 * LOG2E)).astype(BF16)
        rk = norm_rope(proj_ref[:, k_off + h * HEAD_W:k_off + (h + 1) * HEAD_W], kg)
        kb_ref[:, sl] = rk.astype(BF16)
        ks.append(rk)
        vs.append(proj_ref[:, v_off + h * HEAD_W:v_off + (h + 1) * HEAD_W])
    k32_ref[...] = pltpu.einshape("htd->thd", jnp.stack(ks, axis=0))
    v32_ref[...] = pltpu.einshape("htd->thd", jnp.stack(vs, axis=0))
    vb_ref[...] = proj_ref[:, v_off:v_off + width].astype(BF16)


def _diff_rope_tables(p, l):
    inv_freq = ROPE_THETA ** (-jnp.arange(0, ROPE_DIM, 2, dtype=F32) / ROPE_DIM)
    ang = (p + jnp.arange(l)).astype(F32)[:, None] * inv_freq[None, :]
    cos, sin = jnp.cos(ang), jnp.sin(ang)
    half = ROPE_DIM // 2
    rest = DIFF_DH - ROPE_DIM
    c = jnp.concatenate([cos, cos, jnp.ones((l, rest), F32)], axis=-1)
    s1 = jnp.concatenate([-sin, jnp.zeros((l, half + rest), F32)], axis=-1)
    s2 = jnp.concatenate([jnp.zeros((l, half), F32), sin, jnp.zeros((l, rest), F32)], axis=-1)
    return tuple(jnp.concatenate([a, a], axis=-1) for a in (c, s1, s2))


def prep_even(proj, pool_prev, p, qn, kn, pool_w, pool_scale, layer_slot, n_slots, carry):
    b, l, n_in = proj.shape
    width = (n_in - POOL_DIM) // 3
    heads = width // HEAD_W
    t = _pick(l, 256)
    nl = l // t
    prev16 = jnp.concatenate([jnp.zeros((b, 1, POOL_DIM), F32), pool_prev.astype(F32)], axis=1)
    rc, rs1, rs2 = _diff_rope_tables(p, l)
    qg = jnp.concatenate([qn, qn]).reshape(1, HEAD_W).astype(F32)
    kg = jnp.concatenate([kn, kn]).reshape(1, HEAD_W).astype(F32)
    tok = lambda w: pl.BlockSpec((None, t, w), lambda ib, il: (ib, il, 0))
    tab = pl.BlockSpec((t, HEAD_W), lambda ib, il: (il, 0))
    vec = pl.BlockSpec((1, HEAD_W), lambda ib, il: (0, 0))
    out_shapes = (
        jax.ShapeDtypeStruct((b, l, POOL_DIM), BF16),
        jax.ShapeDtypeStruct((b, POOL_MAX - 1, POOL_DIM), F32),
        jax.ShapeDtypeStruct((b, l, width), BF16),
        jax.ShapeDtypeStruct((n_slots, b, l, heads, HEAD_W), F32),
        jax.ShapeDtypeStruct((b, l, width), BF16),
        jax.ShapeDtypeStruct((n_slots, b, l, heads, HEAD_W), F32),
        jax.ShapeDtypeStruct((b, l, width), BF16),
    )
    stacked = pl.BlockSpec((None, None, t, heads, HEAD_W), lambda ib, il: (layer_slot, ib, il, 0, 0))
    return pl.pallas_call(
        functools.partial(_prep_even_kernel, t=t, p=p, nl=nl, heads=heads, n_carry=len(carry)),
        grid=(b, nl),
        in_specs=[tok(n_in),
                  pl.BlockSpec((None, POOL_MAX, POOL_DIM), lambda ib, il: (ib, 0, 0)),
                  tab, tab, tab, vec, vec,
                  pl.BlockSpec((len(POOL_WINDOWS), POOL_GDIM, POOL_GDIM), lambda ib, il: (0, 0, 0)),
                  pl.BlockSpec((1, POOL_DIM), lambda ib, il: (0, 0))]
                 + [pl.BlockSpec(memory_space=pl.ANY)] * len(carry),
        out_specs=(tok(POOL_DIM),
                   pl.BlockSpec((None, POOL_MAX - 1, POOL_DIM), lambda ib, il: (ib, 0, 0)),
                   tok(width), stacked, tok(width), stacked, tok(width)),
        out_shape=out_shapes,
        scratch_shapes=[pltpu.VMEM((POOL_MAX + t, POOL_DIM), F32)],
        input_output_aliases={9 + i: o for i, o in zip(range(len(carry)), (3, 5))},
        compiler_params=_cparams("arbitrary", "arbitrary"),
        name="prep_even",
    )(proj, prev16, rc, rs1, rs2, qg, kg, pool_w.astype(BF16), pool_scale.reshape(1, POOL_DIM), *carry)


def _prep_odd_kernel(*refs, heads, n_carry):
    fq_ref, fk_ref, fv_ref, fl_ref, qg_ref, kg_ref, fb_ref = refs[:7]
    q_ref, k32_ref, kb_ref, v32_ref, vb_ref, lf_ref = refs[7 + n_carry:]
    def rms(x, g):
        return x * lax.rsqrt(jnp.mean(x * x, axis=-1, keepdims=True) + EPS) * g

    qg, kg = qg_ref[...], kg_ref[...]
    ks, vs = [], []
    for h in range(heads):
        sl = slice(h * HEAD_W, (h + 1) * HEAD_W)
        q_ref[:, sl] = (rms(fq_ref[:, sl], qg) * (FOX_DH ** -0.5 * LOG2E)).astype(BF16)
        rk = rms(fk_ref[:, sl], kg)
        kb_ref[:, sl] = rk.astype(BF16)
        ks.append(rk)
        vs.append(fv_ref[:, sl])
    k32_ref[...] = pltpu.einshape("htd->thd", jnp.stack(ks, axis=0))
    v32_ref[...] = pltpu.einshape("htd->thd", jnp.stack(vs, axis=0))
    vb_ref[...] = fv_ref[...].astype(BF16)
    x = fl_ref[...] + fb_ref[...]
    logf = -(jnp.maximum(-x, 0.0) + jnp.log1p(jnp.exp(-jnp.abs(x))))
    lf_ref[...] = logf[:, 0:heads]


def prep_odd(proj, fl, fox_off, heads, qn, kn, f_bias, layer_slot, n_slots, carry):
    b, l, _ = proj.shape
    width = heads * HEAD_W
    t = _pick(l, 512)
    cb = fox_off // width
    tok = lambda w, c: pl.BlockSpec((None, t, w), lambda ib, il: (ib, il, c))
    vec = pl.BlockSpec((1, HEAD_W), lambda ib, il: (0, 0))
    fb = jnp.zeros((1, HEAD_W), F32).at[0, :heads].set(f_bias.astype(F32))
    out_shapes = (
        jax.ShapeDtypeStruct((b, l, width), BF16),
        jax.ShapeDtypeStruct((n_slots, b, l, heads, HEAD_W), F32),
        jax.ShapeDtypeStruct((b, l, width), BF16),
        jax.ShapeDtypeStruct((n_slots, b, l, heads, HEAD_W), F32),
        jax.ShapeDtypeStruct((b, l, width), BF16),
        jax.ShapeDtypeStruct((b, l, heads), F32),
    )
    stacked = pl.BlockSpec((None, None, t, heads, HEAD_W), lambda ib, il: (layer_slot, ib, il, 0, 0))
    return pl.pallas_call(
        functools.partial(_prep_odd_kernel, heads=heads, n_carry=len(carry)),
        grid=(b, l // t),
        in_specs=[tok(width, cb), tok(width, cb + 1), tok(width, cb + 2), tok(HEAD_W, 0),
                  vec, vec, vec] + [pl.BlockSpec(memory_space=pl.ANY)] * len(carry),
        out_specs=(tok(width, 0), stacked, tok(width, 0), stacked, tok(width, 0),
                   tok(heads, 0)),
        out_shape=out_shapes,
        input_output_aliases={7 + i: o for i, o in zip(range(len(carry)), (1, 3))},
        compiler_params=_cparams("arbitrary", "arbitrary"),
        name="prep_odd",
    )(proj, proj, proj, fl, qn.reshape(1, HEAD_W), kn.reshape(1, HEAD_W), fb, *carry)


CUMSUM_BLOCK = 256


def _cumsum_kernel(x_ref, o_ref, carry_ref):
    @pl.when(pl.program_id(0) == 0)
    def _():
        carry_ref[...] = jnp.zeros_like(carry_ref)

    x = x_ref[...]
    x1 = x.astype(BF16)
    r1 = x - x1.astype(F32)
    x2 = r1.astype(BF16)
    x3 = (r1 - x2.astype(F32)).astype(BF16)
    n = x.shape[-1]
    tri = lax.broadcasted_iota(jnp.int32, (n, n), 0) <= lax.broadcasted_iota(jnp.int32, (n, n), 1)
    tri = tri.astype(F32).astype(BF16)
    c = (jnp.dot(x3, tri, preferred_element_type=F32)
         + jnp.dot(x2, tri, preferred_element_type=F32)
         + jnp.dot(x1, tri, preferred_element_type=F32)) + carry_ref[...]
    o_ref[...] = c
    carry_ref[...] = c[:, n - 1:n]


def cumsum_lanes(x):
    r, n = x.shape
    return pl.pallas_call(
        _cumsum_kernel,
        grid=(n // CUMSUM_BLOCK,),
        in_specs=[pl.BlockSpec((r, CUMSUM_BLOCK), lambda i: (0, i))],
        out_specs=pl.BlockSpec((r, CUMSUM_BLOCK), lambda i: (0, i)),
        out_shape=jax.ShapeDtypeStruct((r, n), F32),
        scratch_shapes=[pltpu.VMEM((r, 1), F32)],
        compiler_params=_cparams("arbitrary"),
        name="cumsum",
    )(x)


def _flash_kernel(*refs, kind, tq, tk, p, lam_init, rq, single_tile, has_past):
    if has_past:
        pk_ref, pv_ref, *refs = refs
    if kind == "diff":
        (q_ref, k_ref, v_ref, lv_ref, gn_ref, o_ref,
         qzt_scr, s0_scr, s1_scr, mx0_scr, mx1_scr, p0_scr, p1_scr, acc_scr, l_scr) = refs
    else:
        (q_ref, k_ref, v_ref, cq_ref, ck_ref, o_ref,
         qzt_scr, s0_scr, s1_scr, mx0_scr, mx1_scr, p0_scr, p1_scr, acc_scr, l_scr,
         ckcol_ref) = refs
    s_scr, mx_scr, p_scr = (s0_scr, s1_scr), (mx0_scr, mx1_scr), (p0_scr, p1_scr)
    iq = pl.program_id(2)
    q0 = iq * tq
    qf = q_ref[...].astype(F32)
    if kind == "diff":
        lane = lax.broadcasted_iota(jnp.int32, (tq, HEAD_W), 1)
        qz = jnp.concatenate([jnp.where(lane < DIFF_DH, qf, 0.0),
                              jnp.where(lane >= DIFF_DH, qf, 0.0)], axis=0)
    else:
        qz = qf if rq == tq else jnp.concatenate([qf, jnp.zeros((rq - tq, HEAD_W), F32)], axis=0)
        cq = cq_ref[pl.ds(iq, 1), :] * LOG2E

        @pl.when(iq == 0)
        def _():
            def fill(c, carry):
                row = jnp.broadcast_to(ck_ref[pl.ds(c, 1), :] * LOG2E, (HEAD_W, HEAD_W))
                ckcol_ref[pl.ds(pl.multiple_of(c * HEAD_W, HEAD_W), HEAD_W), :] = row.T
                return carry
            lax.fori_loop(0, ck_ref.shape[0], fill, 0)

    qzt_scr[...] = qz.T.astype(BF16)
    acc_scr[...] = jnp.zeros_like(acc_scr)

    def produce(slot, j, masked):
        start = pl.multiple_of(j * tk, tk)
        if has_past:
            s = jnp.concatenate(
                [jnp.dot(pk_ref[...].astype(BF16), qzt_scr[...], preferred_element_type=F32),
                 jnp.dot(k_ref[...], qzt_scr[...], preferred_element_type=F32)], axis=0)
        else:
            s = jnp.dot(k_ref[pl.ds(start, tk), :], qzt_scr[...], preferred_element_type=F32)
        if kind == "fox":
            ckc = ckcol_ref[pl.ds(start, tk), :]
            s = jnp.concatenate([s[:, c * HEAD_W:(c + 1) * HEAD_W] - ckc
                                 for c in range(rq // HEAD_W)], axis=1)
        if masked:
            kpos = start + lax.broadcasted_iota(jnp.int32, (tk, rq), 0)
            r = lax.broadcasted_iota(jnp.int32, (tk, rq), 1)
            if kind == "diff":
                qpos = p + q0 + jnp.where(r >= tq, r - tq, r)
                ok = (kpos >> 6) <= (qpos >> 6)
            else:
                ok = kpos <= p + q0 + r
            s = jnp.where(ok, s, -jnp.inf)
        s_scr[slot][...] = s
        mx_scr[slot][...] = jnp.max(s, axis=0, keepdims=True)

    def consume(slot, j, m, l):
        tmax = mx_scr[slot][...]
        if kind == "fox":
            m_new = jnp.maximum(m, tmax + cq)
            shift = cq - m_new
        else:
            m_new = jnp.maximum(m, tmax)
            shift = -m_new
        alpha = jnp.exp2(m - m_new)
        shift_b = jnp.broadcast_to(shift, (PACK_ROWS, rq))
        lsum = jnp.zeros((PACK_ROWS, rq), F32)
        for c in range(tk // PACK_ROWS):
            rows = slice(c * PACK_ROWS, (c + 1) * PACK_ROWS)
            pr = jnp.exp2(s_scr[slot][rows, :] + shift_b)
            lsum = lsum + pr
            p_scr[slot][rows, :] = pr.astype(BF16)
        start = pl.multiple_of(j * tk, tk)
        tn = (((0,), (0,)), ((), ()))
        if has_past:
            pv = (lax.dot_general(pv_ref[...].astype(BF16), p_scr[slot][0:p, :], tn,
                                  preferred_element_type=F32)
                  + lax.dot_general(v_ref[...], p_scr[slot][p:tk, :], tn,
                                    preferred_element_type=F32))
        else:
            pv = lax.dot_general(v_ref[pl.ds(start, tk), :], p_scr[slot][...], tn,
                                 preferred_element_type=F32)
        acc_scr[...] = acc_scr[...] * alpha + pv
        return m_new, alpha * l + jnp.sum(lsum, axis=0, keepdims=True)

    def run(j0, count, m, l, ends_masked):
        for i in range(count):
            slot = i % 2
            if i + 1 < count:
                produce(1 - slot, j0 + i + 1, ends_masked and i + 2 == count)
            elif not ends_masked:
                produce(1 - slot, j0 + count, False)
            m, l = consume(slot, j0 + i, m, l)
        return m, l

    m0 = jnp.full((1, rq), -jnp.inf, F32)
    l0 = jnp.zeros((1, rq), F32)
    if single_tile:
        produce(0, 0, True)
        l_scr[...] = run(0, 1, m0, l0, True)[1]
    else:
        n_full = lax.div(p + q0 + tq + tk - 1, tk) - 1
        n_iter = lax.div(jnp.maximum(n_full - 1, 0), FLASH_UNROLL)

        @pl.when(n_full > 0)
        def _():
            produce(0, 0, False)

        m, l = lax.fori_loop(
            0, n_iter, lambda t, c: run(FLASH_UNROLL * t, FLASH_UNROLL, c[0], c[1], False), (m0, l0))
        first = FLASH_UNROLL * n_iter
        rest = n_full - first

        @pl.when(n_full == 0)
        def _():
            produce(0, 0, True)
            l_scr[...] = run(0, 1, m, l, True)[1]

        for r in range(1, FLASH_UNROLL + 1):
            @pl.when(jnp.logical_and(n_full > 0, rest == r))
            def _(r=r):
                l_scr[...] = run(first, r + 1, m, l, True)[1]

    o = (acc_scr[...] / l_scr[...]).T
    if kind == "diff":
        lv = lv_ref[...]
        lam = (jnp.exp(jnp.sum(lv[0:1] * lv[1:2], axis=-1, keepdims=True))
               - jnp.exp(jnp.sum(lv[2:3] * lv[3:4], axis=-1, keepdims=True)) + lam_init)
        o = o[:tq] - lam * o[tq:]
        o = o * lax.rsqrt(jnp.mean(o * o, axis=-1, keepdims=True) + EPS) * gn_ref[...]
        o = o * (1.0 - lam_init)
    else:
        o = o[:tq]
    o_ref[...] = o.astype(o_ref.dtype)


def _cache_rows_kernel(x_ref, o_ref, *, heads):
    y = pltpu.einshape("thd->htd", x_ref[...])
    for h in range(heads):
        o_ref[:, h * HEAD_W:(h + 1) * HEAD_W] = y[h].astype(o_ref.dtype)


def cache_rows(cache):
    n, b, p, heads, w = cache.shape
    pc = _pick(p, 512)
    return pl.pallas_call(
        functools.partial(_cache_rows_kernel, heads=heads),
        grid=(n, b, p // pc),
        in_specs=[pl.BlockSpec((None, None, pc, heads, w), lambda j, ib, ip: (j, ib, ip, 0, 0))],
        out_specs=pl.BlockSpec((None, None, pc, heads * w), lambda j, ib, ip: (j, ib, ip, 0)),
        out_shape=jax.ShapeDtypeStruct((n, b, p, heads * w), BF16),
        compiler_params=_cparams("arbitrary", "arbitrary", "arbitrary"),
        name="cache_rows",
    )(cache)


def flash(kind, q, k_new, v_new, past_k, past_v, layer_slot, tq_pref, tk_pref, extra, lam_init=0.0):
    b, l, width = q.shape
    heads = width // HEAD_W
    p = past_k.shape[2]
    has_past = p > 0
    tq = _pick(l, tq_pref)
    tk = p + l if has_past else _pick(l, tk_pref)
    assert tk % PACK_ROWS == 0 and (tk == p + l or tk % tq == 0)
    qspec = pl.BlockSpec((None, tq, HEAD_W), lambda ib, ih, iq: (ib, iq, ih))
    kvspec = pl.BlockSpec((None, l, HEAD_W), lambda ib, ih, iq: (ib, 0, ih))
    pastspec = pl.BlockSpec((None, None, p, HEAD_W), lambda ib, ih, iq: (layer_slot, ib, 0, ih))
    past_specs, past_args = ([pastspec, pastspec], [past_k, past_v]) if has_past else ([], [])
    rq = 2 * tq if kind == "diff" else max(tq, HEAD_W)
    scratch = ([pltpu.VMEM((HEAD_W, rq), BF16)]
               + [pltpu.VMEM((tk, rq), F32)] * 2
               + [pltpu.VMEM((1, rq), F32)] * 2
               + [pltpu.VMEM((tk, rq), BF16)] * 2
               + [pltpu.VMEM((HEAD_W, rq), F32),
                  pltpu.VMEM((1, rq), F32)])
    if kind == "diff":
        lam_vec, gn = extra
        especs = [pl.BlockSpec(lam_vec.shape, lambda ib, ih, iq: (0, 0)),
                  pl.BlockSpec((1, HEAD_W), lambda ib, ih, iq: (0, 0))]
        eargs = [lam_vec, gn.reshape(1, HEAD_W)]
    else:
        cum = extra
        lc = cum.shape[-1]
        assert lc % HEAD_W == 0 and lc >= p + l
        cq = cum[:, :, p:p + l].reshape(b, heads, l // tq, tq)
        if rq > tq:
            cq = jnp.pad(cq, ((0, 0), (0, 0), (0, 0), (0, rq - tq)))
        ck = cum.reshape(b, heads, lc // HEAD_W, HEAD_W)
        especs = [pl.BlockSpec((None, None, l // tq, rq), lambda ib, ih, iq: (ib, ih, 0, 0)),
                  pl.BlockSpec((None, None, lc // HEAD_W, HEAD_W), lambda ib, ih, iq: (ib, ih, 0, 0))]
        eargs = [cq, ck]
        scratch.append(pltpu.VMEM((lc, HEAD_W), F32))
    return pl.pallas_call(
        functools.partial(_flash_kernel, kind=kind, tq=tq, tk=tk, p=p, lam_init=lam_init, rq=rq,
                          single_tile=(tk == p + l), has_past=has_past),
        grid=(b, heads, l // tq),
        in_specs=past_specs + [qspec, kvspec, kvspec] + especs,
        out_specs=qspec,
        out_shape=jax.ShapeDtypeStruct((b, l, width), BF16),
        scratch_shapes=scratch,
        compiler_params=_cparams("arbitrary", "arbitrary", "arbitrary"),
        name="flash_" + kind,
    )(*past_args, q, k_new, v_new, *eargs)


def _ret_kernel(lg_ref, q_ref, k_ref, v_ref, g_ref, cos_ref, sin_ref, s0_ref, gn_ref,
                o_ref, sn_ref, s_scr, *, c, nc):
    t = pl.program_id(2)

    @pl.when(t == 0)
    def _():
        s_scr[...] = s0_ref[...]

    cos, sin = cos_ref[...], sin_ref[...]
    half = RET_DK // 2

    def rot(x):
        x1, x2 = x[:, :half], x[:, half:]
        return jnp.concatenate([x1 * cos - x2 * sin, x2 * cos + x1 * sin], axis=-1)

    q = rot(q_ref[...])
    k = rot(k_ref[...]) * RET_DK ** -0.5
    vb = v_ref[...].astype(BF16)
    lg = lg_ref[pl.program_id(1)]
    dist = (lax.broadcasted_iota(jnp.int32, (c, c), 0)
            - lax.broadcasted_iota(jnp.int32, (c, c), 1)).astype(F32)
    intra = jnp.where(dist >= 0, jnp.exp(lg * jnp.maximum(dist, 0.0)), 0.0)
    ic = lax.broadcasted_iota(jnp.int32, (c, 1), 0).astype(F32)
    q_dec = jnp.exp(lg * (ic + 1.0))
    k_dec = jnp.exp(lg * (c - 1.0 - ic))
    blk_dec = jnp.exp(lg * jnp.full((1, RET_DV), float(c), F32))
    att = lax.dot_general(q.astype(BF16), k.astype(BF16), (((1,), (1,)), ((), ())),
                          preferred_element_type=F32) * intra
    s = s_scr[...]
    o = (jnp.dot(att.astype(BF16), vb, preferred_element_type=F32)
         + jnp.dot((q * q_dec).astype(BF16), s.astype(BF16), preferred_element_type=F32))
    s_new = s * blk_dec + lax.dot_general((k * k_dec).astype(BF16), vb, (((0,), (0,)), ((), ())),
                                          preferred_element_type=F32)
    s_scr[...] = s_new
    mu = jnp.mean(o, axis=-1, keepdims=True)
    oc = o - mu
    var = jnp.mean(oc * oc, axis=-1, keepdims=True)
    y = oc * lax.rsqrt(var + EPS) * gn_ref[...]
    o_ref[...] = (y * _silu(g_ref[...])).astype(o_ref.dtype)

    @pl.when(t == nc - 1)
    def _():
        sn_ref[...] = s_new


def retention(proj, p, s0, ret_gn):
    b, l, _ = proj.shape
    heads = s0.shape[1]
    c = _pick(l, 256)
    nc = l // c
    inv_freq = RET_THETA ** (-jnp.arange(0, RET_DK, 2, dtype=F32) / RET_DK)
    ang = (p + jnp.arange(l)).astype(F32)[:, None] * inv_freq[None, :]
    cos, sin = jnp.cos(ang), jnp.sin(ang)
    lg = jnp.asarray([math.log(1.0 - 2.0 ** (-5 - h)) for h in range(heads)], F32)
    col = lambda off: pl.BlockSpec((None, c, RET_DK), lambda ib, ih, it: (ib, it, off + ih))
    tab = pl.BlockSpec((c, RET_DK // 2), lambda ib, ih, it: (it, 0))
    st = pl.BlockSpec((None, None, RET_DK, RET_DV), lambda ib, ih, it: (ib, ih, 0, 0))
    return pl.pallas_call(
        functools.partial(_ret_kernel, c=c, nc=nc),
        grid=(b, heads, nc),
        in_specs=[pl.BlockSpec(memory_space=pltpu.SMEM),
                  col(0), col(heads), col(2 * heads), col(3 * heads), tab, tab, st,
                  pl.BlockSpec((None, 1, RET_DV), lambda ib, ih, it: (ih, 0, 0))],
        out_specs=(pl.BlockSpec((None, c, RET_DV), lambda ib, ih, it: (ib, it, ih)), st),
        out_shape=(jax.ShapeDtypeStruct((b, l, heads * RET_DV), BF16),
                   jax.ShapeDtypeStruct(s0.shape, F32)),
        scratch_shapes=[pltpu.VMEM((RET_DK, RET_DV), F32)],
        compiler_params=_cparams("arbitrary", "arbitrary", "arbitrary"),
        name="retention",
    )(lg, proj, proj, proj, proj, cos, sin, s0.astype(F32), ret_gn.reshape(heads, 1, RET_DV))


def even_mixer(h, x, mod, pool_prev, past_k, past_v, w_in, w_out, pool_w, pool_scale,
               qn, kn, lam_vec, out_gn, lam_init, layer_slot, n_slots, carry):
    p = past_k.shape[2]
    proj = matmul([h], [w_in], F32)
    pool_out, new_pool, q, k32, kb, v32, vb = prep_even(
        proj, pool_prev, p, qn, kn, pool_w, pool_scale, layer_slot, n_slots, carry)
    o = flash("diff", q, kb, vb, past_k, past_v, layer_slot, 256, 512,
              (lam_vec.astype(F32), out_gn.astype(F32)), lam_init)
    x = matmul([pool_out, o], [w_out[:POOL_DIM], w_out[POOL_DIM:]], F32, "resid", x, mod, 2)
    return x, new_pool, k32, v32


def odd_mixer(h, x, mod, s0, past_k, past_v, past_logf, w_in, w_fl, w_out, ret_gn, qn, kn, f_bias,
              layer_slot, n_slots, carry):
    p = past_k.shape[2]
    b, l, _ = h.shape
    ret_heads = s0.shape[1]
    fox_heads = f_bias.shape[0]
    proj = matmul([h], [w_in], F32)
    fl = matmul([h], [w_fl], F32)
    ret, s_new = retention(proj, p, s0, ret_gn)
    fox_off = 2 * ret_heads * RET_DK + 2 * ret_heads * RET_DV
    q, k32, kb, v32, vb, logf = prep_odd(proj, fl, fox_off, fox_heads, qn, kn, f_bias,
                                         layer_slot, n_slots, carry)
    lc = -(-(p + l) // CUMSUM_BLOCK) * CUMSUM_BLOCK
    lf_all = jnp.concatenate([past_logf.astype(F32), logf,
                              jnp.zeros((b, lc - p - l, fox_heads), F32)], axis=1)
    cum = cumsum_lanes(jnp.swapaxes(lf_all, 1, 2).reshape(b * fox_heads, lc))
    o = flash("fox", q, kb, vb, past_k, past_v, layer_slot, 512, 512, cum.reshape(b, fox_heads, lc))
    x = matmul([ret, o], [w_out[:ret.shape[-1]], w_out[ret.shape[-1]:]], F32, "resid", x, mod, 2)
    return x, s_new, k32, v32, logf


def trunk(x, mods, pool_prev, diff_k, diff_v, ret_s, fox_k, fox_v, fox_logf, wts):
    (norm1, norm2, w_up, w_down, w_in_even, w_out_even, pool_w, pool_scale, diff_qn, diff_kn,
     diff_lam, diff_gn, w_in_odd, w_fl_odd, w_out_odd, ret_gn, fox_qn, fox_kn, fox_fbias) = wts
    depth = norm1.shape[0]
    n_pair = depth // 2
    b, seq, _ = x.shape
    n_pool, n_rs, n_fl = [], [], []
    diff_kv, fox_kv = (), ()
    for l in range(depth):
        j = l // 2
        mod = mods[l]
        h = norm_mod(x, norm1[l], mod, 0, 1)
        if l % 2 == 0:
            x, p_new, *diff_kv = even_mixer(
                h, x, mod, pool_prev[j], diff_k, diff_v, w_in_even[j], w_out_even[j], pool_w[j],
                pool_scale[j], diff_qn[j], diff_kn[j], diff_lam[j], diff_gn[j],
                0.8 - 0.6 * math.exp(-0.3 * l), j, n_pair, tuple(diff_kv))
            n_pool.append(p_new)
        else:
            x, s_new, *fox_kv, lf_new = odd_mixer(
                h, x, mod, ret_s[j], fox_k, fox_v, fox_logf[j], w_in_odd[j], w_fl_odd[j],
                w_out_odd[j], ret_gn[j], fox_qn[j], fox_kn[j], fox_fbias[j], j, n_pair, tuple(fox_kv))
            n_rs.append(s_new)
            n_fl.append(lf_new)
        h = norm_mod(x, norm2[l], mod, 3, 4)
        a = matmul([h], [w_up[l]], BF16, "relu2")
        x = matmul([a], [w_down[l]], F32, "resid", x, mod, 5, bk=_pick(a.shape[-1], 2048))
    st = jnp.stack
    return x, st(n_pool), diff_kv[0], diff_kv[1], st(n_rs), fox_kv[0], fox_kv[1], st(n_fl)


def kernel(x_prompt, x_sample, c_prompt, c_sample, cache_pool, cache_diff_k, cache_diff_v, state_ret,
           cache_fox_k, cache_fox_v, cache_fox_logf, w_ada, b_ada, norm1, norm2, w_up, w_down,
           w_in_even, w_out_even, pool_w, pool_scale, diff_qn, diff_kn, diff_lam, diff_gn,
           w_in_odd, w_out_odd, ret_gn, fox_qn, fox_kn, fox_fbias):
    bp, _, d = x_prompt.shape
    bs = x_sample.shape[0]
    n_pair = cache_pool.shape[0]
    depth = w_ada.shape[0]
    diff_heads = cache_diff_k.shape[3]
    ret_heads = state_ret.shape[2]
    fox_heads = cache_fox_k.shape[3]

    mc = -(-(bp + bs) // 16) * 16
    c_all = jnp.concatenate([c_prompt, c_sample, jnp.zeros((mc - bp - bs, d), F32)], axis=0)
    mod_all = ada_mod(c_all, w_ada, b_ada)
    mods_p = mod_all[:, :bp].reshape(depth, bp, 1, 6 * d)
    mods_s = mod_all[:, bp:bp + bs].reshape(depth, bs, 1, 6 * d)

    fox_main = w_in_odd.shape[-1] - fox_heads
    w_fl = jnp.pad(w_in_odd[:, :, fox_main:], ((0, 0), (0, 0), (0, HEAD_W - fox_heads))).astype(BF16)
    wts = (norm1, norm2, w_up.astype(BF16), w_down.astype(BF16), w_in_even.astype(BF16),
           w_out_even.astype(BF16), pool_w, pool_scale, diff_qn, diff_kn, diff_lam, diff_gn,
           w_in_odd[:, :, :fox_main].astype(BF16), w_fl, w_out_odd.astype(BF16),
           ret_gn, fox_qn, fox_kn, fox_fbias)

    dt = x_prompt.dtype
    y_p, pool_p, dk_p, dv_p, rs_p, fk_p, fv_p, fl_p = trunk(
        x_prompt, mods_p,
        jnp.zeros((n_pair, bp, POOL_MAX - 1, POOL_DIM), dt),
        jnp.zeros((n_pair, bp, 0, diff_heads * HEAD_W), BF16),
        jnp.zeros((n_pair, bp, 0, diff_heads * HEAD_W), BF16),
        jnp.zeros((n_pair, bp, ret_heads, RET_DK, RET_DV), F32),
        jnp.zeros((n_pair, bp, 0, fox_heads * HEAD_W), BF16),
        jnp.zeros((n_pair, bp, 0, fox_heads * HEAD_W), BF16),
        jnp.zeros((n_pair, bp, 0, fox_heads), F32),
        wts)
    y_s, pool_s, dk_s, dv_s, rs_s, fk_s, fv_s, fl_s = trunk(
        x_sample, mods_s, cache_pool, cache_rows(cache_diff_k), cache_rows(cache_diff_v), state_ret,
        cache_rows(cache_fox_k), cache_rows(cache_fox_v), cache_fox_logf, wts)
    return (y_p, y_s, pool_p, pool_s, dk_p, dk_s, dv_p, dv_s, rs_p, rs_s,
            fk_p, fk_s, fv_p, fv_s, fl_p, fl_s)
```

```python
import functools
import math

import jax
import jax.numpy as jnp
from jax import lax
from jax.experimental import pallas as pl
from jax.experimental.pallas import tpu as pltpu

F32 = jnp.float32
BF16 = jnp.bfloat16
EPS = 1e-6
CHUNK = 64

POOL_WINDOWS = (2, 4, 8, 16)
POOL_MAX = max(POOL_WINDOWS)
POOL_GDIM = 128
POOL_DIM = POOL_GDIM * len(POOL_WINDOWS)

HEAD_W = 128
DIFF_DH = 64
ROPE_DIM = DIFF_DH // 4
ROPE_THETA = 500000.0

RET_DK = 256
RET_DV = 256
RET_THETA = 10000.0

FOX_DH = 128

LOG2E = math.log2(math.e)
PACK_ROWS = 16
FLASH_UNROLL = 2

VMEM_LIMIT_BYTES = 56 * 1024 * 1024


def _cparams(*sem):
    return pltpu.CompilerParams(dimension_semantics=sem, vmem_limit_bytes=VMEM_LIMIT_BYTES)


def _pick(n, pref):
    t = min(pref, n)
    while n % t:
        t //= 2
    return t


def _token_tile(b, l, rows):
    if l >= rows:
        return 1, _pick(l, rows)
    return _pick(b, max(rows // l, 1)), l


def _silu(x):
    return x / (1.0 + jnp.exp(-x))


def _ada_kernel(c_ref, w_ref, b_ref, o_ref):
    c = c_ref[...]
    ca = _silu(c).astype(BF16)
    o_ref[...] = jnp.dot(ca, w_ref[...].astype(BF16), preferred_element_type=F32) + b_ref[...]


def ada_mod(c_all, w_ada, b_ada):
    depth, d, n = w_ada.shape
    mc = c_all.shape[0]
    bn = _pick(n, 1024)
    return pl.pallas_call(
        _ada_kernel,
        grid=(depth, n // bn),
        in_specs=[pl.BlockSpec((mc, d), lambda l, j: (0, 0)),
                  pl.BlockSpec((None, d, bn), lambda l, j: (l, 0, j)),
                  pl.BlockSpec((None, 1, bn), lambda l, j: (l, 0, j))],
        out_specs=pl.BlockSpec((None, mc, bn), lambda l, j: (l, 0, j)),
        out_shape=jax.ShapeDtypeStruct((depth, mc, n), F32),
        compiler_params=_cparams("arbitrary", "arbitrary"),
        name="ada_mod",
    )(c_all, w_ada, b_ada.reshape(depth, 1, n))


def _norm_mod_kernel(x_ref, g_ref, sh_ref, sc_ref, o_ref):
    x = x_ref[...]
    ms = jnp.mean(x * x, axis=-1, keepdims=True)
    y = x * lax.rsqrt(ms + EPS) * g_ref[...]
    o_ref[...] = (y * (1.0 + sc_ref[...]) + sh_ref[...]).astype(o_ref.dtype)


def norm_mod(x, g, mod, shift_idx, scale_idx):
    b, l, d = x.shape
    bb, bl = _token_tile(b, l, 512)
    return pl.pallas_call(
        _norm_mod_kernel,
        grid=(b // bb, l // bl),
        in_specs=[pl.BlockSpec((bb, bl, d), lambda ib, il: (ib, il, 0)),
                  pl.BlockSpec((1, d), lambda ib, il: (0, 0)),
                  pl.BlockSpec((bb, 1, d), lambda ib, il: (ib, 0, shift_idx)),
                  pl.BlockSpec((bb, 1, d), lambda ib, il: (ib, 0, scale_idx))],
        out_specs=pl.BlockSpec((bb, bl, d), lambda ib, il: (ib, il, 0)),
        out_shape=jax.ShapeDtypeStruct((b, l, d), BF16),
        compiler_params=_cparams("arbitrary", "arbitrary"),
        name="norm_mod",
    )(x, g.reshape(1, d), mod, mod)


def _mm_kernel(*refs, n_lhs, nk, epilogue):
    lhs = refs[:n_lhs]
    ws = refs[n_lhs:2 * n_lhs]
    pos = 2 * n_lhs
    if epilogue == "resid":
        xres_ref, gate_ref = refs[pos], refs[pos + 1]
        pos += 2
    o_ref = refs[pos]

    def compute():
        acc = None
        for a_ref, w_ref in zip(lhs, ws):
            a = a_ref[...]
            a = a.reshape(-1, a.shape[-1])
            part = jnp.dot(a, w_ref[...], preferred_element_type=F32)
            acc = part if acc is None else acc + part
        return acc

    def finish(acc):
        if epilogue == "relu2":
            r = jnp.maximum(acc, 0.0)
            y = r * r
        elif epilogue == "resid":
            y = xres_ref[...] + gate_ref[...] * acc.reshape(o_ref.shape)
        else:
            y = acc
        o_ref[...] = y.reshape(o_ref.shape).astype(o_ref.dtype)

    if nk == 1:
        finish(compute())
    else:
        acc_ref = refs[pos + 1]
        k = pl.program_id(3)

        @pl.when(k == 0)
        def _():
            acc_ref[...] = compute()

        @pl.when(k > 0)
        def _():
            acc_ref[...] += compute()

        @pl.when(k == nk - 1)
        def _():
            finish(acc_ref[...])


def matmul(lhs_list, w_list, out_dtype, epilogue="none", xres=None, mod=None, gate_idx=0,
           rows=1024, cols=1024, bk=None):
    b, l, _ = lhs_list[0].shape
    n = w_list[0].shape[1]
    bb, bl = _token_tile(b, l, rows)
    bn = _pick(n, cols)
    ks = [a.shape[-1] for a in lhs_list]
    nk = 1 if bk is None else ks[0] // bk
    assert nk == 1 or len(lhs_list) == 1
    bks = ks if nk == 1 else [bk]
    in_specs = [pl.BlockSpec((bb, bl, kk), lambda ib, il, j, k: (ib, il, k)) for kk in bks]
    in_specs += [pl.BlockSpec((kk, bn), lambda ib, il, j, k: (k, j)) for kk in bks]
    args = list(lhs_list) + list(w_list)
    if epilogue == "resid":
        gate_off = gate_idx * (n // bn)
        in_specs += [pl.BlockSpec((bb, bl, bn), lambda ib, il, j, k: (ib, il, j)),
                     pl.BlockSpec((bb, 1, bn), lambda ib, il, j, k: (ib, 0, gate_off + j))]
        args += [xres, mod]
    scratch = [pltpu.VMEM((bb * bl, bn), F32)] if nk > 1 else []
    return pl.pallas_call(
        functools.partial(_mm_kernel, n_lhs=len(lhs_list), nk=nk, epilogue=epilogue),
        grid=(b // bb, l // bl, n // bn, nk),
        in_specs=in_specs,
        out_specs=pl.BlockSpec((bb, bl, bn), lambda ib, il, j, k: (ib, il, j)),
        out_shape=jax.ShapeDtypeStruct((b, l, n), out_dtype),
        scratch_shapes=scratch,
        compiler_params=_cparams("arbitrary", "arbitrary", "arbitrary", "arbitrary"),
        name="matmul_" + epilogue,
    )(*args)


def _prep_even_kernel(*refs, t, p, nl, heads, n_carry):
    proj_ref, prev_ref, c_ref, s1_ref, s2_ref, qg_ref, kg_ref, pw_ref, ps_ref = refs[:9]
    pool_ref, npool_ref, q_ref, k32_ref, kb_ref, v32_ref, vb_ref, full_ref = refs[9 + n_carry:]
    il = pl.program_id(1)

    @pl.when(il == 0)
    def _():
        full_ref[0:POOL_MAX, :] = prev_ref[...]

    @pl.when(il > 0)
    def _():
        full_ref[0:POOL_MAX, :] = full_ref[t:t + POOL_MAX, :]

    full_ref[POOL_MAX:POOL_MAX + t, :] = proj_ref[:, 0:POOL_DIM]
    pos1 = p + il * t + 1 + lax.broadcasted_iota(jnp.int32, (t, 1), 0)
    for g, w in enumerate(POOL_WINDOWS):
        sl = slice(g * POOL_GDIM, (g + 1) * POOL_GDIM)
        u = full_ref[POOL_MAX:POOL_MAX + t, sl]
        win = u
        for s in range(1, w):
            win = win + full_ref[POOL_MAX - s:POOL_MAX - s + t, sl]
        cnt = jnp.minimum(pos1, w).astype(F32)
        d = (win / cnt - u).astype(BF16)
        mixed = jnp.dot(d, pw_ref[g], preferred_element_type=F32) * ps_ref[:, sl]
        pool_ref[:, sl] = mixed.astype(pool_ref.dtype)

    @pl.when(il == nl - 1)
    def _():
        npool_ref[...] = full_ref[t + 1:t + POOL_MAX, :]

    seg = (lax.broadcasted_iota(jnp.int32, (HEAD_W, HEAD_W), 0) >> 6) == \
          (lax.broadcasted_iota(jnp.int32, (HEAD_W, HEAD_W), 1) >> 6)
    seg = seg.astype(F32).astype(BF16)
    rc, rs1, rs2 = c_ref[...], s1_ref[...], s2_ref[...]

    def norm_rope(x, g):
        x2 = x * x
        hi = x2.astype(BF16)
        lo = (x2 - hi.astype(F32)).astype(BF16)
        ss = (jnp.dot(lo, seg, preferred_element_type=F32)
              + jnp.dot(hi, seg, preferred_element_type=F32))
        y = x * lax.rsqrt(ss * (1.0 / DIFF_DH) + EPS) * g
        return y * rc + pltpu.roll(y, 8, 1) * rs2 + pltpu.roll(y, HEAD_W - 8, 1) * rs1

    width = heads * HEAD_W
    q_off, k_off, v_off = POOL_DIM, POOL_DIM + width, POOL_DIM + 2 * width
    qg, kg = qg_ref[...], kg_ref[...]
    ks, vs = [], []
    for h in range(heads):
        sl = slice(h * HEAD_W, (h + 1) * HEAD_W)
        rq = norm_rope(proj_ref[:, q_off + h * HEAD_W:q_off + (h + 1) * HEAD_W], qg)
        q_ref[:, sl] = (rq * (DIFF_DH ** -0.5 * LOG2E)).astype(BF16)
        rk = norm_rope(proj_ref[:, k_off + h * HEAD_W:k_off + (h + 1) * HEAD_W], kg)
        kb_ref[:, sl] = rk.astype(BF16)
        ks.append(rk)
        vs.append(proj_ref[:, v_off + h * HEAD_W:v_off + (h + 1) * HEAD_W])
    k32_ref[...] = pltpu.einshape("htd->thd", jnp.stack(ks, axis=0))
    v32_ref[...] = pltpu.einshape("htd->thd", jnp.stack(vs, axis=0))
    vb_ref[...] = proj_ref[:, v_off:v_off + width].astype(BF16)


def _diff_rope_tables(p, l):
    inv_freq = ROPE_THETA ** (-jnp.arange(0, ROPE_DIM, 2, dtype=F32) / ROPE_DIM)
    ang = (p + jnp.arange(l)).astype(F32)[:, None] * inv_freq[None, :]
    cos, sin = jnp.cos(ang), jnp.sin(ang)
    half = ROPE_DIM // 2
    rest = DIFF_DH - ROPE_DIM
    c = jnp.concatenate([cos, cos, jnp.ones((l, rest), F32)], axis=-1)
    s1 = jnp.concatenate([-sin, jnp.zeros((l, half + rest), F32)], axis=-1)
    s2 = jnp.concatenate([jnp.zeros((l, half), F32), sin, jnp.zeros((l, rest), F32)], axis=-1)
    return tuple(jnp.concatenate([a, a], axis=-1) for a in (c, s1, s2))


def prep_even(proj, pool_prev, p, qn, kn, pool_w, pool_scale, layer_slot, n_slots, carry):
    b, l, n_in = proj.shape
    width = (n_in - POOL_DIM) // 3
    heads = width // HEAD_W
    t = _pick(l, 256)
    nl = l // t
    prev16 = jnp.concatenate([jnp.zeros((b, 1, POOL_DIM), F32), pool_prev.astype(F32)], axis=1)
    rc, rs1, rs2 = _diff_rope_tables(p, l)
    qg = jnp.concatenate([qn, qn]).reshape(1, HEAD_W).astype(F32)
    kg = jnp.concatenate([kn, kn]).reshape(1, HEAD_W).astype(F32)
    tok = lambda w: pl.BlockSpec((None, t, w), lambda ib, il: (ib, il, 0))
    tab = pl.BlockSpec((t, HEAD_W), lambda ib, il: (il, 0))
    vec = pl.BlockSpec((1, HEAD_W), lambda ib, il: (0, 0))
    out_shapes = (
        jax.ShapeDtypeStruct((b, l, POOL_DIM), BF16),
        jax.ShapeDtypeStruct((b, POOL_MAX - 1, POOL_DIM), F32),
        jax.ShapeDtypeStruct((b, l, width), BF16),
        jax.ShapeDtypeStruct((n_slots, b, l, heads, HEAD_W), F32),
        jax.ShapeDtypeStruct((b, l, width), BF16),
        jax.ShapeDtypeStruct((n_slots, b, l, heads, HEAD_W), F32),
        jax.ShapeDtypeStruct((b, l, width), BF16),
    )
    stacked = pl.BlockSpec((None, None, t, heads, HEAD_W), lambda ib, il: (layer_slot, ib, il, 0, 0))
    return pl.pallas_call(
        functools.partial(_prep_even_kernel, t=t, p=p, nl=nl, heads=heads, n_carry=len(carry)),
        grid=(b, nl),
        in_specs=[tok(n_in),
                  pl.BlockSpec((None, POOL_MAX, POOL_DIM), lambda ib, il: (ib, 0, 0)),
                  tab, tab, tab, vec, vec,
                  pl.BlockSpec((len(POOL_WINDOWS), POOL_GDIM, POOL_GDIM), lambda ib, il: (0, 0, 0)),
                  pl.BlockSpec((1, POOL_DIM), lambda ib, il: (0, 0))]
                 + [pl.BlockSpec(memory_space=pl.ANY)] * len(carry),
        out_specs=(tok(POOL_DIM),
                   pl.BlockSpec((None, POOL_MAX - 1, POOL_DIM), lambda ib, il: (ib, 0, 0)),
                   tok(width), stacked, tok(width), stacked, tok(width)),
        out_shape=out_shapes,
        scratch_shapes=[pltpu.VMEM((POOL_MAX + t, POOL_DIM), F32)],
        input_output_aliases={9 + i: o for i, o in zip(range(len(carry)), (3, 5))},
        compiler_params=_cparams("arbitrary", "arbitrary"),
        name="prep_even",
    )(proj, prev16, rc, rs1, rs2, qg, kg, pool_w.astype(BF16), pool_scale.reshape(1, POOL_DIM), *carry)


def _prep_odd_kernel(*refs, heads, n_carry):
    fq_ref, fk_ref, fv_ref, fl_ref, qg_ref, kg_ref, fb_ref = refs[:7]
    q_ref, k32_ref, kb_ref, v32_ref, vb_ref, lf_ref = refs[7 + n_carry:]
    def rms(x, g):
        return x * lax.rsqrt(jnp.mean(x * x, axis=-1, keepdims=True) + EPS) * g

    qg, kg = qg_ref[...], kg_ref[...]
    ks, vs = [], []
    for h in range(heads):
        sl = slice(h * HEAD_W, (h + 1) * HEAD_W)
        q_ref[:, sl] = (rms(fq_ref[:, sl], qg) * (FOX_DH ** -0.5 * LOG2E)).astype(BF16)
        rk = rms(fk_ref[:, sl], kg)
        kb_ref[:, sl] = rk.astype(BF16)
        ks.append(rk)
        vs.append(fv_ref[:, sl])
    k32_ref[...] = pltpu.einshape("htd->thd", jnp.stack(ks, axis=0))
    v32_ref[...] = pltpu.einshape("htd->thd", jnp.stack(vs, axis=0))
    vb_ref[...] = fv_ref[...].astype(BF16)
    x = fl_ref[...] + fb_ref[...]
    logf = -(jnp.maximum(-x, 0.0) + jnp.log1p(jnp.exp(-jnp.abs(x))))
    lf_ref[...] = logf[:, 0:heads]


def prep_odd(proj, fl, fox_off, heads, qn, kn, f_bias, layer_slot, n_slots, carry):
    b, l, _ = proj.shape
    width = heads * HEAD_W
    t = _pick(l, 512)
    cb = fox_off // width
    tok = lambda w, c: pl.BlockSpec((None, t, w), lambda ib, il: (ib, il, c))
    vec = pl.BlockSpec((1, HEAD_W), lambda ib, il: (0, 0))
    fb = jnp.zeros((1, HEAD_W), F32).at[0, :heads].set(f_bias.astype(F32))
    out_shapes = (
        jax.ShapeDtypeStruct((b, l, width), BF16),
        jax.ShapeDtypeStruct((n_slots, b, l, heads, HEAD_W), F32),
        jax.ShapeDtypeStruct((b, l, width), BF16),
        jax.ShapeDtypeStruct((n_slots, b, l, heads, HEAD_W), F32),
        jax.ShapeDtypeStruct((b, l, width), BF16),
        jax.ShapeDtypeStruct((b, l, heads), F32),
    )
    stacked = pl.BlockSpec((None, None, t, heads, HEAD_W), lambda ib, il: (layer_slot, ib, il, 0, 0))
    return pl.pallas_call(
        functools.partial(_prep_odd_kernel, heads=heads, n_carry=len(carry)),
        grid=(b, l // t),
        in_specs=[tok(width, cb), tok(width, cb + 1), tok(width, cb + 2), tok(HEAD_W, 0),
                  vec, vec, vec] + [pl.BlockSpec(memory_space=pl.ANY)] * len(carry),
        out_specs=(tok(width, 0), stacked, tok(width, 0), stacked, tok(width, 0),
                   tok(heads, 0)),
        out_shape=out_shapes,
        input_output_aliases={7 + i: o for i, o in zip(range(len(carry)), (1, 3))},
        compiler_params=_cparams("arbitrary", "arbitrary"),
        name="prep_odd",
    )(proj, proj, proj, fl, qn.reshape(1, HEAD_W), kn.reshape(1, HEAD_W), fb, *carry)


CUMSUM_BLOCK = 256


def _cumsum_kernel(x_ref, o_ref, carry_ref):
    @pl.when(pl.program_id(0) == 0)
    def _():
        carry_ref[...] = jnp.zeros_like(carry_ref)

    x = x_ref[...]
    x1 = x.astype(BF16)
    r1 = x - x1.astype(F32)
    x2 = r1.astype(BF16)
    x3 = (r1 - x2.astype(F32)).astype(BF16)
    n = x.shape[-1]
    tri = lax.broadcasted_iota(jnp.int32, (n, n), 0) <= lax.broadcasted_iota(jnp.int32, (n, n), 1)
    tri = tri.astype(F32).astype(BF16)
    c = (jnp.dot(x3, tri, preferred_element_type=F32)
         + jnp.dot(x2, tri, preferred_element_type=F32)
         + jnp.dot(x1, tri, preferred_element_type=F32)) + carry_ref[...]
    o_ref[...] = c
    carry_ref[...] = c[:, n - 1:n]


def cumsum_lanes(x):
    r, n = x.shape
    return pl.pallas_call(
        _cumsum_kernel,
        grid=(n // CUMSUM_BLOCK,),
        in_specs=[pl.BlockSpec((r, CUMSUM_BLOCK), lambda i: (0, i))],
        out_specs=pl.BlockSpec((r, CUMSUM_BLOCK), lambda i: (0, i)),
        out_shape=jax.ShapeDtypeStruct((r, n), F32),
        scratch_shapes=[pltpu.VMEM((r, 1), F32)],
        compiler_params=_cparams("arbitrary"),
        name="cumsum",
    )(x)


def _flash_kernel(*refs, kind, tq, tk, p, lam_init, rq, single_tile, has_past):
    if has_past:
        pk_ref, pv_ref, *refs = refs
    if kind == "diff":
        (q_ref, k_ref, v_ref, lv_ref, gn_ref, o_ref,
         qzt_scr, s0_scr, s1_scr, mx0_scr, mx1_scr, p0_scr, p1_scr, acc_scr, l_scr) = refs
    else:
        (q_ref, k_ref, v_ref, cq_ref, ck_ref, o_ref,
         qzt_scr, s0_scr, s1_scr, mx0_scr, mx1_scr, p0_scr, p1_scr, acc_scr, l_scr,
         ckcol_ref) = refs
    s_scr, mx_scr, p_scr = (s0_scr, s1_scr), (mx0_scr, mx1_scr), (p0_scr, p1_scr)
    iq = pl.program_id(2)
    q0 = iq * tq
    qf = q_ref[...].astype(F32)
    if kind == "diff":
        lane = lax.broadcasted_iota(jnp.int32, (tq, HEAD_W), 1)
        qz = jnp.concatenate([jnp.where(lane < DIFF_DH, qf, 0.0),
                              jnp.where(lane >= DIFF_DH, qf, 0.0)], axis=0)
    else:
        qz = qf if rq == tq else jnp.concatenate([qf, jnp.zeros((rq - tq, HEAD_W), F32)], axis=0)
        cq = cq_ref[pl.ds(iq, 1), :] * LOG2E

        @pl.when(iq == 0)
        def _():
            def fill(c, carry):
                row = jnp.broadcast_to(ck_ref[pl.ds(c, 1), :] * LOG2E, (HEAD_W, HEAD_W))
                ckcol_ref[pl.ds(pl.multiple_of(c * HEAD_W, HEAD_W), HEAD_W), :] = row.T
                return carry
            lax.fori_loop(0, ck_ref.shape[0], fill, 0)

    qzt_scr[...] = qz.T.astype(BF16)
    acc_scr[...] = jnp.zeros_like(acc_scr)

    def produce(slot, j, masked):
        start = pl.multiple_of(j * tk, tk)
        if has_past:
            s = jnp.concatenate(
                [jnp.dot(pk_ref[...].astype(BF16), qzt_scr[...], preferred_element_type=F32),
                 jnp.dot(k_ref[...], qzt_scr[...], preferred_element_type=F32)], axis=0)
        else:
            s = jnp.dot(k_ref[pl.ds(start, tk), :], qzt_scr[...], preferred_element_type=F32)
        if kind == "fox":
            ckc = ckcol_ref[pl.ds(start, tk), :]
            s = jnp.concatenate([s[:, c * HEAD_W:(c + 1) * HEAD_W] - ckc
                                 for c in range(rq // HEAD_W)], axis=1)
        if masked:
            kpos = start + lax.broadcasted_iota(jnp.int32, (tk, rq), 0)
            r = lax.broadcasted_iota(jnp.int32, (tk, rq), 1)
            if kind == "diff":
                qpos = p + q0 + jnp.where(r >= tq, r - tq, r)
                ok = (kpos >> 6) <= (qpos >> 6)
            else:
                ok = kpos <= p + q0 + r
            s = jnp.where(ok, s, -jnp.inf)
        s_scr[slot][...] = s
        mx_scr[slot][...] = jnp.max(s, axis=0, keepdims=True)

    def consume(slot, j, m, l):
        tmax = mx_scr[slot][...]
        if kind == "fox":
            m_new = jnp.maximum(m, tmax + cq)
            shift = cq - m_new
        else:
            m_new = jnp.maximum(m, tmax)
            shift = -m_new
        alpha = jnp.exp2(m - m_new)
        shift_b = jnp.broadcast_to(shift, (PACK_ROWS, rq))
        lsum = jnp.zeros((PACK_ROWS, rq), F32)
        for c in range(tk // PACK_ROWS):
            rows = slice(c * PACK_ROWS, (c + 1) * PACK_ROWS)
            pr = jnp.exp2(s_scr[slot][rows, :] + shift_b)
            lsum = lsum + pr
            p_scr[slot][rows, :] = pr.astype(BF16)
        start = pl.multiple_of(j * tk, tk)
        tn = (((0,), (0,)), ((), ()))
        if has_past:
            pv = (lax.dot_general(pv_ref[...].astype(BF16), p_scr[slot][0:p, :], tn,
                                  preferred_element_type=F32)
                  + lax.dot_general(v_ref[...], p_scr[slot][p:tk, :], tn,
                                    preferred_element_type=F32))
        else:
            pv = lax.dot_general(v_ref[pl.ds(start, tk), :], p_scr[slot][...], tn,
                                 preferred_element_type=F32)
        acc_scr[...] = acc_scr[...] * alpha + pv
        return m_new, alpha * l + jnp.sum(lsum, axis=0, keepdims=True)

    def run(j0, count, m, l, ends_masked):
        for i in range(count):
            slot = i % 2
            if i + 1 < count:
                produce(1 - slot, j0 + i + 1, ends_masked and i + 2 == count)
            elif not ends_masked:
                produce(1 - slot, j0 + count, False)
            m, l = consume(slot, j0 + i, m, l)
        return m, l

    m0 = jnp.full((1, rq), -jnp.inf, F32)
    l0 = jnp.zeros((1, rq), F32)
    if single_tile:
        produce(0, 0, True)
        l_scr[...] = run(0, 1, m0, l0, True)[1]
    else:
        n_full = lax.div(p + q0 + tq + tk - 1, tk) - 1
        n_iter = lax.div(jnp.maximum(n_full - 1, 0), FLASH_UNROLL)

        @pl.when(n_full > 0)
        def _():
            produce(0, 0, False)

        m, l = lax.fori_loop(
            0, n_iter, lambda t, c: run(FLASH_UNROLL * t, FLASH_UNROLL, c[0], c[1], False), (m0, l0))
        first = FLASH_UNROLL * n_iter
        rest = n_full - first

        @pl.when(n_full == 0)
        def _():
            produce(0, 0, True)
            l_scr[...] = run(0, 1, m, l, True)[1]

        for r in range(1, FLASH_UNROLL + 1):
            @pl.when(jnp.logical_and(n_full > 0, rest == r))
            def _(r=r):
                l_scr[...] = run(first, r + 1, m, l, True)[1]

    o = (acc_scr[...] / l_scr[...]).T
    if kind == "diff":
        lv = lv_ref[...]
        lam = (jnp.exp(jnp.sum(lv[0:1] * lv[1:2], axis=-1, keepdims=True))
               - jnp.exp(jnp.sum(lv[2:3] * lv[3:4], axis=-1, keepdims=True)) + lam_init)
        o = o[:tq] - lam * o[tq:]
        o = o * lax.rsqrt(jnp.mean(o * o, axis=-1, keepdims=True) + EPS) * gn_ref[...]
        o = o * (1.0 - lam_init)
    else:
        o = o[:tq]
    o_ref[...] = o.astype(o_ref.dtype)


def _head_major_kernel(x_ref, o_ref):
    o_ref[...] = pltpu.einshape("thd->htd", x_ref[...]).astype(o_ref.dtype)


def head_major(cache):
    n, b, p, heads, w = cache.shape
    pc = _pick(p, 512)
    return pl.pallas_call(
        _head_major_kernel,
        grid=(n, b, p // pc),
        in_specs=[pl.BlockSpec((None, None, pc, heads, w), lambda j, ib, ip: (j, ib, ip, 0, 0))],
        out_specs=pl.BlockSpec((None, None, heads, pc, w), lambda j, ib, ip: (j, ib, 0, ip, 0)),
        out_shape=jax.ShapeDtypeStruct((n, b, heads, p, w), BF16),
        compiler_params=_cparams("arbitrary", "arbitrary", "arbitrary"),
        name="head_major",
    )(cache)


def flash(kind, q, k_new, v_new, past_k, past_v, layer_slot, tq_pref, tk_pref, extra, lam_init=0.0):
    b, l, width = q.shape
    heads = width // HEAD_W
    p = past_k.shape[3]
    has_past = p > 0
    tq = _pick(l, tq_pref)
    tk = p + l if has_past else _pick(l, tk_pref)
    assert tk % PACK_ROWS == 0 and (tk == p + l or tk % tq == 0)
    qspec = pl.BlockSpec((None, tq, HEAD_W), lambda ib, ih, iq: (ib, iq, ih))
    kvspec = pl.BlockSpec((None, l, HEAD_W), lambda ib, ih, iq: (ib, 0, ih))
    pastspec = pl.BlockSpec((None, None, None, p, HEAD_W), lambda ib, ih, iq: (layer_slot, ib, ih, 0, 0))
    past_specs, past_args = ([pastspec, pastspec], [past_k, past_v]) if has_past else ([], [])
    rq = 2 * tq if kind == "diff" else max(tq, HEAD_W)
    scratch = ([pltpu.VMEM((HEAD_W, rq), BF16)]
               + [pltpu.VMEM((tk, rq), F32)] * 2
               + [pltpu.VMEM((1, rq), F32)] * 2
               + [pltpu.VMEM((tk, rq), BF16)] * 2
               + [pltpu.VMEM((HEAD_W, rq), F32),
                  pltpu.VMEM((1, rq), F32)])
    if kind == "diff":
        lam_vec, gn = extra
        especs = [pl.BlockSpec(lam_vec.shape, lambda ib, ih, iq: (0, 0)),
                  pl.BlockSpec((1, HEAD_W), lambda ib, ih, iq: (0, 0))]
        eargs = [lam_vec, gn.reshape(1, HEAD_W)]
    else:
        cum = extra
        lc = cum.shape[-1]
        assert lc % HEAD_W == 0 and lc >= p + l
        cq = cum[:, :, p:p + l].reshape(b, heads, l // tq, tq)
        if rq > tq:
            cq = jnp.pad(cq, ((0, 0), (0, 0), (0, 0), (0, rq - tq)))
        ck = cum.reshape(b, heads, lc // HEAD_W, HEAD_W)
        especs = [pl.BlockSpec((None, None, l // tq, rq), lambda ib, ih, iq: (ib, ih, 0, 0)),
                  pl.BlockSpec((None, None, lc // HEAD_W, HEAD_W), lambda ib, ih, iq: (ib, ih, 0, 0))]
        eargs = [cq, ck]
        scratch.append(pltpu.VMEM((lc, HEAD_W), F32))
    return pl.pallas_call(
        functools.partial(_flash_kernel, kind=kind, tq=tq, tk=tk, p=p, lam_init=lam_init, rq=rq,
                          single_tile=(tk == p + l), has_past=has_past),
        grid=(b, heads, l // tq),
        in_specs=past_specs + [qspec, kvspec, kvspec] + especs,
        out_specs=qspec,
        out_shape=jax.ShapeDtypeStruct((b, l, width), BF16),
        scratch_shapes=scratch,
        compiler_params=_cparams("arbitrary", "arbitrary", "arbitrary"),
        name="flash_" + kind,
    )(*past_args, q, k_new, v_new, *eargs)


def _ret_kernel(lg_ref, q_ref, k_ref, v_ref, g_ref, cos_ref, sin_ref, s0_ref, gn_ref,
                o_ref, sn_ref, s_scr, *, c, nc):
    t = pl.program_id(2)

    @pl.when(t == 0)
    def _():
        s_scr[...] = s0_ref[...]

    cos, sin = cos_ref[...], sin_ref[...]
    half = RET_DK // 2

    def rot(x):
        x1, x2 = x[:, :half], x[:, half:]
        return jnp.concatenate([x1 * cos - x2 * sin, x2 * cos + x1 * sin], axis=-1)

    q = rot(q_ref[...])
    k = rot(k_ref[...]) * RET_DK ** -0.5
    vb = v_ref[...].astype(BF16)
    lg = lg_ref[pl.program_id(1)]
    dist = (lax.broadcasted_iota(jnp.int32, (c, c), 0)
            - lax.broadcasted_iota(jnp.int32, (c, c), 1)).astype(F32)
    intra = jnp.where(dist >= 0, jnp.exp(lg * jnp.maximum(dist, 0.0)), 0.0)
    ic = lax.broadcasted_iota(jnp.int32, (c, 1), 0).astype(F32)
    q_dec = jnp.exp(lg * (ic + 1.0))
    k_dec = jnp.exp(lg * (c - 1.0 - ic))
    blk_dec = jnp.exp(lg * jnp.full((1, RET_DV), float(c), F32))
    att = lax.dot_general(q.astype(BF16), k.astype(BF16), (((1,), (1,)), ((), ())),
                          preferred_element_type=F32) * intra
    s = s_scr[...]
    o = (jnp.dot(att.astype(BF16), vb, preferred_element_type=F32)
         + jnp.dot((q * q_dec).astype(BF16), s.astype(BF16), preferred_element_type=F32))
    s_new = s * blk_dec + lax.dot_general((k * k_dec).astype(BF16), vb, (((0,), (0,)), ((), ())),
                                          preferred_element_type=F32)
    s_scr[...] = s_new
    mu = jnp.mean(o, axis=-1, keepdims=True)
    oc = o - mu
    var = jnp.mean(oc * oc, axis=-1, keepdims=True)
    y = oc * lax.rsqrt(var + EPS) * gn_ref[...]
    o_ref[...] = (y * _silu(g_ref[...])).astype(o_ref.dtype)

    @pl.when(t == nc - 1)
    def _():
        sn_ref[...] = s_new


def retention(proj, p, s0, ret_gn):
    b, l, _ = proj.shape
    heads = s0.shape[1]
    c = _pick(l, 256)
    nc = l // c
    inv_freq = RET_THETA ** (-jnp.arange(0, RET_DK, 2, dtype=F32) / RET_DK)
    ang = (p + jnp.arange(l)).astype(F32)[:, None] * inv_freq[None, :]
    cos, sin = jnp.cos(ang), jnp.sin(ang)
    lg = jnp.asarray([math.log(1.0 - 2.0 ** (-5 - h)) for h in range(heads)], F32)
    col = lambda off: pl.BlockSpec((None, c, RET_DK), lambda ib, ih, it: (ib, it, off + ih))
    tab = pl.BlockSpec((c, RET_DK // 2), lambda ib, ih, it: (it, 0))
    st = pl.BlockSpec((None, None, RET_DK, RET_DV), lambda ib, ih, it: (ib, ih, 0, 0))
    return pl.pallas_call(
        functools.partial(_ret_kernel, c=c, nc=nc),
        grid=(b, heads, nc),
        in_specs=[pl.BlockSpec(memory_space=pltpu.SMEM),
                  col(0), col(heads), col(2 * heads), col(3 * heads), tab, tab, st,
                  pl.BlockSpec((None, 1, RET_DV), lambda ib, ih, it: (ih, 0, 0))],
        out_specs=(pl.BlockSpec((None, c, RET_DV), lambda ib, ih, it: (ib, it, ih)), st),
        out_shape=(jax.ShapeDtypeStruct((b, l, heads * RET_DV), BF16),
                   jax.ShapeDtypeStruct(s0.shape, F32)),
        scratch_shapes=[pltpu.VMEM((RET_DK, RET_DV), F32)],
        compiler_params=_cparams("arbitrary", "arbitrary", "arbitrary"),
        name="retention",
    )(lg, proj, proj, proj, proj, cos, sin, s0.astype(F32), ret_gn.reshape(heads, 1, RET_DV))


def even_mixer(h, x, mod, pool_prev, past_k, past_v, w_in, w_out, pool_w, pool_scale,
               qn, kn, lam_vec, out_gn, lam_init, layer_slot, n_slots, carry):
    p = past_k.shape[3]
    proj = matmul([h], [w_in], F32)
    pool_out, new_pool, q, k32, kb, v32, vb = prep_even(
        proj, pool_prev, p, qn, kn, pool_w, pool_scale, layer_slot, n_slots, carry)
    o = flash("diff", q, kb, vb, past_k, past_v, layer_slot, 256, 512,
              (lam_vec.astype(F32), out_gn.astype(F32)), lam_init)
    x = matmul([pool_out, o], [w_out[:POOL_DIM], w_out[POOL_DIM:]], F32, "resid", x, mod, 2)
    return x, new_pool, k32, v32


def odd_mixer(h, x, mod, s0, past_k, past_v, past_logf, w_in, w_fl, w_out, ret_gn, qn, kn, f_bias,
              layer_slot, n_slots, carry):
    p = past_k.shape[3]
    b, l, _ = h.shape
    ret_heads = s0.shape[1]
    fox_heads = f_bias.shape[0]
    proj = matmul([h], [w_in], F32)
    fl = matmul([h], [w_fl], F32)
    ret, s_new = retention(proj, p, s0, ret_gn)
    fox_off = 2 * ret_heads * RET_DK + 2 * ret_heads * RET_DV
    q, k32, kb, v32, vb, logf = prep_odd(proj, fl, fox_off, fox_heads, qn, kn, f_bias,
                                         layer_slot, n_slots, carry)
    lc = -(-(p + l) // CUMSUM_BLOCK) * CUMSUM_BLOCK
    lf_all = jnp.concatenate([past_logf.astype(F32), logf,
                              jnp.zeros((b, lc - p - l, fox_heads), F32)], axis=1)
    cum = cumsum_lanes(jnp.swapaxes(lf_all, 1, 2).reshape(b * fox_heads, lc))
    o = flash("fox", q, kb, vb, past_k, past_v, layer_slot, 512, 512, cum.reshape(b, fox_heads, lc))
    x = matmul([ret, o], [w_out[:ret.shape[-1]], w_out[ret.shape[-1]:]], F32, "resid", x, mod, 2)
    return x, s_new, k32, v32, logf


def trunk(x, mods, pool_prev, diff_k, diff_v, ret_s, fox_k, fox_v, fox_logf, wts):
    (norm1, norm2, w_up, w_down, w_in_even, w_out_even, pool_w, pool_scale, diff_qn, diff_kn,
     diff_lam, diff_gn, w_in_odd, w_fl_odd, w_out_odd, ret_gn, fox_qn, fox_kn, fox_fbias) = wts
    depth = norm1.shape[0]
    n_pair = depth // 2
    b, seq, _ = x.shape
    n_pool, n_rs, n_fl = [], [], []
    diff_kv, fox_kv = (), ()
    for l in range(depth):
        j = l // 2
        mod = mods[l]
        h = norm_mod(x, norm1[l], mod, 0, 1)
        if l % 2 == 0:
            x, p_new, *diff_kv = even_mixer(
                h, x, mod, pool_prev[j], diff_k, diff_v, w_in_even[j], w_out_even[j], pool_w[j],
                pool_scale[j], diff_qn[j], diff_kn[j], diff_lam[j], diff_gn[j],
                0.8 - 0.6 * math.exp(-0.3 * l), j, n_pair, tuple(diff_kv))
            n_pool.append(p_new)
        else:
            x, s_new, *fox_kv, lf_new = odd_mixer(
                h, x, mod, ret_s[j], fox_k, fox_v, fox_logf[j], w_in_odd[j], w_fl_odd[j],
                w_out_odd[j], ret_gn[j], fox_qn[j], fox_kn[j], fox_fbias[j], j, n_pair, tuple(fox_kv))
            n_rs.append(s_new)
            n_fl.append(lf_new)
        h = norm_mod(x, norm2[l], mod, 3, 4)
        a = matmul([h], [w_up[l]], BF16, "relu2")
        x = matmul([a], [w_down[l]], F32, "resid", x, mod, 5, bk=_pick(a.shape[-1], 2048))
    st = jnp.stack
    return x, st(n_pool), diff_kv[0], diff_kv[1], st(n_rs), fox_kv[0], fox_kv[1], st(n_fl)


def kernel(x_prompt, x_sample, c_prompt, c_sample, cache_pool, cache_diff_k, cache_diff_v, state_ret,
           cache_fox_k, cache_fox_v, cache_fox_logf, w_ada, b_ada, norm1, norm2, w_up, w_down,
           w_in_even, w_out_even, pool_w, pool_scale, diff_qn, diff_kn, diff_lam, diff_gn,
           w_in_odd, w_out_odd, ret_gn, fox_qn, fox_kn, fox_fbias):
    bp, _, d = x_prompt.shape
    bs = x_sample.shape[0]
    n_pair = cache_pool.shape[0]
    depth = w_ada.shape[0]
    diff_heads = cache_diff_k.shape[3]
    ret_heads = state_ret.shape[2]
    fox_heads = cache_fox_k.shape[3]

    mc = -(-(bp + bs) // 16) * 16
    c_all = jnp.concatenate([c_prompt, c_sample, jnp.zeros((mc - bp - bs, d), F32)], axis=0)
    mod_all = ada_mod(c_all, w_ada, b_ada)
    mods_p = mod_all[:, :bp].reshape(depth, bp, 1, 6 * d)
    mods_s = mod_all[:, bp:bp + bs].reshape(depth, bs, 1, 6 * d)

    fox_main = w_in_odd.shape[-1] - fox_heads
    w_fl = jnp.pad(w_in_odd[:, :, fox_main:], ((0, 0), (0, 0), (0, HEAD_W - fox_heads))).astype(BF16)
    wts = (norm1, norm2, w_up.astype(BF16), w_down.astype(BF16), w_in_even.astype(BF16),
           w_out_even.astype(BF16), pool_w, pool_scale, diff_qn, diff_kn, diff_lam, diff_gn,
           w_in_odd[:, :, :fox_main].astype(BF16), w_fl, w_out_odd.astype(BF16),
           ret_gn, fox_qn, fox_kn, fox_fbias)

    dt = x_prompt.dtype
    y_p, pool_p, dk_p, dv_p, rs_p, fk_p, fv_p, fl_p = trunk(
        x_prompt, mods_p,
        jnp.zeros((n_pair, bp, POOL_MAX - 1, POOL_DIM), dt),
        jnp.zeros((n_pair, bp, diff_heads, 0, HEAD_W), BF16),
        jnp.zeros((n_pair, bp, diff_heads, 0, HEAD_W), BF16),
        jnp.zeros((n_pair, bp, ret_heads, RET_DK, RET_DV), F32),
        jnp.zeros((n_pair, bp, fox_heads, 0, HEAD_W), BF16),
        jnp.zeros((n_pair, bp, fox_heads, 0, HEAD_W), BF16),
        jnp.zeros((n_pair, bp, 0, fox_heads), F32),
        wts)

    def by_head(cache):
        if cache.shape[3] % 8:
            return jnp.swapaxes(cache, 2, 3)
        return head_major(cache)

    y_s, pool_s, dk_s, dv_s, rs_s, fk_s, fv_s, fl_s = trunk(
        x_sample, mods_s, cache_pool, by_head(cache_diff_k), by_head(cache_diff_v), state_ret,
        by_head(cache_fox_k), by_head(cache_fox_v), cache_fox_logf, wts)
    return (y_p, y_s, pool_p, pool_s, dk_p, dk_s, dv_p, dv_s, rs_p, rs_s,
            fk_p, fk_s, fv_p, fv_s, fl_p, fl_s)
```

```python
import functools
import math

import jax
import jax.numpy as jnp
from jax import lax
from jax.experimental import pallas as pl
from jax.experimental.pallas import tpu as pltpu

F32 = jnp.float32
BF16 = jnp.bfloat16
EPS = 1e-6
CHUNK = 64

POOL_WINDOWS = (2, 4, 8, 16)
POOL_MAX = max(POOL_WINDOWS)
POOL_GDIM = 128
POOL_DIM = POOL_GDIM * len(POOL_WINDOWS)

HEAD_W = 128
DIFF_DH = 64
ROPE_DIM = DIFF_DH // 4
ROPE_THETA = 500000.0

RET_DK = 256
RET_DV = 256
RET_THETA = 10000.0

FOX_DH = 128

LOG2E = math.log2(math.e)
PACK_ROWS = 16
JUMP_LIMIT = 64.0
FLASH_UNROLL = 2

VMEM_LIMIT_BYTES = 56 * 1024 * 1024


def _cparams(*sem):
    return pltpu.CompilerParams(dimension_semantics=sem, vmem_limit_bytes=VMEM_LIMIT_BYTES)


def _pick(n, pref):
    t = min(pref, n)
    while n % t:
        t //= 2
    return t


def _token_tile(b, l, rows):
    if l >= rows:
        return 1, _pick(l, rows)
    return _pick(b, max(rows // l, 1)), l


def _silu(x):
    return x / (1.0 + jnp.exp(-x))


def _ada_kernel(c_ref, w_ref, b_ref, o_ref):
    c = c_ref[...]
    ca = _silu(c).astype(BF16)
    o_ref[...] = jnp.dot(ca, w_ref[...].astype(BF16), preferred_element_type=F32) + b_ref[...]


def ada_mod(c_all, w_ada, b_ada):
    depth, d, n = w_ada.shape
    mc = c_all.shape[0]
    bn = _pick(n, 1024)
    return pl.pallas_call(
        _ada_kernel,
        grid=(depth, n // bn),
        in_specs=[pl.BlockSpec((mc, d), lambda l, j: (0, 0)),
                  pl.BlockSpec((None, d, bn), lambda l, j: (l, 0, j)),
                  pl.BlockSpec((None, 1, bn), lambda l, j: (l, 0, j))],
        out_specs=pl.BlockSpec((None, mc, bn), lambda l, j: (l, 0, j)),
        out_shape=jax.ShapeDtypeStruct((depth, mc, n), F32),
        compiler_params=_cparams("arbitrary", "arbitrary"),
        name="ada_mod",
    )(c_all, w_ada, b_ada.reshape(depth, 1, n))


def _norm_mod_kernel(x_ref, g_ref, sh_ref, sc_ref, o_ref):
    x = x_ref[...]
    ms = jnp.mean(x * x, axis=-1, keepdims=True)
    y = x * lax.rsqrt(ms + EPS) * g_ref[...]
    o_ref[...] = (y * (1.0 + sc_ref[...]) + sh_ref[...]).astype(o_ref.dtype)


def norm_mod(x, g, mod, shift_idx, scale_idx):
    b, l, d = x.shape
    bb, bl = _token_tile(b, l, 512)
    return pl.pallas_call(
        _norm_mod_kernel,
        grid=(b // bb, l // bl),
        in_specs=[pl.BlockSpec((bb, bl, d), lambda ib, il: (ib, il, 0)),
                  pl.BlockSpec((1, d), lambda ib, il: (0, 0)),
                  pl.BlockSpec((bb, 1, d), lambda ib, il: (ib, 0, shift_idx)),
                  pl.BlockSpec((bb, 1, d), lambda ib, il: (ib, 0, scale_idx))],
        out_specs=pl.BlockSpec((bb, bl, d), lambda ib, il: (ib, il, 0)),
        out_shape=jax.ShapeDtypeStruct((b, l, d), BF16),
        compiler_params=_cparams("arbitrary", "arbitrary"),
        name="norm_mod",
    )(x, g.reshape(1, d), mod, mod)


def _mm_kernel(*refs, n_lhs, nk, epilogue):
    lhs = refs[:n_lhs]
    ws = refs[n_lhs:2 * n_lhs]
    pos = 2 * n_lhs
    if epilogue == "resid":
        xres_ref, gate_ref = refs[pos], refs[pos + 1]
        pos += 2
    o_ref = refs[pos]

    def compute():
        acc = None
        for a_ref, w_ref in zip(lhs, ws):
            a = a_ref[...]
            a = a.reshape(-1, a.shape[-1])
            part = jnp.dot(a, w_ref[...], preferred_element_type=F32)
            acc = part if acc is None else acc + part
        return acc

    def finish(acc):
        if epilogue == "relu2":
            r = jnp.maximum(acc, 0.0)
            y = r * r
        elif epilogue == "resid":
            y = xres_ref[...] + gate_ref[...] * acc.reshape(o_ref.shape)
        else:
            y = acc
        o_ref[...] = y.reshape(o_ref.shape).astype(o_ref.dtype)

    if nk == 1:
        finish(compute())
    else:
        acc_ref = refs[pos + 1]
        k = pl.program_id(3)

        @pl.when(k == 0)
        def _():
            acc_ref[...] = compute()

        @pl.when(k > 0)
        def _():
            acc_ref[...] += compute()

        @pl.when(k == nk - 1)
        def _():
            finish(acc_ref[...])


def matmul(lhs_list, w_list, out_dtype, epilogue="none", xres=None, mod=None, gate_idx=0,
           rows=1024, cols=1024, bk=None):
    b, l, _ = lhs_list[0].shape
    n = w_list[0].shape[1]
    bb, bl = _token_tile(b, l, rows)
    bn = _pick(n, cols)
    ks = [a.shape[-1] for a in lhs_list]
    nk = 1 if bk is None else ks[0] // bk
    assert nk == 1 or len(lhs_list) == 1
    bks = ks if nk == 1 else [bk]
    in_specs = [pl.BlockSpec((bb, bl, kk), lambda ib, il, j, k: (ib, il, k)) for kk in bks]
    in_specs += [pl.BlockSpec((kk, bn), lambda ib, il, j, k: (k, j)) for kk in bks]
    args = list(lhs_list) + list(w_list)
    if epilogue == "resid":
        gate_off = gate_idx * (n // bn)
        in_specs += [pl.BlockSpec((bb, bl, bn), lambda ib, il, j, k: (ib, il, j)),
                     pl.BlockSpec((bb, 1, bn), lambda ib, il, j, k: (ib, 0, gate_off + j))]
        args += [xres, mod]
    scratch = [pltpu.VMEM((bb * bl, bn), F32)] if nk > 1 else []
    return pl.pallas_call(
        functools.partial(_mm_kernel, n_lhs=len(lhs_list), nk=nk, epilogue=epilogue),
        grid=(b // bb, l // bl, n // bn, nk),
        in_specs=in_specs,
        out_specs=pl.BlockSpec((bb, bl, bn), lambda ib, il, j, k: (ib, il, j)),
        out_shape=jax.ShapeDtypeStruct((b, l, n), out_dtype),
        scratch_shapes=scratch,
        compiler_params=_cparams("arbitrary", "arbitrary", "arbitrary", "arbitrary"),
        name="matmul_" + epilogue,
    )(*args)


def _prep_even_kernel(*refs, t, p, nl, heads, n_carry):
    proj_ref, prev_ref, c_ref, s1_ref, s2_ref, qg_ref, kg_ref, pw_ref, ps_ref = refs[:9]
    pool_ref, npool_ref, q_ref, k32_ref, kb_ref, v32_ref, vb_ref, full_ref = refs[9 + n_carry:]
    il = pl.program_id(1)

    @pl.when(il == 0)
    def _():
        full_ref[0:POOL_MAX, :] = prev_ref[...]

    @pl.when(il > 0)
    def _():
        full_ref[0:POOL_MAX, :] = full_ref[t:t + POOL_MAX, :]

    full_ref[POOL_MAX:POOL_MAX + t, :] = proj_ref[:, 0:POOL_DIM]
    pos1 = p + il * t + 1 + lax.broadcasted_iota(jnp.int32, (t, 1), 0)
    for g, w in enumerate(POOL_WINDOWS):
        sl = slice(g * POOL_GDIM, (g + 1) * POOL_GDIM)
        u = full_ref[POOL_MAX:POOL_MAX + t, sl]
        win = u
        for s in range(1, w):
            win = win + full_ref[POOL_MAX - s:POOL_MAX - s + t, sl]
        cnt = jnp.minimum(pos1, w).astype(F32)
        d = (win / cnt - u).astype(BF16)
        mixed = jnp.dot(d, pw_ref[g], preferred_element_type=F32) * ps_ref[:, sl]
        pool_ref[:, sl] = mixed.astype(pool_ref.dtype)

    @pl.when(il == nl - 1)
    def _():
        npool_ref[...] = full_ref[t + 1:t + POOL_MAX, :]

    seg = (lax.broadcasted_iota(jnp.int32, (HEAD_W, HEAD_W), 0) >> 6) == \
          (lax.broadcasted_iota(jnp.int32, (HEAD_W, HEAD_W), 1) >> 6)
    seg = seg.astype(F32).astype(BF16)
    rc, rs1, rs2 = c_ref[...], s1_ref[...], s2_ref[...]

    def norm_rope(x, g):
        x2 = x * x
        hi = x2.astype(BF16)
        lo = (x2 - hi.astype(F32)).astype(BF16)
        ss = (jnp.dot(lo, seg, preferred_element_type=F32)
              + jnp.dot(hi, seg, preferred_element_type=F32))
        y = x * lax.rsqrt(ss * (1.0 / DIFF_DH) + EPS) * g
        return y * rc + pltpu.roll(y, 8, 1) * rs2 + pltpu.roll(y, HEAD_W - 8, 1) * rs1

    width = heads * HEAD_W
    q_off, k_off, v_off = POOL_DIM, POOL_DIM + width, POOL_DIM + 2 * width
    qg, kg = qg_ref[...], kg_ref[...]
    ks, vs = [], []
    for h in range(heads):
        sl = slice(h * HEAD_W, (h + 1) * HEAD_W)
        rq = norm_rope(proj_ref[:, q_off + h * HEAD_W:q_off + (h + 1) * HEAD_W], qg)
        q_ref[:, sl] = (rq * (DIFF_DH ** -0.5 * LOG2E)).astype(BF16)
        rk = norm_rope(proj_ref[:, k_off + h * HEAD_W:k_off + (h + 1) * HEAD_W], kg)
        kb_ref[:, sl] = rk.astype(BF16)
        ks.append(rk)
        vs.append(proj_ref[:, v_off + h * HEAD_W:v_off + (h + 1) * HEAD_W])
    k32_ref[...] = pltpu.einshape("htd->thd", jnp.stack(ks, axis=0))
    v32_ref[...] = pltpu.einshape("htd->thd", jnp.stack(vs, axis=0))
    vb_ref[...] = proj_ref[:, v_off:v_off + width].astype(BF16)


def _diff_rope_tables(p, l):
    inv_freq = ROPE_THETA ** (-jnp.arange(0, ROPE_DIM, 2, dtype=F32) / ROPE_DIM)
    ang = (p + jnp.arange(l)).astype(F32)[:, None] * inv_freq[None, :]
    cos, sin = jnp.cos(ang), jnp.sin(ang)
    half = ROPE_DIM // 2
    rest = DIFF_DH - ROPE_DIM
    c = jnp.concatenate([cos, cos, jnp.ones((l, rest), F32)], axis=-1)
    s1 = jnp.concatenate([-sin, jnp.zeros((l, half + rest), F32)], axis=-1)
    s2 = jnp.concatenate([jnp.zeros((l, half), F32), sin, jnp.zeros((l, rest), F32)], axis=-1)
    return tuple(jnp.concatenate([a, a], axis=-1) for a in (c, s1, s2))


def prep_even(proj, pool_prev, p, qn, kn, pool_w, pool_scale, layer_slot, n_slots, carry):
    b, l, n_in = proj.shape
    width = (n_in - POOL_DIM) // 3
    heads = width // HEAD_W
    t = _pick(l, 256)
    nl = l // t
    prev16 = jnp.concatenate([jnp.zeros((b, 1, POOL_DIM), F32), pool_prev.astype(F32)], axis=1)
    rc, rs1, rs2 = _diff_rope_tables(p, l)
    qg = jnp.concatenate([qn, qn]).reshape(1, HEAD_W).astype(F32)
    kg = jnp.concatenate([kn, kn]).reshape(1, HEAD_W).astype(F32)
    tok = lambda w: pl.BlockSpec((None, t, w), lambda ib, il: (ib, il, 0))
    tab = pl.BlockSpec((t, HEAD_W), lambda ib, il: (il, 0))
    vec = pl.BlockSpec((1, HEAD_W), lambda ib, il: (0, 0))
    out_shapes = (
        jax.ShapeDtypeStruct((b, l, POOL_DIM), BF16),
        jax.ShapeDtypeStruct((b, POOL_MAX - 1, POOL_DIM), F32),
        jax.ShapeDtypeStruct((b, l, width), BF16),
        jax.ShapeDtypeStruct((n_slots, b, l, heads, HEAD_W), F32),
        jax.ShapeDtypeStruct((b, l, width), BF16),
        jax.ShapeDtypeStruct((n_slots, b, l, heads, HEAD_W), F32),
        jax.ShapeDtypeStruct((b, l, width), BF16),
    )
    stacked = pl.BlockSpec((None, None, t, heads, HEAD_W), lambda ib, il: (layer_slot, ib, il, 0, 0))
    return pl.pallas_call(
        functools.partial(_prep_even_kernel, t=t, p=p, nl=nl, heads=heads, n_carry=len(carry)),
        grid=(b, nl),
        in_specs=[tok(n_in),
                  pl.BlockSpec((None, POOL_MAX, POOL_DIM), lambda ib, il: (ib, 0, 0)),
                  tab, tab, tab, vec, vec,
                  pl.BlockSpec((len(POOL_WINDOWS), POOL_GDIM, POOL_GDIM), lambda ib, il: (0, 0, 0)),
                  pl.BlockSpec((1, POOL_DIM), lambda ib, il: (0, 0))]
                 + [pl.BlockSpec(memory_space=pl.ANY)] * len(carry),
        out_specs=(tok(POOL_DIM),
                   pl.BlockSpec((None, POOL_MAX - 1, POOL_DIM), lambda ib, il: (ib, 0, 0)),
                   tok(width), stacked, tok(width), stacked, tok(width)),
        out_shape=out_shapes,
        scratch_shapes=[pltpu.VMEM((POOL_MAX + t, POOL_DIM), F32)],
        input_output_aliases={9 + i: o for i, o in zip(range(len(carry)), (3, 5))},
        compiler_params=_cparams("arbitrary", "arbitrary"),
        name="prep_even",
    )(proj, prev16, rc, rs1, rs2, qg, kg, pool_w.astype(BF16), pool_scale.reshape(1, POOL_DIM), *carry)


def _prep_odd_kernel(*refs, heads, n_carry):
    fq_ref, fk_ref, fv_ref, fl_ref, qg_ref, kg_ref, fb_ref = refs[:7]
    q_ref, k32_ref, kb_ref, v32_ref, vb_ref, lf_ref = refs[7 + n_carry:]
    def rms(x, g):
        return x * lax.rsqrt(jnp.mean(x * x, axis=-1, keepdims=True) + EPS) * g

    qg, kg = qg_ref[...], kg_ref[...]
    ks, vs = [], []
    for h in range(heads):
        sl = slice(h * HEAD_W, (h + 1) * HEAD_W)
        q_ref[:, sl] = (rms(fq_ref[:, sl], qg) * (FOX_DH ** -0.5 * LOG2E)).astype(BF16)
        rk = rms(fk_ref[:, sl], kg)
        kb_ref[:, sl] = rk.astype(BF16)
        ks.append(rk)
        vs.append(fv_ref[:, sl])
    k32_ref[...] = pltpu.einshape("htd->thd", jnp.stack(ks, axis=0))
    v32_ref[...] = pltpu.einshape("htd->thd", jnp.stack(vs, axis=0))
    vb_ref[...] = fv_ref[...].astype(BF16)
    x = fl_ref[...] + fb_ref[...]
    logf = -(jnp.maximum(-x, 0.0) + jnp.log1p(jnp.exp(-jnp.abs(x))))
    lf_ref[...] = logf[:, 0:heads]


def prep_odd(proj, fl, fox_off, heads, qn, kn, f_bias, layer_slot, n_slots, carry):
    b, l, _ = proj.shape
    width = heads * HEAD_W
    t = _pick(l, 512)
    cb = fox_off // width
    tok = lambda w, c: pl.BlockSpec((None, t, w), lambda ib, il: (ib, il, c))
    vec = pl.BlockSpec((1, HEAD_W), lambda ib, il: (0, 0))
    fb = jnp.zeros((1, HEAD_W), F32).at[0, :heads].set(f_bias.astype(F32))
    out_shapes = (
        jax.ShapeDtypeStruct((b, l, width), BF16),
        jax.ShapeDtypeStruct((n_slots, b, l, heads, HEAD_W), F32),
        jax.ShapeDtypeStruct((b, l, width), BF16),
        jax.ShapeDtypeStruct((n_slots, b, l, heads, HEAD_W), F32),
        jax.ShapeDtypeStruct((b, l, width), BF16),
        jax.ShapeDtypeStruct((b, l, heads), F32),
    )
    stacked = pl.BlockSpec((None, None, t, heads, HEAD_W), lambda ib, il: (layer_slot, ib, il, 0, 0))
    return pl.pallas_call(
        functools.partial(_prep_odd_kernel, heads=heads, n_carry=len(carry)),
        grid=(b, l // t),
        in_specs=[tok(width, cb), tok(width, cb + 1), tok(width, cb + 2), tok(HEAD_W, 0),
                  vec, vec, vec] + [pl.BlockSpec(memory_space=pl.ANY)] * len(carry),
        out_specs=(tok(width, 0), stacked, tok(width, 0), stacked, tok(width, 0),
                   tok(heads, 0)),
        out_shape=out_shapes,
        input_output_aliases={7 + i: o for i, o in zip(range(len(carry)), (1, 3))},
        compiler_params=_cparams("arbitrary", "arbitrary"),
        name="prep_odd",
    )(proj, proj, proj, fl, qn.reshape(1, HEAD_W), kn.reshape(1, HEAD_W), fb, *carry)


CUMSUM_BLOCK = 256


def _cumsum_kernel(x_ref, o_ref, carry_ref):
    @pl.when(pl.program_id(0) == 0)
    def _():
        carry_ref[...] = jnp.zeros_like(carry_ref)

    x = x_ref[...]
    x1 = x.astype(BF16)
    r1 = x - x1.astype(F32)
    x2 = r1.astype(BF16)
    x3 = (r1 - x2.astype(F32)).astype(BF16)
    n = x.shape[-1]
    tri = lax.broadcasted_iota(jnp.int32, (n, n), 0) <= lax.broadcasted_iota(jnp.int32, (n, n), 1)
    tri = tri.astype(F32).astype(BF16)
    c = (jnp.dot(x3, tri, preferred_element_type=F32)
         + jnp.dot(x2, tri, preferred_element_type=F32)
         + jnp.dot(x1, tri, preferred_element_type=F32)) + carry_ref[...]
    o_ref[...] = c
    carry_ref[...] = c[:, n - 1:n]


def cumsum_lanes(x):
    r, n = x.shape
    return pl.pallas_call(
        _cumsum_kernel,
        grid=(n // CUMSUM_BLOCK,),
        in_specs=[pl.BlockSpec((r, CUMSUM_BLOCK), lambda i: (0, i))],
        out_specs=pl.BlockSpec((r, CUMSUM_BLOCK), lambda i: (0, i)),
        out_shape=jax.ShapeDtypeStruct((r, n), F32),
        scratch_shapes=[pltpu.VMEM((r, 1), F32)],
        compiler_params=_cparams("arbitrary"),
        name="cumsum",
    )(x)


def _flash_kernel(*refs, kind, tq, tk, p, lam_init, rq, single_tile, has_past):
    if has_past:
        pk_ref, pv_ref, *refs = refs
    if kind == "diff":
        (q_ref, k_ref, v_ref, lv_ref, gn_ref, o_ref,
         qzt_scr, s0_scr, s1_scr, mx0_scr, mx1_scr, p0_scr, p1_scr, acc_scr, l_scr, m_scr,
         jump_scr) = refs
    else:
        (q_ref, k_ref, v_ref, cq_ref, ck_ref, o_ref,
         qzt_scr, s0_scr, s1_scr, mx0_scr, mx1_scr, p0_scr, p1_scr, acc_scr, l_scr, m_scr,
         jump_scr, ckcol_ref) = refs
    s_scr, mx_scr, p_scr = (s0_scr, s1_scr), (mx0_scr, mx1_scr), (p0_scr, p1_scr)
    iq = pl.program_id(2)
    q0 = iq * tq
    qf = q_ref[...].astype(F32)
    if kind == "diff":
        lane = lax.broadcasted_iota(jnp.int32, (tq, HEAD_W), 1)
        qz = jnp.concatenate([jnp.where(lane < DIFF_DH, qf, 0.0),
                              jnp.where(lane >= DIFF_DH, qf, 0.0)], axis=0)
    else:
        qz = qf if rq == tq else jnp.concatenate([qf, jnp.zeros((rq - tq, HEAD_W), F32)], axis=0)
        cq = cq_ref[pl.ds(iq, 1), :] * LOG2E

        @pl.when(iq == 0)
        def _():
            def fill(c, carry):
                row = jnp.broadcast_to(ck_ref[pl.ds(c, 1), :] * LOG2E, (HEAD_W, HEAD_W))
                ckcol_ref[pl.ds(pl.multiple_of(c * HEAD_W, HEAD_W), HEAD_W), :] = row.T
                return carry
            lax.fori_loop(0, ck_ref.shape[0], fill, 0)

    qzt_scr[...] = qz.T.astype(BF16)
    acc_scr[...] = jnp.zeros_like(acc_scr)

    def scores(j, masked):
        start = pl.multiple_of(j * tk, tk)
        if has_past:
            s = jnp.concatenate(
                [jnp.dot(pk_ref[...].astype(BF16), qzt_scr[...], preferred_element_type=F32),
                 jnp.dot(k_ref[...], qzt_scr[...], preferred_element_type=F32)], axis=0)
        else:
            s = jnp.dot(k_ref[pl.ds(start, tk), :], qzt_scr[...], preferred_element_type=F32)
        if kind == "fox":
            ckc = ckcol_ref[pl.ds(start, tk), :]
            s = jnp.concatenate([s[:, c * HEAD_W:(c + 1) * HEAD_W] - ckc
                                 for c in range(rq // HEAD_W)], axis=1)
        if masked:
            kpos = start + lax.broadcasted_iota(jnp.int32, (tk, rq), 0)
            r = lax.broadcasted_iota(jnp.int32, (tk, rq), 1)
            if kind == "diff":
                qpos = p + q0 + jnp.where(r >= tq, r - tq, r)
                ok = (kpos >> 6) <= (qpos >> 6)
            else:
                ok = kpos <= p + q0 + r
            s = jnp.where(ok, s, -jnp.inf)
        return s

    def weighted_values(slot, j):
        tn = (((0,), (0,)), ((), ()))
        if has_past:
            return (lax.dot_general(pv_ref[...].astype(BF16), p_scr[slot][0:p, :], tn,
                                    preferred_element_type=F32)
                    + lax.dot_general(v_ref[...], p_scr[slot][p:tk, :], tn,
                                      preferred_element_type=F32))
        start = pl.multiple_of(j * tk, tk)
        return lax.dot_general(v_ref[pl.ds(start, tk), :], p_scr[slot][...], tn,
                               preferred_element_type=F32)

    def produce(slot, j, masked):
        s = scores(j, masked)
        s_scr[slot][...] = s
        mx_scr[slot][...] = jnp.max(s, axis=0, keepdims=True)

    def consume(slot, j, m, l):
        tmax = mx_scr[slot][...]
        if kind == "fox":
            m_new = jnp.maximum(m, tmax + cq)
            shift = cq - m_new
        else:
            m_new = jnp.maximum(m, tmax)
            shift = -m_new
        alpha = jnp.exp2(m - m_new)
        shift_b = jnp.broadcast_to(shift, (PACK_ROWS, rq))
        lsum = jnp.zeros((PACK_ROWS, rq), F32)
        for c in range(tk // PACK_ROWS):
            rows = slice(c * PACK_ROWS, (c + 1) * PACK_ROWS)
            pr = jnp.exp2(s_scr[slot][rows, :] + shift_b)
            lsum = lsum + pr
            p_scr[slot][rows, :] = pr.astype(BF16)
        acc_scr[...] = acc_scr[...] * alpha + weighted_values(slot, j)
        return m_new, alpha * l + jnp.sum(lsum, axis=0, keepdims=True)

    def streamed_tile(slot, j, masked):
        m = m_scr[...]
        s = scores(j, masked)
        tmax = jnp.max(s, axis=0, keepdims=True)
        if kind == "fox":
            tmax = tmax + cq
            pr = jnp.exp2(s + (cq - m))
        else:
            pr = jnp.exp2(s - m)
        lsum = jnp.sum(pr, axis=0, keepdims=True)
        p_scr[slot][...] = pr.astype(BF16)
        m_new = jnp.maximum(m, tmax)
        alpha = jnp.exp2(m - m_new)
        acc_scr[...] = (acc_scr[...] + weighted_values(slot, j)) * alpha
        l_scr[...] = (l_scr[...] + lsum) * alpha
        jump_scr[...] = jnp.maximum(jump_scr[...], m_new - m)
        m_scr[...] = m_new

    def run(j0, count, m, l, ends_masked):
        for i in range(count):
            slot = i % 2
            if i + 1 < count:
                produce(1 - slot, j0 + i + 1, ends_masked and i + 2 == count)
            elif not ends_masked:
                produce(1 - slot, j0 + count, False)
            m, l = consume(slot, j0 + i, m, l)
        return m, l

    m0 = jnp.full((1, rq), -jnp.inf, F32)
    l0 = jnp.zeros((1, rq), F32)
    if single_tile:
        produce(0, 0, True)
        l_scr[...] = run(0, 1, m0, l0, True)[1]
        return_early = True
    else:
        return_early = False
        n_full = lax.div(p + q0 + tq + tk - 1, tk) - 1

    def exact_path():
        acc_scr[...] = jnp.zeros_like(acc_scr)
        n_iter = lax.div(jnp.maximum(n_full - 1, 0), FLASH_UNROLL)

        @pl.when(n_full > 0)
        def _():
            produce(0, 0, False)

        m, l = lax.fori_loop(
            0, n_iter, lambda t, c: run(FLASH_UNROLL * t, FLASH_UNROLL, c[0], c[1], False), (m0, l0))
        first = FLASH_UNROLL * n_iter
        rest = n_full - first

        @pl.when(n_full == 0)
        def _():
            produce(0, 0, True)
            l_scr[...] = run(0, 1, m, l, True)[1]

        for r in range(1, FLASH_UNROLL + 1):
            @pl.when(jnp.logical_and(n_full > 0, rest == r))
            def _(r=r):
                l_scr[...] = run(first, r + 1, m, l, True)[1]

    def streamed_path():
        for masked in (True, False):
            @pl.when((n_full == 0) == masked)
            def _(masked=masked):
                produce(0, 0, masked)
                m_scr[...], l_scr[...] = consume(0, 0, m0, l0)
        jump_scr[...] = jnp.zeros_like(jump_scr)
        middle = jnp.maximum(n_full - 1, 0)

        def pair(t, carry):
            streamed_tile(0, 2 * t + 1, False)
            streamed_tile(1, 2 * t + 2, False)
            return carry

        lax.fori_loop(0, lax.div(middle, 2), pair, 0)

        @pl.when((middle & 1) == 1)
        def _():
            streamed_tile(0, n_full - 1, False)

        @pl.when(n_full > 0)
        def _():
            streamed_tile(1, n_full, True)

    if not return_early:
        streamed_path()

        @pl.when(jnp.max(jump_scr[...]) > JUMP_LIMIT)
        def _():
            exact_path()

    o = (acc_scr[...] / l_scr[...]).T
    if kind == "diff":
        lv = lv_ref[...]
        lam = (jnp.exp(jnp.sum(lv[0:1] * lv[1:2], axis=-1, keepdims=True))
               - jnp.exp(jnp.sum(lv[2:3] * lv[3:4], axis=-1, keepdims=True)) + lam_init)
        o = o[:tq] - lam * o[tq:]
        o = o * lax.rsqrt(jnp.mean(o * o, axis=-1, keepdims=True) + EPS) * gn_ref[...]
        o = o * (1.0 - lam_init)
    else:
        o = o[:tq]
    o_ref[...] = o.astype(o_ref.dtype)


def _head_major_kernel(x_ref, o_ref):
    o_ref[...] = pltpu.einshape("thd->htd", x_ref[...]).astype(o_ref.dtype)


def head_major(cache):
    n, b, p, heads, w = cache.shape
    pc = _pick(p, 512)
    return pl.pallas_call(
        _head_major_kernel,
        grid=(n, b, p // pc),
        in_specs=[pl.BlockSpec((None, None, pc, heads, w), lambda j, ib, ip: (j, ib, ip, 0, 0))],
        out_specs=pl.BlockSpec((None, None, heads, pc, w), lambda j, ib, ip: (j, ib, 0, ip, 0)),
        out_shape=jax.ShapeDtypeStruct((n, b, heads, p, w), BF16),
        compiler_params=_cparams("arbitrary", "arbitrary", "arbitrary"),
        name="head_major",
    )(cache)


def flash(kind, q, k_new, v_new, past_k, past_v, layer_slot, tq_pref, tk_pref, extra, lam_init=0.0):
    b, l, width = q.shape
    heads = width // HEAD_W
    p = past_k.shape[3]
    has_past = p > 0
    tq = _pick(l, tq_pref)
    tk = p + l if has_past else _pick(l, tk_pref)
    assert tk % PACK_ROWS == 0 and (tk == p + l or tk % tq == 0)
    qspec = pl.BlockSpec((None, tq, HEAD_W), lambda ib, ih, iq: (ib, iq, ih))
    kvspec = pl.BlockSpec((None, l, HEAD_W), lambda ib, ih, iq: (ib, 0, ih))
    pastspec = pl.BlockSpec((None, None, None, p, HEAD_W), lambda ib, ih, iq: (layer_slot, ib, ih, 0, 0))
    past_specs, past_args = ([pastspec, pastspec], [past_k, past_v]) if has_past else ([], [])
    rq = 2 * tq if kind == "diff" else max(tq, HEAD_W)
    scratch = ([pltpu.VMEM((HEAD_W, rq), BF16)]
               + [pltpu.VMEM((tk, rq), F32)] * 2
               + [pltpu.VMEM((1, rq), F32)] * 2
               + [pltpu.VMEM((tk, rq), BF16)] * 2
               + [pltpu.VMEM((HEAD_W, rq), F32),
                  pltpu.VMEM((1, rq), F32),
                  pltpu.VMEM((1, rq), F32),
                  pltpu.VMEM((1, rq), F32)])
    if kind == "diff":
        lam_vec, gn = extra
        especs = [pl.BlockSpec(lam_vec.shape, lambda ib, ih, iq: (0, 0)),
                  pl.BlockSpec((1, HEAD_W), lambda ib, ih, iq: (0, 0))]
        eargs = [lam_vec, gn.reshape(1, HEAD_W)]
    else:
        cum = extra
        lc = cum.shape[-1]
        assert lc % HEAD_W == 0 and lc >= p + l
        cq = cum[:, :, p:p + l].reshape(b, heads, l // tq, tq)
        if rq > tq:
            cq = jnp.pad(cq, ((0, 0), (0, 0), (0, 0), (0, rq - tq)))
        ck = cum.reshape(b, heads, lc // HEAD_W, HEAD_W)
        especs = [pl.BlockSpec((None, None, l // tq, rq), lambda ib, ih, iq: (ib, ih, 0, 0)),
                  pl.BlockSpec((None, None, lc // HEAD_W, HEAD_W), lambda ib, ih, iq: (ib, ih, 0, 0))]
        eargs = [cq, ck]
        scratch.append(pltpu.VMEM((lc, HEAD_W), F32))
    return pl.pallas_call(
        functools.partial(_flash_kernel, kind=kind, tq=tq, tk=tk, p=p, lam_init=lam_init, rq=rq,
                          single_tile=(tk == p + l), has_past=has_past),
        grid=(b, heads, l // tq),
        in_specs=past_specs + [qspec, kvspec, kvspec] + especs,
        out_specs=qspec,
        out_shape=jax.ShapeDtypeStruct((b, l, width), BF16),
        scratch_shapes=scratch,
        compiler_params=_cparams("arbitrary", "arbitrary", "arbitrary"),
        name="flash_" + kind,
    )(*past_args, q, k_new, v_new, *eargs)


def _ret_kernel(lg_ref, q_ref, k_ref, v_ref, g_ref, cos_ref, sin_ref, s0_ref, gn_ref,
                o_ref, sn_ref, s_scr, *, c, nc):
    t = pl.program_id(2)

    @pl.when(t == 0)
    def _():
        s_scr[...] = s0_ref[...]

    cos, sin = cos_ref[...], sin_ref[...]
    half = RET_DK // 2

    def rot(x):
        x1, x2 = x[:, :half], x[:, half:]
        return jnp.concatenate([x1 * cos - x2 * sin, x2 * cos + x1 * sin], axis=-1)

    q = rot(q_ref[...])
    k = rot(k_ref[...]) * RET_DK ** -0.5
    vb = v_ref[...].astype(BF16)
    lg = lg_ref[pl.program_id(1)]
    dist = (lax.broadcasted_iota(jnp.int32, (c, c), 0)
            - lax.broadcasted_iota(jnp.int32, (c, c), 1)).astype(F32)
    intra = jnp.where(dist >= 0, jnp.exp(lg * jnp.maximum(dist, 0.0)), 0.0)
    ic = lax.broadcasted_iota(jnp.int32, (c, 1), 0).astype(F32)
    q_dec = jnp.exp(lg * (ic + 1.0))
    k_dec = jnp.exp(lg * (c - 1.0 - ic))
    blk_dec = jnp.exp(lg * jnp.full((1, RET_DV), float(c), F32))
    att = lax.dot_general(q.astype(BF16), k.astype(BF16), (((1,), (1,)), ((), ())),
                          preferred_element_type=F32) * intra
    s = s_scr[...]
    o = (jnp.dot(att.astype(BF16), vb, preferred_element_type=F32)
         + jnp.dot((q * q_dec).astype(BF16), s.astype(BF16), preferred_element_type=F32))
    s_new = s * blk_dec + lax.dot_general((k * k_dec).astype(BF16), vb, (((0,), (0,)), ((), ())),
                                          preferred_element_type=F32)
    s_scr[...] = s_new
    mu = jnp.mean(o, axis=-1, keepdims=True)
    oc = o - mu
    var = jnp.mean(oc * oc, axis=-1, keepdims=True)
    y = oc * lax.rsqrt(var + EPS) * gn_ref[...]
    o_ref[...] = (y * _silu(g_ref[...])).astype(o_ref.dtype)

    @pl.when(t == nc - 1)
    def _():
        sn_ref[...] = s_new


def retention(proj, p, s0, ret_gn):
    b, l, _ = proj.shape
    heads = s0.shape[1]
    c = _pick(l, 256)
    nc = l // c
    inv_freq = RET_THETA ** (-jnp.arange(0, RET_DK, 2, dtype=F32) / RET_DK)
    ang = (p + jnp.arange(l)).astype(F32)[:, None] * inv_freq[None, :]
    cos, sin = jnp.cos(ang), jnp.sin(ang)
    lg = jnp.asarray([math.log(1.0 - 2.0 ** (-5 - h)) for h in range(heads)], F32)
    col = lambda off: pl.BlockSpec((None, c, RET_DK), lambda ib, ih, it: (ib, it, off + ih))
    tab = pl.BlockSpec((c, RET_DK // 2), lambda ib, ih, it: (it, 0))
    st = pl.BlockSpec((None, None, RET_DK, RET_DV), lambda ib, ih, it: (ib, ih, 0, 0))
    return pl.pallas_call(
        functools.partial(_ret_kernel, c=c, nc=nc),
        grid=(b, heads, nc),
        in_specs=[pl.BlockSpec(memory_space=pltpu.SMEM),
                  col(0), col(heads), col(2 * heads), col(3 * heads), tab, tab, st,
                  pl.BlockSpec((None, 1, RET_DV), lambda ib, ih, it: (ih, 0, 0))],
        out_specs=(pl.BlockSpec((None, c, RET_DV), lambda ib, ih, it: (ib, it, ih)), st),
        out_shape=(jax.ShapeDtypeStruct((b, l, heads * RET_DV), BF16),
                   jax.ShapeDtypeStruct(s0.shape, F32)),
        scratch_shapes=[pltpu.VMEM((RET_DK, RET_DV), F32)],
        compiler_params=_cparams("arbitrary", "arbitrary", "arbitrary"),
        name="retention",
    )(lg, proj, proj, proj, proj, cos, sin, s0.astype(F32), ret_gn.reshape(heads, 1, RET_DV))


def even_mixer(h, x, mod, pool_prev, past_k, past_v, w_in, w_out, pool_w, pool_scale,
               qn, kn, lam_vec, out_gn, lam_init, layer_slot, n_slots, carry):
    p = past_k.shape[3]
    proj = matmul([h], [w_in], F32)
    pool_out, new_pool, q, k32, kb, v32, vb = prep_even(
        proj, pool_prev, p, qn, kn, pool_w, pool_scale, layer_slot, n_slots, carry)
    o = flash("diff", q, kb, vb, past_k, past_v, layer_slot, 256, 512,
              (lam_vec.astype(F32), out_gn.astype(F32)), lam_init)
    x = matmul([pool_out, o], [w_out[:POOL_DIM], w_out[POOL_DIM:]], F32, "resid", x, mod, 2)
    return x, new_pool, k32, v32


def odd_mixer(h, x, mod, s0, past_k, past_v, past_logf, w_in, w_fl, w_out, ret_gn, qn, kn, f_bias,
              layer_slot, n_slots, carry):
    p = past_k.shape[3]
    b, l, _ = h.shape
    ret_heads = s0.shape[1]
    fox_heads = f_bias.shape[0]
    proj = matmul([h], [w_in], F32)
    fl = matmul([h], [w_fl], F32)
    ret, s_new = retention(proj, p, s0, ret_gn)
    fox_off = 2 * ret_heads * RET_DK + 2 * ret_heads * RET_DV
    q, k32, kb, v32, vb, logf = prep_odd(proj, fl, fox_off, fox_heads, qn, kn, f_bias,
                                         layer_slot, n_slots, carry)
    lc = -(-(p + l) // CUMSUM_BLOCK) * CUMSUM_BLOCK
    lf_all = jnp.concatenate([past_logf.astype(F32), logf,
                              jnp.zeros((b, lc - p - l, fox_heads), F32)], axis=1)
    cum = cumsum_lanes(jnp.swapaxes(lf_all, 1, 2).reshape(b * fox_heads, lc))
    o = flash("fox", q, kb, vb, past_k, past_v, layer_slot, 512, 512, cum.reshape(b, fox_heads, lc))
    x = matmul([ret, o], [w_out[:ret.shape[-1]], w_out[ret.shape[-1]:]], F32, "resid", x, mod, 2)
    return x, s_new, k32, v32, logf


def trunk(x, mods, pool_prev, diff_k, diff_v, ret_s, fox_k, fox_v, fox_logf, wts):
    (norm1, norm2, w_up, w_down, w_in_even, w_out_even, pool_w, pool_scale, diff_qn, diff_kn,
     diff_lam, diff_gn, w_in_odd, w_fl_odd, w_out_odd, ret_gn, fox_qn, fox_kn, fox_fbias) = wts
    depth = norm1.shape[0]
    n_pair = depth // 2
    b, seq, _ = x.shape
    n_pool, n_rs, n_fl = [], [], []
    diff_kv, fox_kv = (), ()
    for l in range(depth):
        j = l // 2
        mod = mods[l]
        h = norm_mod(x, norm1[l], mod, 0, 1)
        if l % 2 == 0:
            x, p_new, *diff_kv = even_mixer(
                h, x, mod, pool_prev[j], diff_k, diff_v, w_in_even[j], w_out_even[j], pool_w[j],
                pool_scale[j], diff_qn[j], diff_kn[j], diff_lam[j], diff_gn[j],
                0.8 - 0.6 * math.exp(-0.3 * l), j, n_pair, tuple(diff_kv))
            n_pool.append(p_new)
        else:
            x, s_new, *fox_kv, lf_new = odd_mixer(
                h, x, mod, ret_s[j], fox_k, fox_v, fox_logf[j], w_in_odd[j], w_fl_odd[j],
                w_out_odd[j], ret_gn[j], fox_qn[j], fox_kn[j], fox_fbias[j], j, n_pair, tuple(fox_kv))
            n_rs.append(s_new)
            n_fl.append(lf_new)
        h = norm_mod(x, norm2[l], mod, 3, 4)
        a = matmul([h], [w_up[l]], BF16, "relu2")
        x = matmul([a], [w_down[l]], F32, "resid", x, mod, 5, bk=_pick(a.shape[-1], 2048))
    st = jnp.stack
    return x, st(n_pool), diff_kv[0], diff_kv[1], st(n_rs), fox_kv[0], fox_kv[1], st(n_fl)


def kernel(x_prompt, x_sample, c_prompt, c_sample, cache_pool, cache_diff_k, cache_diff_v, state_ret,
           cache_fox_k, cache_fox_v, cache_fox_logf, w_ada, b_ada, norm1, norm2, w_up, w_down,
           w_in_even, w_out_even, pool_w, pool_scale, diff_qn, diff_kn, diff_lam, diff_gn,
           w_in_odd, w_out_odd, ret_gn, fox_qn, fox_kn, fox_fbias):
    bp, _, d = x_prompt.shape
    bs = x_sample.shape[0]
    n_pair = cache_pool.shape[0]
    depth = w_ada.shape[0]
    diff_heads = cache_diff_k.shape[3]
    ret_heads = state_ret.shape[2]
    fox_heads = cache_fox_k.shape[3]

    mc = -(-(bp + bs) // 16) * 16
    c_all = jnp.concatenate([c_prompt, c_sample, jnp.zeros((mc - bp - bs, d), F32)], axis=0)
    mod_all = ada_mod(c_all, w_ada, b_ada)
    mods_p = mod_all[:, :bp].reshape(depth, bp, 1, 6 * d)
    mods_s = mod_all[:, bp:bp + bs].reshape(depth, bs, 1, 6 * d)

    fox_main = w_in_odd.shape[-1] - fox_heads
    w_fl = jnp.pad(w_in_odd[:, :, fox_main:], ((0, 0), (0, 0), (0, HEAD_W - fox_heads))).astype(BF16)
    wts = (norm1, norm2, w_up.astype(BF16), w_down.astype(BF16), w_in_even.astype(BF16),
           w_out_even.astype(BF16), pool_w, pool_scale, diff_qn, diff_kn, diff_lam, diff_gn,
           w_in_odd[:, :, :fox_main].astype(BF16), w_fl, w_out_odd.astype(BF16),
           ret_gn, fox_qn, fox_kn, fox_fbias)

    dt = x_prompt.dtype
    y_p, pool_p, dk_p, dv_p, rs_p, fk_p, fv_p, fl_p = trunk(
        x_prompt, mods_p,
        jnp.zeros((n_pair, bp, POOL_MAX - 1, POOL_DIM), dt),
        jnp.zeros((n_pair, bp, diff_heads, 0, HEAD_W), BF16),
        jnp.zeros((n_pair, bp, diff_heads, 0, HEAD_W), BF16),
        jnp.zeros((n_pair, bp, ret_heads, RET_DK, RET_DV), F32),
        jnp.zeros((n_pair, bp, fox_heads, 0, HEAD_W), BF16),
        jnp.zeros((n_pair, bp, fox_heads, 0, HEAD_W), BF16),
        jnp.zeros((n_pair, bp, 0, fox_heads), F32),
        wts)

    def by_head(cache):
        if cache.shape[3] % 8:
            return jnp.swapaxes(cache, 2, 3)
        return head_major(cache)

    y_s, pool_s, dk_s, dv_s, rs_s, fk_s, fv_s, fl_s = trunk(
        x_sample, mods_s, cache_pool, by_head(cache_diff_k), by_head(cache_diff_v), state_ret,
        by_head(cache_fox_k), by_head(cache_fox_v), cache_fox_logf, wts)
    return (y_p, y_s, pool_p, pool_s, dk_p, dk_s, dv_p, dv_s, rs_p, rs_s,
            fk_p, fk_s, fv_p, fv_s, fl_p, fl_s)
```

```python
import functools
import math

import jax
import jax.numpy as jnp
from jax import lax
from jax.experimental import pallas as pl
from jax.experimental.pallas import tpu as pltpu

F32 = jnp.float32
BF16 = jnp.bfloat16
EPS = 1e-6
CHUNK = 64

POOL_WINDOWS = (2, 4, 8, 16)
POOL_MAX = max(POOL_WINDOWS)
POOL_GDIM = 128
POOL_DIM = POOL_GDIM * len(POOL_WINDOWS)

HEAD_W = 128
DIFF_DH = 64
ROPE_DIM = DIFF_DH // 4
ROPE_THETA = 500000.0

RET_DK = 256
RET_DV = 256
RET_THETA = 10000.0

FOX_DH = 128

LOG2E = math.log2(math.e)
PACK_ROWS = 16
FLASH_UNROLL = 2
FILL_UNROLL = 8
NORM_COLS = 512

VMEM_LIMIT_BYTES = 56 * 1024 * 1024


def _cparams(*sem):
    return pltpu.CompilerParams(dimension_semantics=sem, vmem_limit_bytes=VMEM_LIMIT_BYTES)


def _pick(n, pref):
    t = min(pref, n)
    while n % t:
        t //= 2
    return t


def _token_tile(b, l, rows):
    if l >= rows:
        return 1, _pick(l, rows)
    return _pick(b, max(rows // l, 1)), l


def _silu(x):
    return x / (1.0 + jnp.exp(-x))


def _ada_kernel(c_ref, w_ref, b_ref, o_ref):
    c = c_ref[...]
    ca = _silu(c).astype(BF16)
    o_ref[...] = jnp.dot(ca, w_ref[...].astype(BF16), preferred_element_type=F32) + b_ref[...]


def ada_mod(c_all, w_ada, b_ada):
    depth, d, n = w_ada.shape
    mc = c_all.shape[0]
    bn = _pick(n, 1024)
    return pl.pallas_call(
        _ada_kernel,
        grid=(depth, n // bn),
        in_specs=[pl.BlockSpec((mc, d), lambda l, j: (0, 0)),
                  pl.BlockSpec((None, d, bn), lambda l, j: (l, 0, j)),
                  pl.BlockSpec((None, 1, bn), lambda l, j: (l, 0, j))],
        out_specs=pl.BlockSpec((None, mc, bn), lambda l, j: (l, 0, j)),
        out_shape=jax.ShapeDtypeStruct((depth, mc, n), F32),
        compiler_params=_cparams("arbitrary", "arbitrary"),
        name="ada_mod",
    )(c_all, w_ada, b_ada.reshape(depth, 1, n))


def _norm_mod_kernel(x_ref, g_ref, sh_ref, sc_ref, o_ref):
    d = x_ref.shape[-1]
    cw = _pick(d, NORM_COLS)
    cols = [slice(c * cw, (c + 1) * cw) for c in range(d // cw)]
    ss = None
    for sl in cols:
        x = x_ref[:, :, sl]
        part = jnp.sum(x * x, axis=-1, keepdims=True)
        ss = part if ss is None else ss + part
    r = lax.rsqrt(ss * (1.0 / d) + EPS)
    for sl in cols:
        y = x_ref[:, :, sl] * r * g_ref[:, sl]
        o_ref[:, :, sl] = (y * (1.0 + sc_ref[:, :, sl]) + sh_ref[:, :, sl]).astype(o_ref.dtype)


def norm_mod(x, g, mod, shift_idx, scale_idx):
    b, l, d = x.shape
    bb, bl = _token_tile(b, l, 512)
    return pl.pallas_call(
        _norm_mod_kernel,
        grid=(b // bb, l // bl),
        in_specs=[pl.BlockSpec((bb, bl, d), lambda ib, il: (ib, il, 0)),
                  pl.BlockSpec((1, d), lambda ib, il: (0, 0)),
                  pl.BlockSpec((bb, 1, d), lambda ib, il: (ib, 0, shift_idx)),
                  pl.BlockSpec((bb, 1, d), lambda ib, il: (ib, 0, scale_idx))],
        out_specs=pl.BlockSpec((bb, bl, d), lambda ib, il: (ib, il, 0)),
        out_shape=jax.ShapeDtypeStruct((b, l, d), BF16),
        compiler_params=_cparams("arbitrary", "arbitrary"),
        name="norm_mod",
    )(x, g.reshape(1, d), mod, mod)


def _mm_kernel(*refs, n_lhs, nk, epilogue):
    lhs = refs[:n_lhs]
    ws = refs[n_lhs:2 * n_lhs]
    pos = 2 * n_lhs
    if epilogue == "resid":
        xres_ref, gate_ref = refs[pos], refs[pos + 1]
        pos += 2
    o_ref = refs[pos]

    def compute():
        acc = None
        for a_ref, w_ref in zip(lhs, ws):
            a = a_ref[...]
            a = a.reshape(-1, a.shape[-1])
            part = jnp.dot(a, w_ref[...], preferred_element_type=F32)
            acc = part if acc is None else acc + part
        return acc

    def finish(acc):
        if epilogue == "relu2":
            r = jnp.maximum(acc, 0.0)
            y = r * r
        elif epilogue == "resid":
            y = xres_ref[...] + gate_ref[...] * acc.reshape(o_ref.shape)
        else:
            y = acc
        o_ref[...] = y.reshape(o_ref.shape).astype(o_ref.dtype)

    if nk == 1:
        finish(compute())
    else:
        acc_ref = refs[pos + 1]
        k = pl.program_id(3)

        @pl.when(k == 0)
        def _():
            acc_ref[...] = compute()

        @pl.when(k > 0)
        def _():
            acc_ref[...] += compute()

        @pl.when(k == nk - 1)
        def _():
            finish(acc_ref[...])


def matmul(lhs_list, w_list, out_dtype, epilogue="none", xres=None, mod=None, gate_idx=0,
           rows=1024, cols=1024, bk=None):
    b, l, _ = lhs_list[0].shape
    n = w_list[0].shape[1]
    bb, bl = _token_tile(b, l, rows)
    bn = _pick(n, cols)
    ks = [a.shape[-1] for a in lhs_list]
    nk = 1 if bk is None else ks[0] // bk
    assert nk == 1 or len(lhs_list) == 1
    bks = ks if nk == 1 else [bk]
    in_specs = [pl.BlockSpec((bb, bl, kk), lambda ib, il, j, k: (ib, il, k)) for kk in bks]
    in_specs += [pl.BlockSpec((kk, bn), lambda ib, il, j, k: (k, j)) for kk in bks]
    args = list(lhs_list) + list(w_list)
    if epilogue == "resid":
        gate_off = gate_idx * (n // bn)
        in_specs += [pl.BlockSpec((bb, bl, bn), lambda ib, il, j, k: (ib, il, j)),
                     pl.BlockSpec((bb, 1, bn), lambda ib, il, j, k: (ib, 0, gate_off + j))]
        args += [xres, mod]
    scratch = [pltpu.VMEM((bb * bl, bn), F32)] if nk > 1 else []
    return pl.pallas_call(
        functools.partial(_mm_kernel, n_lhs=len(lhs_list), nk=nk, epilogue=epilogue),
        grid=(b // bb, l // bl, n // bn, nk),
        in_specs=in_specs,
        out_specs=pl.BlockSpec((bb, bl, bn), lambda ib, il, j, k: (ib, il, j)),
        out_shape=jax.ShapeDtypeStruct((b, l, n), out_dtype),
        scratch_shapes=scratch,
        compiler_params=_cparams("arbitrary", "arbitrary", "arbitrary", "arbitrary"),
        name="matmul_" + epilogue,
    )(*args)


def _prep_even_kernel(*refs, t, p, nl, heads, n_carry):
    proj_ref, prev_ref, c_ref, s1_ref, s2_ref, qg_ref, kg_ref, pw_ref, ps_ref = refs[:9]
    pool_ref, npool_ref, q_ref, k32_ref, kb_ref, v32_ref, vb_ref, full_ref = refs[9 + n_carry:]
    il = pl.program_id(1)

    @pl.when(il == 0)
    def _():
        full_ref[0:POOL_MAX, :] = prev_ref[...]

    @pl.when(il > 0)
    def _():
        full_ref[0:POOL_MAX, :] = full_ref[t:t + POOL_MAX, :]

    full_ref[POOL_MAX:POOL_MAX + t, :] = proj_ref[:, 0:POOL_DIM]
    pos1 = p + il * t + 1 + lax.broadcasted_iota(jnp.int32, (t, 1), 0)
    for g, w in enumerate(POOL_WINDOWS):
        sl = slice(g * POOL_GDIM, (g + 1) * POOL_GDIM)
        u = full_ref[POOL_MAX:POOL_MAX + t, sl]
        win = u
        for s in range(1, w):
            win = win + full_ref[POOL_MAX - s:POOL_MAX - s + t, sl]
        cnt = jnp.minimum(pos1, w).astype(F32)
        d = (win / cnt - u).astype(BF16)
        mixed = jnp.dot(d, pw_ref[g], preferred_element_type=F32) * ps_ref[:, sl]
        pool_ref[:, sl] = mixed.astype(pool_ref.dtype)

    @pl.when(il == nl - 1)
    def _():
        npool_ref[...] = full_ref[t + 1:t + POOL_MAX, :]

    seg = (lax.broadcasted_iota(jnp.int32, (HEAD_W, HEAD_W), 0) >> 6) == \
          (lax.broadcasted_iota(jnp.int32, (HEAD_W, HEAD_W), 1) >> 6)
    seg = seg.astype(F32).astype(BF16)
    rc, rs1, rs2 = c_ref[...], s1_ref[...], s2_ref[...]

    def norm_rope(x, g):
        x2 = x * x
        hi = x2.astype(BF16)
        lo = (x2 - hi.astype(F32)).astype(BF16)
        ss = (jnp.dot(lo, seg, preferred_element_type=F32)
              + jnp.dot(hi, seg, preferred_element_type=F32))
        y = x * lax.rsqrt(ss * (1.0 / DIFF_DH) + EPS) * g
        return y * rc + pltpu.roll(y, 8, 1) * rs2 + pltpu.roll(y, HEAD_W - 8, 1) * rs1

    width = heads * HEAD_W
    q_off, k_off, v_off = POOL_DIM, POOL_DIM + width, POOL_DIM + 2 * width
    qg, kg = qg_ref[...], kg_ref[...]
    ks, vs = [], []
    for h in range(heads):
        sl = slice(h * HEAD_W, (h + 1) * HEAD_W)
        rq = norm_rope(proj_ref[:, q_off + h * HEAD_W:q_off + (h + 1) * HEAD_W], qg)
        q_ref[:, sl] = (rq * (DIFF_DH ** -0.5 * LOG2E)).astype(BF16)
        rk = norm_rope(proj_ref[:, k_off + h * HEAD_W:k_off + (h + 1) * HEAD_W], kg)
        kb_ref[:, sl] = rk.astype(BF16)
        ks.append(rk)
        vs.append(proj_ref[:, v_off + h * HEAD_W:v_off + (h + 1) * HEAD_W])
    k32_ref[...] = pltpu.einshape("htd->thd", jnp.stack(ks, axis=0))
    v32_ref[...] = pltpu.einshape("htd->thd", jnp.stack(vs, axis=0))
    vb_ref[...] = proj_ref[:, v_off:v_off + width].astype(BF16)


def _diff_rope_tables(p, l):
    inv_freq = ROPE_THETA ** (-jnp.arange(0, ROPE_DIM, 2, dtype=F32) / ROPE_DIM)
    ang = (p + jnp.arange(l)).astype(F32)[:, None] * inv_freq[None, :]
    cos, sin = jnp.cos(ang), jnp.sin(ang)
    half = ROPE_DIM // 2
    rest = DIFF_DH - ROPE_DIM
    c = jnp.concatenate([cos, cos, jnp.ones((l, rest), F32)], axis=-1)
    s1 = jnp.concatenate([-sin, jnp.zeros((l, half + rest), F32)], axis=-1)
    s2 = jnp.concatenate([jnp.zeros((l, half), F32), sin, jnp.zeros((l, rest), F32)], axis=-1)
    return tuple(jnp.concatenate([a, a], axis=-1) for a in (c, s1, s2))


def prep_even(proj, pool_prev, p, qn, kn, pool_w, pool_scale, layer_slot, n_slots, carry):
    b, l, n_in = proj.shape
    width = (n_in - POOL_DIM) // 3
    heads = width // HEAD_W
    t = _pick(l, 256)
    nl = l // t
    prev16 = jnp.concatenate([jnp.zeros((b, 1, POOL_DIM), F32), pool_prev.astype(F32)], axis=1)
    rc, rs1, rs2 = _diff_rope_tables(p, l)
    qg = jnp.concatenate([qn, qn]).reshape(1, HEAD_W).astype(F32)
    kg = jnp.concatenate([kn, kn]).reshape(1, HEAD_W).astype(F32)
    tok = lambda w: pl.BlockSpec((None, t, w), lambda ib, il: (ib, il, 0))
    tab = pl.BlockSpec((t, HEAD_W), lambda ib, il: (il, 0))
    vec = pl.BlockSpec((1, HEAD_W), lambda ib, il: (0, 0))
    out_shapes = (
        jax.ShapeDtypeStruct((b, l, POOL_DIM), BF16),
        jax.ShapeDtypeStruct((b, POOL_MAX - 1, POOL_DIM), F32),
        jax.ShapeDtypeStruct((b, l, width), BF16),
        jax.ShapeDtypeStruct((n_slots, b, l, heads, HEAD_W), F32),
        jax.ShapeDtypeStruct((b, l, width), BF16),
        jax.ShapeDtypeStruct((n_slots, b, l, heads, HEAD_W), F32),
        jax.ShapeDtypeStruct((b, l, width), BF16),
    )
    stacked = pl.BlockSpec((None, None, t, heads, HEAD_W), lambda ib, il: (layer_slot, ib, il, 0, 0))
    return pl.pallas_call(
        functools.partial(_prep_even_kernel, t=t, p=p, nl=nl, heads=heads, n_carry=len(carry)),
        grid=(b, nl),
        in_specs=[tok(n_in),
                  pl.BlockSpec((None, POOL_MAX, POOL_DIM), lambda ib, il: (ib, 0, 0)),
                  tab, tab, tab, vec, vec,
                  pl.BlockSpec((len(POOL_WINDOWS), POOL_GDIM, POOL_GDIM), lambda ib, il: (0, 0, 0)),
                  pl.BlockSpec((1, POOL_DIM), lambda ib, il: (0, 0))]
                 + [pl.BlockSpec(memory_space=pl.ANY)] * len(carry),
        out_specs=(tok(POOL_DIM),
                   pl.BlockSpec((None, POOL_MAX - 1, POOL_DIM), lambda ib, il: (ib, 0, 0)),
                   tok(width), stacked, tok(width), stacked, tok(width)),
        out_shape=out_shapes,
        scratch_shapes=[pltpu.VMEM((POOL_MAX + t, POOL_DIM), F32)],
        input_output_aliases={9 + i: o for i, o in zip(range(len(carry)), (3, 5))},
        compiler_params=_cparams("arbitrary", "arbitrary"),
        name="prep_even",
    )(proj, prev16, rc, rs1, rs2, qg, kg, pool_w.astype(BF16), pool_scale.reshape(1, POOL_DIM), *carry)


def _prep_odd_kernel(*refs, heads, n_carry):
    fq_ref, fk_ref, fv_ref, fl_ref, qg_ref, kg_ref, fb_ref = refs[:7]
    q_ref, k32_ref, kb_ref, v32_ref, vb_ref, lf_ref = refs[7 + n_carry:]
    def rms(x, g):
        return x * lax.rsqrt(jnp.mean(x * x, axis=-1, keepdims=True) + EPS) * g

    qg, kg = qg_ref[...], kg_ref[...]
    ks, vs = [], []
    for h in range(heads):
        sl = slice(h * HEAD_W, (h + 1) * HEAD_W)
        q_ref[:, sl] = (rms(fq_ref[:, sl], qg) * (FOX_DH ** -0.5 * LOG2E)).astype(BF16)
        rk = rms(fk_ref[:, sl], kg)
        kb_ref[:, sl] = rk.astype(BF16)
        ks.append(rk)
        vs.append(fv_ref[:, sl])
    k32_ref[...] = pltpu.einshape("htd->thd", jnp.stack(ks, axis=0))
    v32_ref[...] = pltpu.einshape("htd->thd", jnp.stack(vs, axis=0))
    vb_ref[...] = fv_ref[...].astype(BF16)
    x = fl_ref[...] + fb_ref[...]
    logf = -(jnp.maximum(-x, 0.0) + jnp.log1p(jnp.exp(-jnp.abs(x))))
    lf_ref[...] = logf[:, 0:heads]


def prep_odd(proj, fl, fox_off, heads, qn, kn, f_bias, layer_slot, n_slots, carry):
    b, l, _ = proj.shape
    width = heads * HEAD_W
    t = _pick(l, 512)
    cb = fox_off // width
    tok = lambda w, c: pl.BlockSpec((None, t, w), lambda ib, il: (ib, il, c))
    vec = pl.BlockSpec((1, HEAD_W), lambda ib, il: (0, 0))
    fb = jnp.zeros((1, HEAD_W), F32).at[0, :heads].set(f_bias.astype(F32))
    out_shapes = (
        jax.ShapeDtypeStruct((b, l, width), BF16),
        jax.ShapeDtypeStruct((n_slots, b, l, heads, HEAD_W), F32),
        jax.ShapeDtypeStruct((b, l, width), BF16),
        jax.ShapeDtypeStruct((n_slots, b, l, heads, HEAD_W), F32),
        jax.ShapeDtypeStruct((b, l, width), BF16),
        jax.ShapeDtypeStruct((b, l, heads), F32),
    )
    stacked = pl.BlockSpec((None, None, t, heads, HEAD_W), lambda ib, il: (layer_slot, ib, il, 0, 0))
    return pl.pallas_call(
        functools.partial(_prep_odd_kernel, heads=heads, n_carry=len(carry)),
        grid=(b, l // t),
        in_specs=[tok(width, cb), tok(width, cb + 1), tok(width, cb + 2), tok(HEAD_W, 0),
                  vec, vec, vec] + [pl.BlockSpec(memory_space=pl.ANY)] * len(carry),
        out_specs=(tok(width, 0), stacked, tok(width, 0), stacked, tok(width, 0),
                   tok(heads, 0)),
        out_shape=out_shapes,
        input_output_aliases={7 + i: o for i, o in zip(range(len(carry)), (1, 3))},
        compiler_params=_cparams("arbitrary", "arbitrary"),
        name="prep_odd",
    )(proj, proj, proj, fl, qn.reshape(1, HEAD_W), kn.reshape(1, HEAD_W), fb, *carry)


CUMSUM_BLOCK = 256


def _cumsum_kernel(x_ref, o_ref, carry_ref):
    @pl.when(pl.program_id(0) == 0)
    def _():
        carry_ref[...] = jnp.zeros_like(carry_ref)

    x = x_ref[...]
    x1 = x.astype(BF16)
    r1 = x - x1.astype(F32)
    x2 = r1.astype(BF16)
    x3 = (r1 - x2.astype(F32)).astype(BF16)
    n = x.shape[-1]
    tri = lax.broadcasted_iota(jnp.int32, (n, n), 0) <= lax.broadcasted_iota(jnp.int32, (n, n), 1)
    tri = tri.astype(F32).astype(BF16)
    c = (jnp.dot(x3, tri, preferred_element_type=F32)
         + jnp.dot(x2, tri, preferred_element_type=F32)
         + jnp.dot(x1, tri, preferred_element_type=F32)) + carry_ref[...]
    o_ref[...] = c
    carry_ref[...] = c[:, n - 1:n]


def cumsum_lanes(x):
    r, n = x.shape
    return pl.pallas_call(
        _cumsum_kernel,
        grid=(n // CUMSUM_BLOCK,),
        in_specs=[pl.BlockSpec((r, CUMSUM_BLOCK), lambda i: (0, i))],
        out_specs=pl.BlockSpec((r, CUMSUM_BLOCK), lambda i: (0, i)),
        out_shape=jax.ShapeDtypeStruct((r, n), F32),
        scratch_shapes=[pltpu.VMEM((r, 1), F32)],
        compiler_params=_cparams("arbitrary"),
        name="cumsum",
    )(x)


def _flash_kernel(*refs, kind, tq, tk, p, lam_init, rq, single_tile, has_past):
    if has_past:
        pk_ref, pv_ref, *refs = refs
    if kind == "diff":
        (q_ref, k_ref, v_ref, lv_ref, gn_ref, o_ref,
         qzt_scr, s0_scr, s1_scr, mx0_scr, mx1_scr, p0_scr, p1_scr, acc_scr, l_scr) = refs
    else:
        (q_ref, k_ref, v_ref, cq_ref, ck_ref, o_ref,
         qzt_scr, s0_scr, s1_scr, mx0_scr, mx1_scr, p0_scr, p1_scr, acc_scr, l_scr,
         ckcol_ref) = refs
    s_scr, mx_scr, p_scr = (s0_scr, s1_scr), (mx0_scr, mx1_scr), (p0_scr, p1_scr)
    iq = pl.program_id(2)
    q0 = iq * tq
    qf = q_ref[...].astype(F32)
    if kind == "diff":
        lane = lax.broadcasted_iota(jnp.int32, (tq, HEAD_W), 1)
        qz = jnp.concatenate([jnp.where(lane < DIFF_DH, qf, 0.0),
                              jnp.where(lane >= DIFF_DH, qf, 0.0)], axis=0)
    else:
        qz = qf if rq == tq else jnp.concatenate([qf, jnp.zeros((rq - tq, HEAD_W), F32)], axis=0)
        cq = cq_ref[pl.ds(iq, 1), :] * LOG2E

        @pl.when(iq == 0)
        def _():
            def fill(c, carry):
                row = jnp.broadcast_to(ck_ref[pl.ds(c, 1), :] * LOG2E, (HEAD_W, HEAD_W))
                ckcol_ref[pl.ds(pl.multiple_of(c * HEAD_W, HEAD_W), HEAD_W), :] = row.T
                return carry
            n_chunks = ck_ref.shape[0]
            lax.fori_loop(0, n_chunks, fill, 0,
                          unroll=FILL_UNROLL if n_chunks % FILL_UNROLL == 0 else True)

    qzt_scr[...] = qz.T.astype(BF16)
    acc_scr[...] = jnp.zeros_like(acc_scr)

    def produce(slot, j, masked):
        start = pl.multiple_of(j * tk, tk)
        if has_past:
            s = jnp.concatenate(
                [jnp.dot(pk_ref[...].astype(BF16), qzt_scr[...], preferred_element_type=F32),
                 jnp.dot(k_ref[...], qzt_scr[...], preferred_element_type=F32)], axis=0)
        else:
            s = jnp.dot(k_ref[pl.ds(start, tk), :], qzt_scr[...], preferred_element_type=F32)
        if kind == "fox":
            ckc = ckcol_ref[pl.ds(start, tk), :]
            s = jnp.concatenate([s[:, c * HEAD_W:(c + 1) * HEAD_W] - ckc
                                 for c in range(rq // HEAD_W)], axis=1)
        if masked:
            kpos = start + lax.broadcasted_iota(jnp.int32, (tk, rq), 0)
            r = lax.broadcasted_iota(jnp.int32, (tk, rq), 1)
            if kind == "diff":
                qpos = p + q0 + jnp.where(r >= tq, r - tq, r)
                ok = (kpos >> 6) <= (qpos >> 6)
            else:
                ok = kpos <= p + q0 + r
            s = jnp.where(ok, s, -jnp.inf)
        s_scr[slot][...] = s
        mx_scr[slot][...] = jnp.max(s, axis=0, keepdims=True)

    def consume(slot, j, m, l):
        tmax = mx_scr[slot][...]
        if kind == "fox":
            m_new = jnp.maximum(m, tmax + cq)
            shift = cq - m_new
        else:
            m_new = jnp.maximum(m, tmax)
            shift = -m_new
        alpha = jnp.exp2(m - m_new)
        shift_b = jnp.broadcast_to(shift, (PACK_ROWS, rq))
        lsum = jnp.zeros((PACK_ROWS, rq), F32)
        for c in range(tk // PACK_ROWS):
            rows = slice(c * PACK_ROWS, (c + 1) * PACK_ROWS)
            pr = jnp.exp2(s_scr[slot][rows, :] + shift_b)
            lsum = lsum + pr
            p_scr[slot][rows, :] = pr.astype(BF16)
        start = pl.multiple_of(j * tk, tk)
        tn = (((0,), (0,)), ((), ()))
        if has_past:
            pv = (lax.dot_general(pv_ref[...].astype(BF16), p_scr[slot][0:p, :], tn,
                                  preferred_element_type=F32)
                  + lax.dot_general(v_ref[...], p_scr[slot][p:tk, :], tn,
                                    preferred_element_type=F32))
        else:
            pv = lax.dot_general(v_ref[pl.ds(start, tk), :], p_scr[slot][...], tn,
                                 preferred_element_type=F32)
        acc_scr[...] = acc_scr[...] * alpha + pv
        return m_new, alpha * l + jnp.sum(lsum, axis=0, keepdims=True)

    def run(j0, count, m, l, ends_masked):
        for i in range(count):
            slot = i % 2
            if i + 1 < count:
                produce(1 - slot, j0 + i + 1, ends_masked and i + 2 == count)
            elif not ends_masked:
                produce(1 - slot, j0 + count, False)
            m, l = consume(slot, j0 + i, m, l)
        return m, l

    m0 = jnp.full((1, rq), -jnp.inf, F32)
    l0 = jnp.zeros((1, rq), F32)
    if single_tile:
        produce(0, 0, True)
        l_scr[...] = run(0, 1, m0, l0, True)[1]
    else:
        n_full = lax.div(p + q0 + tq + tk - 1, tk) - 1
        n_iter = lax.div(jnp.maximum(n_full - 1, 0), FLASH_UNROLL)

        @pl.when(n_full > 0)
        def _():
            produce(0, 0, False)

        m, l = lax.fori_loop(
            0, n_iter, lambda t, c: run(FLASH_UNROLL * t, FLASH_UNROLL, c[0], c[1], False), (m0, l0))
        first = FLASH_UNROLL * n_iter
        rest = n_full - first

        @pl.when(n_full == 0)
        def _():
            produce(0, 0, True)
            l_scr[...] = run(0, 1, m, l, True)[1]

        for r in range(1, FLASH_UNROLL + 1):
            @pl.when(jnp.logical_and(n_full > 0, rest == r))
            def _(r=r):
                l_scr[...] = run(first, r + 1, m, l, True)[1]

    o = (acc_scr[...] / l_scr[...]).T
    if kind == "diff":
        lv = lv_ref[...]
        lam = (jnp.exp(jnp.sum(lv[0:1] * lv[1:2], axis=-1, keepdims=True))
               - jnp.exp(jnp.sum(lv[2:3] * lv[3:4], axis=-1, keepdims=True)) + lam_init)
        o = o[:tq] - lam * o[tq:]
        o = o * lax.rsqrt(jnp.mean(o * o, axis=-1, keepdims=True) + EPS) * gn_ref[...]
        o = o * (1.0 - lam_init)
    else:
        o = o[:tq]
    o_ref[...] = o.astype(o_ref.dtype)


def _head_major_kernel(x_ref, o_ref):
    o_ref[...] = pltpu.einshape("thd->htd", x_ref[...]).astype(o_ref.dtype)


def head_major(cache):
    n, b, p, heads, w = cache.shape
    pc = _pick(p, 512)
    return pl.pallas_call(
        _head_major_kernel,
        grid=(n, b, p // pc),
        in_specs=[pl.BlockSpec((None, None, pc, heads, w), lambda j, ib, ip: (j, ib, ip, 0, 0))],
        out_specs=pl.BlockSpec((None, None, heads, pc, w), lambda j, ib, ip: (j, ib, 0, ip, 0)),
        out_shape=jax.ShapeDtypeStruct((n, b, heads, p, w), BF16),
        compiler_params=_cparams("arbitrary", "arbitrary", "arbitrary"),
        name="head_major",
    )(cache)


def flash(kind, q, k_new, v_new, past_k, past_v, layer_slot, tq_pref, tk_pref, extra, lam_init=0.0):
    b, l, width = q.shape
    heads = width // HEAD_W
    p = past_k.shape[3]
    has_past = p > 0
    tq = _pick(l, tq_pref)
    tk = p + l if has_past else _pick(l, tk_pref)
    assert tk % PACK_ROWS == 0 and (tk == p + l or tk % tq == 0)
    qspec = pl.BlockSpec((None, tq, HEAD_W), lambda ib, ih, iq: (ib, iq, ih))
    kvspec = pl.BlockSpec((None, l, HEAD_W), lambda ib, ih, iq: (ib, 0, ih))
    pastspec = pl.BlockSpec((None, None, None, p, HEAD_W), lambda ib, ih, iq: (layer_slot, ib, ih, 0, 0))
    past_specs, past_args = ([pastspec, pastspec], [past_k, past_v]) if has_past else ([], [])
    rq = 2 * tq if kind == "diff" else max(tq, HEAD_W)
    scratch = ([pltpu.VMEM((HEAD_W, rq), BF16)]
               + [pltpu.VMEM((tk, rq), F32)] * 2
               + [pltpu.VMEM((1, rq), F32)] * 2
               + [pltpu.VMEM((tk, rq), BF16)] * 2
               + [pltpu.VMEM((HEAD_W, rq), F32),
                  pltpu.VMEM((1, rq), F32)])
    if kind == "diff":
        lam_vec, gn = extra
        especs = [pl.BlockSpec(lam_vec.shape, lambda ib, ih, iq: (0, 0)),
                  pl.BlockSpec((1, HEAD_W), lambda ib, ih, iq: (0, 0))]
        eargs = [lam_vec, gn.reshape(1, HEAD_W)]
    else:
        cum = extra
        lc = cum.shape[-1]
        assert lc % HEAD_W == 0 and lc >= p + l
        cq = cum[:, :, p:p + l].reshape(b, heads, l // tq, tq)
        if rq > tq:
            cq = jnp.pad(cq, ((0, 0), (0, 0), (0, 0), (0, rq - tq)))
        ck = cum.reshape(b, heads, lc // HEAD_W, HEAD_W)
        especs = [pl.BlockSpec((None, None, l // tq, rq), lambda ib, ih, iq: (ib, ih, 0, 0)),
                  pl.BlockSpec((None, None, lc // HEAD_W, HEAD_W), lambda ib, ih, iq: (ib, ih, 0, 0))]
        eargs = [cq, ck]
        scratch.append(pltpu.VMEM((lc, HEAD_W), F32))
    return pl.pallas_call(
        functools.partial(_flash_kernel, kind=kind, tq=tq, tk=tk, p=p, lam_init=lam_init, rq=rq,
                          single_tile=(tk == p + l), has_past=has_past),
        grid=(b, heads, l // tq),
        in_specs=past_specs + [qspec, kvspec, kvspec] + especs,
        out_specs=qspec,
        out_shape=jax.ShapeDtypeStruct((b, l, width), BF16),
        scratch_shapes=scratch,
        compiler_params=_cparams("arbitrary", "arbitrary", "arbitrary"),
        name="flash_" + kind,
    )(*past_args, q, k_new, v_new, *eargs)


def _ret_kernel(lg_ref, q_ref, k_ref, v_ref, g_ref, cos_ref, sin_ref, s0_ref, gn_ref,
                o_ref, sn_ref, s_scr, *, c, nc):
    t = pl.program_id(2)

    @pl.when(t == 0)
    def _():
        s_scr[...] = s0_ref[...]

    cos, sin = cos_ref[...], sin_ref[...]
    half = RET_DK // 2

    def rot(x):
        x1, x2 = x[:, :half], x[:, half:]
        return jnp.concatenate([x1 * cos - x2 * sin, x2 * cos + x1 * sin], axis=-1)

    q = rot(q_ref[...])
    k = rot(k_ref[...]) * RET_DK ** -0.5
    vb = v_ref[...].astype(BF16)
    lg = lg_ref[pl.program_id(1)]
    dist = (lax.broadcasted_iota(jnp.int32, (c, c), 0)
            - lax.broadcasted_iota(jnp.int32, (c, c), 1)).astype(F32)
    intra = jnp.where(dist >= 0, jnp.exp(lg * jnp.maximum(dist, 0.0)), 0.0)
    ic = lax.broadcasted_iota(jnp.int32, (c, 1), 0).astype(F32)
    q_dec = jnp.exp(lg * (ic + 1.0))
    k_dec = jnp.exp(lg * (c - 1.0 - ic))
    blk_dec = jnp.exp(lg * jnp.full((1, RET_DV), float(c), F32))
    att = lax.dot_general(q.astype(BF16), k.astype(BF16), (((1,), (1,)), ((), ())),
                          preferred_element_type=F32) * intra
    s = s_scr[...]
    o = (jnp.dot(att.astype(BF16), vb, preferred_element_type=F32)
         + jnp.dot((q * q_dec).astype(BF16), s.astype(BF16), preferred_element_type=F32))
    s_new = s * blk_dec + lax.dot_general((k * k_dec).astype(BF16), vb, (((0,), (0,)), ((), ())),
                                          preferred_element_type=F32)
    s_scr[...] = s_new
    mu = jnp.mean(o, axis=-1, keepdims=True)
    oc = o - mu
    var = jnp.mean(oc * oc, axis=-1, keepdims=True)
    y = oc * lax.rsqrt(var + EPS) * gn_ref[...]
    o_ref[...] = (y * _silu(g_ref[...])).astype(o_ref.dtype)

    @pl.when(t == nc - 1)
    def _():
        sn_ref[...] = s_new


def retention(proj, p, s0, ret_gn):
    b, l, _ = proj.shape
    heads = s0.shape[1]
    c = _pick(l, 256)
    nc = l // c
    inv_freq = RET_THETA ** (-jnp.arange(0, RET_DK, 2, dtype=F32) / RET_DK)
    ang = (p + jnp.arange(l)).astype(F32)[:, None] * inv_freq[None, :]
    cos, sin = jnp.cos(ang), jnp.sin(ang)
    lg = jnp.asarray([math.log(1.0 - 2.0 ** (-5 - h)) for h in range(heads)], F32)
    col = lambda off: pl.BlockSpec((None, c, RET_DK), lambda ib, ih, it: (ib, it, off + ih))
    tab = pl.BlockSpec((c, RET_DK // 2), lambda ib, ih, it: (it, 0))
    st = pl.BlockSpec((None, None, RET_DK, RET_DV), lambda ib, ih, it: (ib, ih, 0, 0))
    return pl.pallas_call(
        functools.partial(_ret_kernel, c=c, nc=nc),
        grid=(b, heads, nc),
        in_specs=[pl.BlockSpec(memory_space=pltpu.SMEM),
                  col(0), col(heads), col(2 * heads), col(3 * heads), tab, tab, st,
                  pl.BlockSpec((None, 1, RET_DV), lambda ib, ih, it: (ih, 0, 0))],
        out_specs=(pl.BlockSpec((None, c, RET_DV), lambda ib, ih, it: (ib, it, ih)), st),
        out_shape=(jax.ShapeDtypeStruct((b, l, heads * RET_DV), BF16),
                   jax.ShapeDtypeStruct(s0.shape, F32)),
        scratch_shapes=[pltpu.VMEM((RET_DK, RET_DV), F32)],
        compiler_params=_cparams("arbitrary", "arbitrary", "arbitrary"),
        name="retention",
    )(lg, proj, proj, proj, proj, cos, sin, s0.astype(F32), ret_gn.reshape(heads, 1, RET_DV))


def even_mixer(h, x, mod, pool_prev, past_k, past_v, w_in, w_out, pool_w, pool_scale,
               qn, kn, lam_vec, out_gn, lam_init, layer_slot, n_slots, carry):
    p = past_k.shape[3]
    proj = matmul([h], [w_in], F32)
    pool_out, new_pool, q, k32, kb, v32, vb = prep_even(
        proj, pool_prev, p, qn, kn, pool_w, pool_scale, layer_slot, n_slots, carry)
    o = flash("diff", q, kb, vb, past_k, past_v, layer_slot, 256, 512,
              (lam_vec.astype(F32), out_gn.astype(F32)), lam_init)
    x = matmul([pool_out, o], [w_out[:POOL_DIM], w_out[POOL_DIM:]], F32, "resid", x, mod, 2)
    return x, new_pool, k32, v32


def odd_mixer(h, x, mod, s0, past_k, past_v, past_logf, w_in, w_fl, w_out, ret_gn, qn, kn, f_bias,
              layer_slot, n_slots, carry):
    p = past_k.shape[3]
    b, l, _ = h.shape
    ret_heads = s0.shape[1]
    fox_heads = f_bias.shape[0]
    proj = matmul([h], [w_in], F32)
    fl = matmul([h], [w_fl], F32)
    ret, s_new = retention(proj, p, s0, ret_gn)
    fox_off = 2 * ret_heads * RET_DK + 2 * ret_heads * RET_DV
    q, k32, kb, v32, vb, logf = prep_odd(proj, fl, fox_off, fox_heads, qn, kn, f_bias,
                                         layer_slot, n_slots, carry)
    lc = -(-(p + l) // CUMSUM_BLOCK) * CUMSUM_BLOCK
    lf_all = jnp.concatenate([past_logf.astype(F32), logf,
                              jnp.zeros((b, lc - p - l, fox_heads), F32)], axis=1)
    cum = cumsum_lanes(jnp.swapaxes(lf_all, 1, 2).reshape(b * fox_heads, lc))
    o = flash("fox", q, kb, vb, past_k, past_v, layer_slot, 512, 512, cum.reshape(b, fox_heads, lc))
    x = matmul([ret, o], [w_out[:ret.shape[-1]], w_out[ret.shape[-1]:]], F32, "resid", x, mod, 2)
    return x, s_new, k32, v32, logf


def trunk(x, mods, pool_prev, diff_k, diff_v, ret_s, fox_k, fox_v, fox_logf, wts):
    (norm1, norm2, w_up, w_down, w_in_even, w_out_even, pool_w, pool_scale, diff_qn, diff_kn,
     diff_lam, diff_gn, w_in_odd, w_fl_odd, w_out_odd, ret_gn, fox_qn, fox_kn, fox_fbias) = wts
    depth = norm1.shape[0]
    n_pair = depth // 2
    b, seq, _ = x.shape
    n_pool, n_rs, n_fl = [], [], []
    diff_kv, fox_kv = (), ()
    for l in range(depth):
        j = l // 2
        mod = mods[l]
        h = norm_mod(x, norm1[l], mod, 0, 1)
        if l % 2 == 0:
            x, p_new, *diff_kv = even_mixer(
                h, x, mod, pool_prev[j], diff_k, diff_v, w_in_even[j], w_out_even[j], pool_w[j],
                pool_scale[j], diff_qn[j], diff_kn[j], diff_lam[j], diff_gn[j],
                0.8 - 0.6 * math.exp(-0.3 * l), j, n_pair, tuple(diff_kv))
            n_pool.append(p_new)
        else:
            x, s_new, *fox_kv, lf_new = odd_mixer(
                h, x, mod, ret_s[j], fox_k, fox_v, fox_logf[j], w_in_odd[j], w_fl_odd[j],
                w_out_odd[j], ret_gn[j], fox_qn[j], fox_kn[j], fox_fbias[j], j, n_pair, tuple(fox_kv))
            n_rs.append(s_new)
            n_fl.append(lf_new)
        h = norm_mod(x, norm2[l], mod, 3, 4)
        a = matmul([h], [w_up[l]], BF16, "relu2")
        x = matmul([a], [w_down[l]], F32, "resid", x, mod, 5, bk=_pick(a.shape[-1], 2048))
    st = jnp.stack
    return x, st(n_pool), diff_kv[0], diff_kv[1], st(n_rs), fox_kv[0], fox_kv[1], st(n_fl)


def kernel(x_prompt, x_sample, c_prompt, c_sample, cache_pool, cache_diff_k, cache_diff_v, state_ret,
           cache_fox_k, cache_fox_v, cache_fox_logf, w_ada, b_ada, norm1, norm2, w_up, w_down,
           w_in_even, w_out_even, pool_w, pool_scale, diff_qn, diff_kn, diff_lam, diff_gn,
           w_in_odd, w_out_odd, ret_gn, fox_qn, fox_kn, fox_fbias):
    bp, _, d = x_prompt.shape
    bs = x_sample.shape[0]
    n_pair = cache_pool.shape[0]
    depth = w_ada.shape[0]
    diff_heads = cache_diff_k.shape[3]
    ret_heads = state_ret.shape[2]
    fox_heads = cache_fox_k.shape[3]

    mc = -(-(bp + bs) // 16) * 16
    c_all = jnp.concatenate([c_prompt, c_sample, jnp.zeros((mc - bp - bs, d), F32)], axis=0)
    mod_all = ada_mod(c_all, w_ada, b_ada)
    mods_p = mod_all[:, :bp].reshape(depth, bp, 1, 6 * d)
    mods_s = mod_all[:, bp:bp + bs].reshape(depth, bs, 1, 6 * d)

    fox_main = w_in_odd.shape[-1] - fox_heads
    w_fl = jnp.pad(w_in_odd[:, :, fox_main:], ((0, 0), (0, 0), (0, HEAD_W - fox_heads))).astype(BF16)
    wts = (norm1, norm2, w_up.astype(BF16), w_down.astype(BF16), w_in_even.astype(BF16),
           w_out_even.astype(BF16), pool_w, pool_scale, diff_qn, diff_kn, diff_lam, diff_gn,
           w_in_odd[:, :, :fox_main].astype(BF16), w_fl, w_out_odd.astype(BF16),
           ret_gn, fox_qn, fox_kn, fox_fbias)

    dt = x_prompt.dtype
    y_p, pool_p, dk_p, dv_p, rs_p, fk_p, fv_p, fl_p = trunk(
        x_prompt, mods_p,
        jnp.zeros((n_pair, bp, POOL_MAX - 1, POOL_DIM), dt),
        jnp.zeros((n_pair, bp, diff_heads, 0, HEAD_W), BF16),
        jnp.zeros((n_pair, bp, diff_heads, 0, HEAD_W), BF16),
        jnp.zeros((n_pair, bp, ret_heads, RET_DK, RET_DV), F32),
        jnp.zeros((n_pair, bp, fox_heads, 0, HEAD_W), BF16),
        jnp.zeros((n_pair, bp, fox_heads, 0, HEAD_W), BF16),
        jnp.zeros((n_pair, bp, 0, fox_heads), F32),
        wts)

    def by_head(cache):
        if cache.shape[3] % 8:
            return jnp.swapaxes(cache, 2, 3)
        return head_major(cache)

    y_s, pool_s, dk_s, dv_s, rs_s, fk_s, fv_s, fl_s = trunk(
        x_sample, mods_s, cache_pool, by_head(cache_diff_k), by_head(cache_diff_v), state_ret,
        by_head(cache_fox_k), by_head(cache_fox_v), cache_fox_logf, wts)
    return (y_p, y_s, pool_p, pool_s, dk_p, dk_s, dv_p, dv_s, rs_p, rs_s,
            fk_p, fk_s, fv_p, fv_s, fl_p, fl_s)
```

```python
import functools
import math

import jax
import jax.numpy as jnp
from jax import lax
from jax.experimental import pallas as pl
from jax.experimental.pallas import tpu as pltpu

F32 = jnp.float32
BF16 = jnp.bfloat16
EPS = 1e-6
CHUNK = 64

POOL_WINDOWS = (2, 4, 8, 16)
POOL_MAX = max(POOL_WINDOWS)
POOL_GDIM = 128
POOL_DIM = POOL_GDIM * len(POOL_WINDOWS)

HEAD_W = 128
DIFF_DH = 64
ROPE_DIM = DIFF_DH // 4
ROPE_THETA = 500000.0

RET_DK = 256
RET_DV = 256
RET_THETA = 10000.0

FOX_DH = 128

LOG2E = math.log2(math.e)
PACK_ROWS = 16
FLASH_UNROLL = 4
FILL_UNROLL = 8
NORM_COLS = 512

VMEM_LIMIT_BYTES = 56 * 1024 * 1024


def _cparams(*sem):
    return pltpu.CompilerParams(dimension_semantics=sem, vmem_limit_bytes=VMEM_LIMIT_BYTES)


def _pick(n, pref):
    t = min(pref, n)
    while n % t:
        t //= 2
    return t


def _token_tile(b, l, rows):
    if l >= rows:
        return 1, _pick(l, rows)
    return _pick(b, max(rows // l, 1)), l


def _silu(x):
    return x / (1.0 + jnp.exp(-x))


def _ada_kernel(c_ref, w_ref, b_ref, o_ref):
    c = c_ref[...]
    ca = _silu(c).astype(BF16)
    o_ref[...] = jnp.dot(ca, w_ref[...].astype(BF16), preferred_element_type=F32) + b_ref[...]


def ada_mod(c_all, w_ada, b_ada):
    depth, d, n = w_ada.shape
    mc = c_all.shape[0]
    bn = _pick(n, 1024)
    return pl.pallas_call(
        _ada_kernel,
        grid=(depth, n // bn),
        in_specs=[pl.BlockSpec((mc, d), lambda l, j: (0, 0)),
                  pl.BlockSpec((None, d, bn), lambda l, j: (l, 0, j)),
                  pl.BlockSpec((None, 1, bn), lambda l, j: (l, 0, j))],
        out_specs=pl.BlockSpec((None, mc, bn), lambda l, j: (l, 0, j)),
        out_shape=jax.ShapeDtypeStruct((depth, mc, n), F32),
        compiler_params=_cparams("arbitrary", "arbitrary"),
        name="ada_mod",
    )(c_all, w_ada, b_ada.reshape(depth, 1, n))


def _norm_mod_kernel(x_ref, g_ref, sh_ref, sc_ref, o_ref):
    d = x_ref.shape[-1]
    cw = _pick(d, NORM_COLS)
    cols = [slice(c * cw, (c + 1) * cw) for c in range(d // cw)]
    ss = None
    for sl in cols:
        x = x_ref[:, :, sl]
        part = jnp.sum(x * x, axis=-1, keepdims=True)
        ss = part if ss is None else ss + part
    r = lax.rsqrt(ss * (1.0 / d) + EPS)
    for sl in cols:
        y = x_ref[:, :, sl] * r * g_ref[:, sl]
        o_ref[:, :, sl] = (y * (1.0 + sc_ref[:, :, sl]) + sh_ref[:, :, sl]).astype(o_ref.dtype)


def norm_mod(x, g, mod, shift_idx, scale_idx):
    b, l, d = x.shape
    bb, bl = _token_tile(b, l, 512)
    return pl.pallas_call(
        _norm_mod_kernel,
        grid=(b // bb, l // bl),
        in_specs=[pl.BlockSpec((bb, bl, d), lambda ib, il: (ib, il, 0)),
                  pl.BlockSpec((1, d), lambda ib, il: (0, 0)),
                  pl.BlockSpec((bb, 1, d), lambda ib, il: (ib, 0, shift_idx)),
                  pl.BlockSpec((bb, 1, d), lambda ib, il: (ib, 0, scale_idx))],
        out_specs=pl.BlockSpec((bb, bl, d), lambda ib, il: (ib, il, 0)),
        out_shape=jax.ShapeDtypeStruct((b, l, d), BF16),
        compiler_params=_cparams("arbitrary", "arbitrary"),
        name="norm_mod",
    )(x, g.reshape(1, d), mod, mod)


def _mm_kernel(*refs, n_lhs, nk, epilogue):
    lhs = refs[:n_lhs]
    ws = refs[n_lhs:2 * n_lhs]
    pos = 2 * n_lhs
    if epilogue == "resid":
        xres_ref, gate_ref = refs[pos], refs[pos + 1]
        pos += 2
    o_ref = refs[pos]

    def compute():
        acc = None
        for a_ref, w_ref in zip(lhs, ws):
            a = a_ref[...]
            a = a.reshape(-1, a.shape[-1])
            part = jnp.dot(a, w_ref[...], preferred_element_type=F32)
            acc = part if acc is None else acc + part
        return acc

    def finish(acc):
        if epilogue == "relu2":
            r = jnp.maximum(acc, 0.0)
            y = r * r
        elif epilogue == "resid":
            y = xres_ref[...] + gate_ref[...] * acc.reshape(o_ref.shape)
        else:
            y = acc
        o_ref[...] = y.reshape(o_ref.shape).astype(o_ref.dtype)

    if nk == 1:
        finish(compute())
    else:
        acc_ref = refs[pos + 1]
        k = pl.program_id(3)

        @pl.when(k == 0)
        def _():
            acc_ref[...] = compute()

        @pl.when(k > 0)
        def _():
            acc_ref[...] += compute()

        @pl.when(k == nk - 1)
        def _():
            finish(acc_ref[...])


def matmul(lhs_list, w_list, out_dtype, epilogue="none", xres=None, mod=None, gate_idx=0,
           rows=1024, cols=1024, bk=None):
    b, l, _ = lhs_list[0].shape
    n = w_list[0].shape[1]
    bb, bl = _token_tile(b, l, rows)
    bn = _pick(n, cols)
    ks = [a.shape[-1] for a in lhs_list]
    nk = 1 if bk is None else ks[0] // bk
    assert nk == 1 or len(lhs_list) == 1
    bks = ks if nk == 1 else [bk]
    in_specs = [pl.BlockSpec((bb, bl, kk), lambda ib, il, j, k: (ib, il, k)) for kk in bks]
    in_specs += [pl.BlockSpec((kk, bn), lambda ib, il, j, k: (k, j)) for kk in bks]
    args = list(lhs_list) + list(w_list)
    if epilogue == "resid":
        gate_off = gate_idx * (n // bn)
        in_specs += [pl.BlockSpec((bb, bl, bn), lambda ib, il, j, k: (ib, il, j)),
                     pl.BlockSpec((bb, 1, bn), lambda ib, il, j, k: (ib, 0, gate_off + j))]
        args += [xres, mod]
    scratch = [pltpu.VMEM((bb * bl, bn), F32)] if nk > 1 else []
    return pl.pallas_call(
        functools.partial(_mm_kernel, n_lhs=len(lhs_list), nk=nk, epilogue=epilogue),
        grid=(b // bb, l // bl, n // bn, nk),
        in_specs=in_specs,
        out_specs=pl.BlockSpec((bb, bl, bn), lambda ib, il, j, k: (ib, il, j)),
        out_shape=jax.ShapeDtypeStruct((b, l, n), out_dtype),
        scratch_shapes=scratch,
        compiler_params=_cparams("arbitrary", "arbitrary", "arbitrary", "arbitrary"),
        name="matmul_" + epilogue,
    )(*args)


def _prep_even_kernel(*refs, t, p, nl, heads, n_carry):
    proj_ref, prev_ref, c_ref, s1_ref, s2_ref, qg_ref, kg_ref, pw_ref, ps_ref = refs[:9]
    pool_ref, npool_ref, q_ref, k32_ref, kb_ref, v32_ref, vb_ref, full_ref = refs[9 + n_carry:]
    il = pl.program_id(1)

    @pl.when(il == 0)
    def _():
        full_ref[0:POOL_MAX, :] = prev_ref[...]

    @pl.when(il > 0)
    def _():
        full_ref[0:POOL_MAX, :] = full_ref[t:t + POOL_MAX, :]

    full_ref[POOL_MAX:POOL_MAX + t, :] = proj_ref[:, 0:POOL_DIM]
    pos1 = p + il * t + 1 + lax.broadcasted_iota(jnp.int32, (t, 1), 0)
    for g, w in enumerate(POOL_WINDOWS):
        sl = slice(g * POOL_GDIM, (g + 1) * POOL_GDIM)
        u = full_ref[POOL_MAX:POOL_MAX + t, sl]
        win = u
        for s in range(1, w):
            win = win + full_ref[POOL_MAX - s:POOL_MAX - s + t, sl]
        cnt = jnp.minimum(pos1, w).astype(F32)
        d = (win / cnt - u).astype(BF16)
        mixed = jnp.dot(d, pw_ref[g], preferred_element_type=F32) * ps_ref[:, sl]
        pool_ref[:, sl] = mixed.astype(pool_ref.dtype)

    @pl.when(il == nl - 1)
    def _():
        npool_ref[...] = full_ref[t + 1:t + POOL_MAX, :]

    seg = (lax.broadcasted_iota(jnp.int32, (HEAD_W, HEAD_W), 0) >> 6) == \
          (lax.broadcasted_iota(jnp.int32, (HEAD_W, HEAD_W), 1) >> 6)
    seg = seg.astype(F32).astype(BF16)
    rc, rs1, rs2 = c_ref[...], s1_ref[...], s2_ref[...]

    def norm_rope(x, g):
        x2 = x * x
        hi = x2.astype(BF16)
        lo = (x2 - hi.astype(F32)).astype(BF16)
        ss = (jnp.dot(lo, seg, preferred_element_type=F32)
              + jnp.dot(hi, seg, preferred_element_type=F32))
        y = x * lax.rsqrt(ss * (1.0 / DIFF_DH) + EPS) * g
        return y * rc + pltpu.roll(y, 8, 1) * rs2 + pltpu.roll(y, HEAD_W - 8, 1) * rs1

    width = heads * HEAD_W
    q_off, k_off, v_off = POOL_DIM, POOL_DIM + width, POOL_DIM + 2 * width
    qg, kg = qg_ref[...], kg_ref[...]
    ks, vs = [], []
    for h in range(heads):
        sl = slice(h * HEAD_W, (h + 1) * HEAD_W)
        rq = norm_rope(proj_ref[:, q_off + h * HEAD_W:q_off + (h + 1) * HEAD_W], qg)
        q_ref[:, sl] = (rq * (DIFF_DH ** -0.5 * LOG2E)).astype(BF16)
        rk = norm_rope(proj_ref[:, k_off + h * HEAD_W:k_off + (h + 1) * HEAD_W], kg)
        kb_ref[:, sl] = rk.astype(BF16)
        ks.append(rk)
        vs.append(proj_ref[:, v_off + h * HEAD_W:v_off + (h + 1) * HEAD_W])
    k32_ref[...] = pltpu.einshape("htd->thd", jnp.stack(ks, axis=0))
    v32_ref[...] = pltpu.einshape("htd->thd", jnp.stack(vs, axis=0))
    vb_ref[...] = proj_ref[:, v_off:v_off + width].astype(BF16)


def _diff_rope_tables(p, l):
    inv_freq = ROPE_THETA ** (-jnp.arange(0, ROPE_DIM, 2, dtype=F32) / ROPE_DIM)
    ang = (p + jnp.arange(l)).astype(F32)[:, None] * inv_freq[None, :]
    cos, sin = jnp.cos(ang), jnp.sin(ang)
    half = ROPE_DIM // 2
    rest = DIFF_DH - ROPE_DIM
    c = jnp.concatenate([cos, cos, jnp.ones((l, rest), F32)], axis=-1)
    s1 = jnp.concatenate([-sin, jnp.zeros((l, half + rest), F32)], axis=-1)
    s2 = jnp.concatenate([jnp.zeros((l, half), F32), sin, jnp.zeros((l, rest), F32)], axis=-1)
    return tuple(jnp.concatenate([a, a], axis=-1) for a in (c, s1, s2))


def prep_even(proj, pool_prev, p, qn, kn, pool_w, pool_scale, layer_slot, n_slots, carry):
    b, l, n_in = proj.shape
    width = (n_in - POOL_DIM) // 3
    heads = width // HEAD_W
    t = _pick(l, 256)
    nl = l // t
    prev16 = jnp.concatenate([jnp.zeros((b, 1, POOL_DIM), F32), pool_prev.astype(F32)], axis=1)
    rc, rs1, rs2 = _diff_rope_tables(p, l)
    qg = jnp.concatenate([qn, qn]).reshape(1, HEAD_W).astype(F32)
    kg = jnp.concatenate([kn, kn]).reshape(1, HEAD_W).astype(F32)
    tok = lambda w: pl.BlockSpec((None, t, w), lambda ib, il: (ib, il, 0))
    tab = pl.BlockSpec((t, HEAD_W), lambda ib, il: (il, 0))
    vec = pl.BlockSpec((1, HEAD_W), lambda ib, il: (0, 0))
    out_shapes = (
        jax.ShapeDtypeStruct((b, l, POOL_DIM), BF16),
        jax.ShapeDtypeStruct((b, POOL_MAX - 1, POOL_DIM), F32),
        jax.ShapeDtypeStruct((b, l, width), BF16),
        jax.ShapeDtypeStruct((n_slots, b, l, heads, HEAD_W), F32),
        jax.ShapeDtypeStruct((b, l, width), BF16),
        jax.ShapeDtypeStruct((n_slots, b, l, heads, HEAD_W), F32),
        jax.ShapeDtypeStruct((b, l, width), BF16),
    )
    stacked = pl.BlockSpec((None, None, t, heads, HEAD_W), lambda ib, il: (layer_slot, ib, il, 0, 0))
    return pl.pallas_call(
        functools.partial(_prep_even_kernel, t=t, p=p, nl=nl, heads=heads, n_carry=len(carry)),
        grid=(b, nl),
        in_specs=[tok(n_in),
                  pl.BlockSpec((None, POOL_MAX, POOL_DIM), lambda ib, il: (ib, 0, 0)),
                  tab, tab, tab, vec, vec,
                  pl.BlockSpec((len(POOL_WINDOWS), POOL_GDIM, POOL_GDIM), lambda ib, il: (0, 0, 0)),
                  pl.BlockSpec((1, POOL_DIM), lambda ib, il: (0, 0))]
                 + [pl.BlockSpec(memory_space=pl.ANY)] * len(carry),
        out_specs=(tok(POOL_DIM),
                   pl.BlockSpec((None, POOL_MAX - 1, POOL_DIM), lambda ib, il: (ib, 0, 0)),
                   tok(width), stacked, tok(width), stacked, tok(width)),
        out_shape=out_shapes,
        scratch_shapes=[pltpu.VMEM((POOL_MAX + t, POOL_DIM), F32)],
        input_output_aliases={9 + i: o for i, o in zip(range(len(carry)), (3, 5))},
        compiler_params=_cparams("arbitrary", "arbitrary"),
        name="prep_even",
    )(proj, prev16, rc, rs1, rs2, qg, kg, pool_w.astype(BF16), pool_scale.reshape(1, POOL_DIM), *carry)


def _prep_odd_kernel(*refs, heads, n_carry):
    fq_ref, fk_ref, fv_ref, fl_ref, qg_ref, kg_ref, fb_ref = refs[:7]
    q_ref, k32_ref, kb_ref, v32_ref, vb_ref, lf_ref = refs[7 + n_carry:]
    def rms(x, g):
        return x * lax.rsqrt(jnp.mean(x * x, axis=-1, keepdims=True) + EPS) * g

    qg, kg = qg_ref[...], kg_ref[...]
    ks, vs = [], []
    for h in range(heads):
        sl = slice(h * HEAD_W, (h + 1) * HEAD_W)
        q_ref[:, sl] = (rms(fq_ref[:, sl], qg) * (FOX_DH ** -0.5 * LOG2E)).astype(BF16)
        rk = rms(fk_ref[:, sl], kg)
        kb_ref[:, sl] = rk.astype(BF16)
        ks.append(rk)
        vs.append(fv_ref[:, sl])
    k32_ref[...] = pltpu.einshape("htd->thd", jnp.stack(ks, axis=0))
    v32_ref[...] = pltpu.einshape("htd->thd", jnp.stack(vs, axis=0))
    vb_ref[...] = fv_ref[...].astype(BF16)
    x = fl_ref[...] + fb_ref[...]
    logf = -(jnp.maximum(-x, 0.0) + jnp.log1p(jnp.exp(-jnp.abs(x))))
    lf_ref[...] = logf[:, 0:heads]


def prep_odd(proj, fl, fox_off, heads, qn, kn, f_bias, layer_slot, n_slots, carry):
    b, l, _ = proj.shape
    width = heads * HEAD_W
    t = _pick(l, 512)
    cb = fox_off // width
    tok = lambda w, c: pl.BlockSpec((None, t, w), lambda ib, il: (ib, il, c))
    vec = pl.BlockSpec((1, HEAD_W), lambda ib, il: (0, 0))
    fb = jnp.zeros((1, HEAD_W), F32).at[0, :heads].set(f_bias.astype(F32))
    out_shapes = (
        jax.ShapeDtypeStruct((b, l, width), BF16),
        jax.ShapeDtypeStruct((n_slots, b, l, heads, HEAD_W), F32),
        jax.ShapeDtypeStruct((b, l, width), BF16),
        jax.ShapeDtypeStruct((n_slots, b, l, heads, HEAD_W), F32),
        jax.ShapeDtypeStruct((b, l, width), BF16),
        jax.ShapeDtypeStruct((b, l, heads), F32),
    )
    stacked = pl.BlockSpec((None, None, t, heads, HEAD_W), lambda ib, il: (layer_slot, ib, il, 0, 0))
    return pl.pallas_call(
        functools.partial(_prep_odd_kernel, heads=heads, n_carry=len(carry)),
        grid=(b, l // t),
        in_specs=[tok(width, cb), tok(width, cb + 1), tok(width, cb + 2), tok(HEAD_W, 0),
                  vec, vec, vec] + [pl.BlockSpec(memory_space=pl.ANY)] * len(carry),
        out_specs=(tok(width, 0), stacked, tok(width, 0), stacked, tok(width, 0),
                   tok(heads, 0)),
        out_shape=out_shapes,
        input_output_aliases={7 + i: o for i, o in zip(range(len(carry)), (1, 3))},
        compiler_params=_cparams("arbitrary", "arbitrary"),
        name="prep_odd",
    )(proj, proj, proj, fl, qn.reshape(1, HEAD_W), kn.reshape(1, HEAD_W), fb, *carry)


CUMSUM_BLOCK = 256


def _cumsum_kernel(x_ref, o_ref, carry_ref):
    @pl.when(pl.program_id(0) == 0)
    def _():
        carry_ref[...] = jnp.zeros_like(carry_ref)

    x = x_ref[...]
    x1 = x.astype(BF16)
    r1 = x - x1.astype(F32)
    x2 = r1.astype(BF16)
    x3 = (r1 - x2.astype(F32)).astype(BF16)
    n = x.shape[-1]
    tri = lax.broadcasted_iota(jnp.int32, (n, n), 0) <= lax.broadcasted_iota(jnp.int32, (n, n), 1)
    tri = tri.astype(F32).astype(BF16)
    c = (jnp.dot(x3, tri, preferred_element_type=F32)
         + jnp.dot(x2, tri, preferred_element_type=F32)
         + jnp.dot(x1, tri, preferred_element_type=F32)) + carry_ref[...]
    o_ref[...] = c
    carry_ref[...] = c[:, n - 1:n]


def cumsum_lanes(x):
    r, n = x.shape
    return pl.pallas_call(
        _cumsum_kernel,
        grid=(n // CUMSUM_BLOCK,),
        in_specs=[pl.BlockSpec((r, CUMSUM_BLOCK), lambda i: (0, i))],
        out_specs=pl.BlockSpec((r, CUMSUM_BLOCK), lambda i: (0, i)),
        out_shape=jax.ShapeDtypeStruct((r, n), F32),
        scratch_shapes=[pltpu.VMEM((r, 1), F32)],
        compiler_params=_cparams("arbitrary"),
        name="cumsum",
    )(x)


def _flash_kernel(*refs, kind, tq, tk, p, lam_init, rq, single_tile, has_past):
    if has_past:
        pk_ref, pv_ref, *refs = refs
    if kind == "diff":
        (q_ref, k_ref, v_ref, lv_ref, gn_ref, o_ref,
         qzt_scr, s0_scr, s1_scr, mx0_scr, mx1_scr, p0_scr, p1_scr, acc_scr, l_scr) = refs
    else:
        (q_ref, k_ref, v_ref, cq_ref, ck_ref, o_ref,
         qzt_scr, s0_scr, s1_scr, mx0_scr, mx1_scr, p0_scr, p1_scr, acc_scr, l_scr,
         ckcol_ref) = refs
    s_scr, mx_scr, p_scr = (s0_scr, s1_scr), (mx0_scr, mx1_scr), (p0_scr, p1_scr)
    iq = pl.program_id(2)
    q0 = iq * tq
    qf = q_ref[...].astype(F32)
    if kind == "diff":
        lane = lax.broadcasted_iota(jnp.int32, (tq, HEAD_W), 1)
        qz = jnp.concatenate([jnp.where(lane < DIFF_DH, qf, 0.0),
                              jnp.where(lane >= DIFF_DH, qf, 0.0)], axis=0)
    else:
        qz = qf if rq == tq else jnp.concatenate([qf, jnp.zeros((rq - tq, HEAD_W), F32)], axis=0)
        cq = cq_ref[pl.ds(iq, 1), :] * LOG2E

        @pl.when(iq == 0)
        def _():
            def fill(c, carry):
                row = jnp.broadcast_to(ck_ref[pl.ds(c, 1), :] * LOG2E, (HEAD_W, HEAD_W))
                ckcol_ref[pl.ds(pl.multiple_of(c * HEAD_W, HEAD_W), HEAD_W), :] = row.T
                return carry
            n_chunks = ck_ref.shape[0]
            lax.fori_loop(0, n_chunks, fill, 0,
                          unroll=FILL_UNROLL if n_chunks % FILL_UNROLL == 0 else True)

    qzt_scr[...] = qz.T.astype(BF16)
    acc_scr[...] = jnp.zeros_like(acc_scr)

    def produce(slot, j, masked):
        start = pl.multiple_of(j * tk, tk)
        if has_past:
            s = jnp.concatenate(
                [jnp.dot(pk_ref[...].astype(BF16), qzt_scr[...], preferred_element_type=F32),
                 jnp.dot(k_ref[...], qzt_scr[...], preferred_element_type=F32)], axis=0)
        else:
            s = jnp.dot(k_ref[pl.ds(start, tk), :], qzt_scr[...], preferred_element_type=F32)
        if kind == "fox":
            ckc = ckcol_ref[pl.ds(start, tk), :]
            s = jnp.concatenate([s[:, c * HEAD_W:(c + 1) * HEAD_W] - ckc
                                 for c in range(rq // HEAD_W)], axis=1)
        if masked:
            kpos = start + lax.broadcasted_iota(jnp.int32, (tk, rq), 0)
            r = lax.broadcasted_iota(jnp.int32, (tk, rq), 1)
            if kind == "diff":
                qpos = p + q0 + jnp.where(r >= tq, r - tq, r)
                ok = (kpos >> 6) <= (qpos >> 6)
            else:
                ok = kpos <= p + q0 + r
            s = jnp.where(ok, s, -jnp.inf)
        s_scr[slot][...] = s
        mx_scr[slot][...] = jnp.max(s, axis=0, keepdims=True)

    def consume(slot, j, m, l):
        tmax = mx_scr[slot][...]
        if kind == "fox":
            m_new = jnp.maximum(m, tmax + cq)
            shift = cq - m_new
        else:
            m_new = jnp.maximum(m, tmax)
            shift = -m_new
        alpha = jnp.exp2(m - m_new)
        shift_b = jnp.broadcast_to(shift, (PACK_ROWS, rq))
        lsum = jnp.zeros((PACK_ROWS, rq), F32)
        for c in range(tk // PACK_ROWS):
            rows = slice(c * PACK_ROWS, (c + 1) * PACK_ROWS)
            pr = jnp.exp2(s_scr[slot][rows, :] + shift_b)
            lsum = lsum + pr
            p_scr[slot][rows, :] = pr.astype(BF16)
        start = pl.multiple_of(j * tk, tk)
        tn = (((0,), (0,)), ((), ()))
        if has_past:
            pv = (lax.dot_general(pv_ref[...].astype(BF16), p_scr[slot][0:p, :], tn,
                                  preferred_element_type=F32)
                  + lax.dot_general(v_ref[...], p_scr[slot][p:tk, :], tn,
                                    preferred_element_type=F32))
        else:
            pv = lax.dot_general(v_ref[pl.ds(start, tk), :], p_scr[slot][...], tn,
                                 preferred_element_type=F32)
        acc_scr[...] = acc_scr[...] * alpha + pv
        return m_new, alpha * l + jnp.sum(lsum, axis=0, keepdims=True)

    def run(j0, count, m, l, ends_masked):
        for i in range(count):
            slot = i % 2
            if i + 1 < count:
                produce(1 - slot, j0 + i + 1, ends_masked and i + 2 == count)
            elif not ends_masked:
                produce(1 - slot, j0 + count, False)
            m, l = consume(slot, j0 + i, m, l)
        return m, l

    m0 = jnp.full((1, rq), -jnp.inf, F32)
    l0 = jnp.zeros((1, rq), F32)
    if single_tile:
        produce(0, 0, True)
        l_scr[...] = run(0, 1, m0, l0, True)[1]
    else:
        n_full = lax.div(p + q0 + tq + tk - 1, tk) - 1
        n_iter = lax.div(jnp.maximum(n_full - 1, 0), FLASH_UNROLL)

        @pl.when(n_full > 0)
        def _():
            produce(0, 0, False)

        m, l = lax.fori_loop(
            0, n_iter, lambda t, c: run(FLASH_UNROLL * t, FLASH_UNROLL, c[0], c[1], False), (m0, l0))
        first = FLASH_UNROLL * n_iter
        rest = n_full - first

        @pl.when(n_full == 0)
        def _():
            produce(0, 0, True)
            l_scr[...] = run(0, 1, m, l, True)[1]

        for r in range(1, FLASH_UNROLL + 1):
            @pl.when(jnp.logical_and(n_full > 0, rest == r))
            def _(r=r):
                l_scr[...] = run(first, r + 1, m, l, True)[1]

    o = (acc_scr[...] / l_scr[...]).T
    if kind == "diff":
        lv = lv_ref[...]
        lam = (jnp.exp(jnp.sum(lv[0:1] * lv[1:2], axis=-1, keepdims=True))
               - jnp.exp(jnp.sum(lv[2:3] * lv[3:4], axis=-1, keepdims=True)) + lam_init)
        o = o[:tq] - lam * o[tq:]
        o = o * lax.rsqrt(jnp.mean(o * o, axis=-1, keepdims=True) + EPS) * gn_ref[...]
        o = o * (1.0 - lam_init)
    else:
        o = o[:tq]
    o_ref[...] = o.astype(o_ref.dtype)


def _head_major_kernel(x_ref, o_ref):
    o_ref[...] = pltpu.einshape("thd->htd", x_ref[...]).astype(o_ref.dtype)


def head_major(cache):
    n, b, p, heads, w = cache.shape
    pc = _pick(p, 512)
    return pl.pallas_call(
        _head_major_kernel,
        grid=(n, b, p // pc),
        in_specs=[pl.BlockSpec((None, None, pc, heads, w), lambda j, ib, ip: (j, ib, ip, 0, 0))],
        out_specs=pl.BlockSpec((None, None, heads, pc, w), lambda j, ib, ip: (j, ib, 0, ip, 0)),
        out_shape=jax.ShapeDtypeStruct((n, b, heads, p, w), BF16),
        compiler_params=_cparams("arbitrary", "arbitrary", "arbitrary"),
        name="head_major",
    )(cache)


def flash(kind, q, k_new, v_new, past_k, past_v, layer_slot, tq_pref, tk_pref, extra, lam_init=0.0):
    b, l, width = q.shape
    heads = width // HEAD_W
    p = past_k.shape[3]
    has_past = p > 0
    tq = _pick(l, tq_pref)
    tk = p + l if has_past else _pick(l, tk_pref)
    assert tk % PACK_ROWS == 0 and (tk == p + l or tk % tq == 0)
    qspec = pl.BlockSpec((None, tq, HEAD_W), lambda ib, ih, iq: (ib, iq, ih))
    kvspec = pl.BlockSpec((None, l, HEAD_W), lambda ib, ih, iq: (ib, 0, ih))
    pastspec = pl.BlockSpec((None, None, None, p, HEAD_W), lambda ib, ih, iq: (layer_slot, ib, ih, 0, 0))
    past_specs, past_args = ([pastspec, pastspec], [past_k, past_v]) if has_past else ([], [])
    rq = 2 * tq if kind == "diff" else max(tq, HEAD_W)
    scratch = ([pltpu.VMEM((HEAD_W, rq), BF16)]
               + [pltpu.VMEM((tk, rq), F32)] * 2
               + [pltpu.VMEM((1, rq), F32)] * 2
               + [pltpu.VMEM((tk, rq), BF16)] * 2
               + [pltpu.VMEM((HEAD_W, rq), F32),
                  pltpu.VMEM((1, rq), F32)])
    if kind == "diff":
        lam_vec, gn = extra
        especs = [pl.BlockSpec(lam_vec.shape, lambda ib, ih, iq: (0, 0)),
                  pl.BlockSpec((1, HEAD_W), lambda ib, ih, iq: (0, 0))]
        eargs = [lam_vec, gn.reshape(1, HEAD_W)]
    else:
        cum = extra
        lc = cum.shape[-1]
        assert lc % HEAD_W == 0 and lc >= p + l
        cq = cum[:, :, p:p + l].reshape(b, heads, l // tq, tq)
        if rq > tq:
            cq = jnp.pad(cq, ((0, 0), (0, 0), (0, 0), (0, rq - tq)))
        ck = cum.reshape(b, heads, lc // HEAD_W, HEAD_W)
        especs = [pl.BlockSpec((None, None, l // tq, rq), lambda ib, ih, iq: (ib, ih, 0, 0)),
                  pl.BlockSpec((None, None, lc // HEAD_W, HEAD_W), lambda ib, ih, iq: (ib, ih, 0, 0))]
        eargs = [cq, ck]
        scratch.append(pltpu.VMEM((lc, HEAD_W), F32))
    return pl.pallas_call(
        functools.partial(_flash_kernel, kind=kind, tq=tq, tk=tk, p=p, lam_init=lam_init, rq=rq,
                          single_tile=(tk == p + l), has_past=has_past),
        grid=(b, heads, l // tq),
        in_specs=past_specs + [qspec, kvspec, kvspec] + especs,
        out_specs=qspec,
        out_shape=jax.ShapeDtypeStruct((b, l, width), BF16),
        scratch_shapes=scratch,
        compiler_params=_cparams("arbitrary", "arbitrary", "arbitrary"),
        name="flash_" + kind,
    )(*past_args, q, k_new, v_new, *eargs)


def _ret_kernel(lg_ref, q_ref, k_ref, v_ref, g_ref, cos_ref, sin_ref, s0_ref, gn_ref,
                o_ref, sn_ref, s_scr, *, c, nc):
    t = pl.program_id(2)

    @pl.when(t == 0)
    def _():
        s_scr[...] = s0_ref[...]

    cos, sin = cos_ref[...], sin_ref[...]
    half = RET_DK // 2

    def rot(x):
        x1, x2 = x[:, :half], x[:, half:]
        return jnp.concatenate([x1 * cos - x2 * sin, x2 * cos + x1 * sin], axis=-1)

    q = rot(q_ref[...])
    k = rot(k_ref[...]) * RET_DK ** -0.5
    vb = v_ref[...].astype(BF16)
    lg = lg_ref[pl.program_id(1)]
    dist = (lax.broadcasted_iota(jnp.int32, (c, c), 0)
            - lax.broadcasted_iota(jnp.int32, (c, c), 1)).astype(F32)
    intra = jnp.where(dist >= 0, jnp.exp(lg * jnp.maximum(dist, 0.0)), 0.0)
    ic = lax.broadcasted_iota(jnp.int32, (c, 1), 0).astype(F32)
    q_dec = jnp.exp(lg * (ic + 1.0))
    k_dec = jnp.exp(lg * (c - 1.0 - ic))
    blk_dec = jnp.exp(lg * jnp.full((1, RET_DV), float(c), F32))
    att = lax.dot_general(q.astype(BF16), k.astype(BF16), (((1,), (1,)), ((), ())),
                          preferred_element_type=F32) * intra
    s = s_scr[...]
    o = (jnp.dot(att.astype(BF16), vb, preferred_element_type=F32)
         + jnp.dot((q * q_dec).astype(BF16), s.astype(BF16), preferred_element_type=F32))
    s_new = s * blk_dec + lax.dot_general((k * k_dec).astype(BF16), vb, (((0,), (0,)), ((), ())),
                                          preferred_element_type=F32)
    s_scr[...] = s_new
    mu = jnp.mean(o, axis=-1, keepdims=True)
    oc = o - mu
    var = jnp.mean(oc * oc, axis=-1, keepdims=True)
    y = oc * lax.rsqrt(var + EPS) * gn_ref[...]
    o_ref[...] = (y * _silu(g_ref[...])).astype(o_ref.dtype)

    @pl.when(t == nc - 1)
    def _():
        sn_ref[...] = s_new


def retention(proj, p, s0, ret_gn):
    b, l, _ = proj.shape
    heads = s0.shape[1]
    c = _pick(l, 256)
    nc = l // c
    inv_freq = RET_THETA ** (-jnp.arange(0, RET_DK, 2, dtype=F32) / RET_DK)
    ang = (p + jnp.arange(l)).astype(F32)[:, None] * inv_freq[None, :]
    cos, sin = jnp.cos(ang), jnp.sin(ang)
    lg = jnp.asarray([math.log(1.0 - 2.0 ** (-5 - h)) for h in range(heads)], F32)
    col = lambda off: pl.BlockSpec((None, c, RET_DK), lambda ib, ih, it: (ib, it, off + ih))
    tab = pl.BlockSpec((c, RET_DK // 2), lambda ib, ih, it: (it, 0))
    st = pl.BlockSpec((None, None, RET_DK, RET_DV), lambda ib, ih, it: (ib, ih, 0, 0))
    return pl.pallas_call(
        functools.partial(_ret_kernel, c=c, nc=nc),
        grid=(b, heads, nc),
        in_specs=[pl.BlockSpec(memory_space=pltpu.SMEM),
                  col(0), col(heads), col(2 * heads), col(3 * heads), tab, tab, st,
                  pl.BlockSpec((None, 1, RET_DV), lambda ib, ih, it: (ih, 0, 0))],
        out_specs=(pl.BlockSpec((None, c, RET_DV), lambda ib, ih, it: (ib, it, ih)), st),
        out_shape=(jax.ShapeDtypeStruct((b, l, heads * RET_DV), BF16),
                   jax.ShapeDtypeStruct(s0.shape, F32)),
        scratch_shapes=[pltpu.VMEM((RET_DK, RET_DV), F32)],
        compiler_params=_cparams("arbitrary", "arbitrary", "arbitrary"),
        name="retention",
    )(lg, proj, proj, proj, proj, cos, sin, s0.astype(F32), ret_gn.reshape(heads, 1, RET_DV))


def even_mixer(h, x, mod, pool_prev, past_k, past_v, w_in, w_out, pool_w, pool_scale,
               qn, kn, lam_vec, out_gn, lam_init, layer_slot, n_slots, carry):
    p = past_k.shape[3]
    proj = matmul([h], [w_in], F32)
    pool_out, new_pool, q, k32, kb, v32, vb = prep_even(
        proj, pool_prev, p, qn, kn, pool_w, pool_scale, layer_slot, n_slots, carry)
    o = flash("diff", q, kb, vb, past_k, past_v, layer_slot, 256, 512,
              (lam_vec.astype(F32), out_gn.astype(F32)), lam_init)
    x = matmul([pool_out, o], [w_out[:POOL_DIM], w_out[POOL_DIM:]], F32, "resid", x, mod, 2)
    return x, new_pool, k32, v32


def odd_mixer(h, x, mod, s0, past_k, past_v, past_logf, w_in, w_fl, w_out, ret_gn, qn, kn, f_bias,
              layer_slot, n_slots, carry):
    p = past_k.shape[3]
    b, l, _ = h.shape
    ret_heads = s0.shape[1]
    fox_heads = f_bias.shape[0]
    proj = matmul([h], [w_in], F32)
    fl = matmul([h], [w_fl], F32)
    ret, s_new = retention(proj, p, s0, ret_gn)
    fox_off = 2 * ret_heads * RET_DK + 2 * ret_heads * RET_DV
    q, k32, kb, v32, vb, logf = prep_odd(proj, fl, fox_off, fox_heads, qn, kn, f_bias,
                                         layer_slot, n_slots, carry)
    lc = -(-(p + l) // CUMSUM_BLOCK) * CUMSUM_BLOCK
    lf_all = jnp.concatenate([past_logf.astype(F32), logf,
                              jnp.zeros((b, lc - p - l, fox_heads), F32)], axis=1)
    cum = cumsum_lanes(jnp.swapaxes(lf_all, 1, 2).reshape(b * fox_heads, lc))
    o = flash("fox", q, kb, vb, past_k, past_v, layer_slot, 512, 512, cum.reshape(b, fox_heads, lc))
    x = matmul([ret, o], [w_out[:ret.shape[-1]], w_out[ret.shape[-1]:]], F32, "resid", x, mod, 2)
    return x, s_new, k32, v32, logf


def trunk(x, mods, pool_prev, diff_k, diff_v, ret_s, fox_k, fox_v, fox_logf, wts):
    (norm1, norm2, w_up, w_down, w_in_even, w_out_even, pool_w, pool_scale, diff_qn, diff_kn,
     diff_lam, diff_gn, w_in_odd, w_fl_odd, w_out_odd, ret_gn, fox_qn, fox_kn, fox_fbias) = wts
    depth = norm1.shape[0]
    n_pair = depth // 2
    b, seq, _ = x.shape
    n_pool, n_rs, n_fl = [], [], []
    diff_kv, fox_kv = (), ()
    for l in range(depth):
        j = l // 2
        mod = mods[l]
        h = norm_mod(x, norm1[l], mod, 0, 1)
        if l % 2 == 0:
            x, p_new, *diff_kv = even_mixer(
                h, x, mod, pool_prev[j], diff_k, diff_v, w_in_even[j], w_out_even[j], pool_w[j],
                pool_scale[j], diff_qn[j], diff_kn[j], diff_lam[j], diff_gn[j],
                0.8 - 0.6 * math.exp(-0.3 * l), j, n_pair, tuple(diff_kv))
            n_pool.append(p_new)
        else:
            x, s_new, *fox_kv, lf_new = odd_mixer(
                h, x, mod, ret_s[j], fox_k, fox_v, fox_logf[j], w_in_odd[j], w_fl_odd[j],
                w_out_odd[j], ret_gn[j], fox_qn[j], fox_kn[j], fox_fbias[j], j, n_pair, tuple(fox_kv))
            n_rs.append(s_new)
            n_fl.append(lf_new)
        h = norm_mod(x, norm2[l], mod, 3, 4)
        a = matmul([h], [w_up[l]], BF16, "relu2")
        x = matmul([a], [w_down[l]], F32, "resid", x, mod, 5, bk=_pick(a.shape[-1], 2048))
    st = jnp.stack
    return x, st(n_pool), diff_kv[0], diff_kv[1], st(n_rs), fox_kv[0], fox_kv[1], st(n_fl)


def kernel(x_prompt, x_sample, c_prompt, c_sample, cache_pool, cache_diff_k, cache_diff_v, state_ret,
           cache_fox_k, cache_fox_v, cache_fox_logf, w_ada, b_ada, norm1, norm2, w_up, w_down,
           w_in_even, w_out_even, pool_w, pool_scale, diff_qn, diff_kn, diff_lam, diff_gn,
           w_in_odd, w_out_odd, ret_gn, fox_qn, fox_kn, fox_fbias):
    bp, _, d = x_prompt.shape
    bs = x_sample.shape[0]
    n_pair = cache_pool.shape[0]
    depth = w_ada.shape[0]
    diff_heads = cache_diff_k.shape[3]
    ret_heads = state_ret.shape[2]
    fox_heads = cache_fox_k.shape[3]

    mc = -(-(bp + bs) // 16) * 16
    c_all = jnp.concatenate([c_prompt, c_sample, jnp.zeros((mc - bp - bs, d), F32)], axis=0)
    mod_all = ada_mod(c_all, w_ada, b_ada)
    mods_p = mod_all[:, :bp].reshape(depth, bp, 1, 6 * d)
    mods_s = mod_all[:, bp:bp + bs].reshape(depth, bs, 1, 6 * d)

    fox_main = w_in_odd.shape[-1] - fox_heads
    w_fl = jnp.pad(w_in_odd[:, :, fox_main:], ((0, 0), (0, 0), (0, HEAD_W - fox_heads))).astype(BF16)
    wts = (norm1, norm2, w_up.astype(BF16), w_down.astype(BF16), w_in_even.astype(BF16),
           w_out_even.astype(BF16), pool_w, pool_scale, diff_qn, diff_kn, diff_lam, diff_gn,
           w_in_odd[:, :, :fox_main].astype(BF16), w_fl, w_out_odd.astype(BF16),
           ret_gn, fox_qn, fox_kn, fox_fbias)

    dt = x_prompt.dtype
    y_p, pool_p, dk_p, dv_p, rs_p, fk_p, fv_p, fl_p = trunk(
        x_prompt, mods_p,
        jnp.zeros((n_pair, bp, POOL_MAX - 1, POOL_DIM), dt),
        jnp.zeros((n_pair, bp, diff_heads, 0, HEAD_W), BF16),
        jnp.zeros((n_pair, bp, diff_heads, 0, HEAD_W), BF16),
        jnp.zeros((n_pair, bp, ret_heads, RET_DK, RET_DV), F32),
        jnp.zeros((n_pair, bp, fox_heads, 0, HEAD_W), BF16),
        jnp.zeros((n_pair, bp, fox_heads, 0, HEAD_W), BF16),
        jnp.zeros((n_pair, bp, 0, fox_heads), F32),
        wts)

    def by_head(cache):
        if cache.shape[3] % 8:
            return jnp.swapaxes(cache, 2, 3)
        return head_major(cache)

    y_s, pool_s, dk_s, dv_s, rs_s, fk_s, fv_s, fl_s = trunk(
        x_sample, mods_s, cache_pool, by_head(cache_diff_k), by_head(cache_diff_v), state_ret,
        by_head(cache_fox_k), by_head(cache_fox_v), cache_fox_logf, wts)
    return (y_p, y_s, pool_p, pool_s, dk_p, dk_s, dv_p, dv_s, rs_p, rs_s,
            fk_p, fk_s, fv_p, fv_s, fl_p, fl_s)
```

```python
import functools
import math

import jax
import jax.numpy as jnp
from jax import lax
from jax.experimental import pallas as pl
from jax.experimental.pallas import tpu as pltpu

F32 = jnp.float32
BF16 = jnp.bfloat16
EPS = 1e-6
CHUNK = 64

POOL_WINDOWS = (2, 4, 8, 16)
POOL_MAX = max(POOL_WINDOWS)
POOL_GDIM = 128
POOL_DIM = POOL_GDIM * len(POOL_WINDOWS)

HEAD_W = 128
DIFF_DH = 64
ROPE_DIM = DIFF_DH // 4
ROPE_THETA = 500000.0

RET_DK = 256
RET_DV = 256
RET_THETA = 10000.0

FOX_DH = 128

LOG2E = math.log2(math.e)
PACK_ROWS = 16
FLASH_UNROLL = 4
FILL_UNROLL = 8
NORM_COLS = 512

VMEM_LIMIT_BYTES = 56 * 1024 * 1024


def _cparams(*sem):
    return pltpu.CompilerParams(dimension_semantics=sem, vmem_limit_bytes=VMEM_LIMIT_BYTES)


def _pick(n, pref):
    t = min(pref, n)
    while n % t:
        t //= 2
    return t


def _token_tile(b, l, rows):
    if l >= rows:
        return 1, _pick(l, rows)
    return _pick(b, max(rows // l, 1)), l


def _silu(x):
    return x / (1.0 + jnp.exp(-x))


def _ada_kernel(c_ref, w_ref, b_ref, o_ref):
    c = c_ref[...]
    ca = _silu(c).astype(BF16)
    o_ref[...] = jnp.dot(ca, w_ref[...].astype(BF16), preferred_element_type=F32) + b_ref[...]


def ada_mod(c_all, w_ada, b_ada):
    depth, d, n = w_ada.shape
    mc = c_all.shape[0]
    bn = _pick(n, 1024)
    return pl.pallas_call(
        _ada_kernel,
        grid=(depth, n // bn),
        in_specs=[pl.BlockSpec((mc, d), lambda l, j: (0, 0)),
                  pl.BlockSpec((None, d, bn), lambda l, j: (l, 0, j)),
                  pl.BlockSpec((None, 1, bn), lambda l, j: (l, 0, j))],
        out_specs=pl.BlockSpec((None, mc, bn), lambda l, j: (l, 0, j)),
        out_shape=jax.ShapeDtypeStruct((depth, mc, n), F32),
        compiler_params=_cparams("arbitrary", "arbitrary"),
        name="ada_mod",
    )(c_all, w_ada, b_ada.reshape(depth, 1, n))


def _norm_mod_kernel(x_ref, g_ref, sh_ref, sc_ref, o_ref):
    d = x_ref.shape[-1]
    cw = _pick(d, NORM_COLS)
    cols = [slice(c * cw, (c + 1) * cw) for c in range(d // cw)]
    ss = None
    for sl in cols:
        x = x_ref[:, :, sl]
        part = jnp.sum(x * x, axis=-1, keepdims=True)
        ss = part if ss is None else ss + part
    r = lax.rsqrt(ss * (1.0 / d) + EPS)
    for sl in cols:
        y = x_ref[:, :, sl] * r * g_ref[:, sl]
        o_ref[:, :, sl] = (y * (1.0 + sc_ref[:, :, sl]) + sh_ref[:, :, sl]).astype(o_ref.dtype)


def norm_mod(x, g, mod, shift_idx, scale_idx):
    b, l, d = x.shape
    bb, bl = _token_tile(b, l, 512)
    return pl.pallas_call(
        _norm_mod_kernel,
        grid=(b // bb, l // bl),
        in_specs=[pl.BlockSpec((bb, bl, d), lambda ib, il: (ib, il, 0)),
                  pl.BlockSpec((1, d), lambda ib, il: (0, 0)),
                  pl.BlockSpec((bb, 1, d), lambda ib, il: (ib, 0, shift_idx)),
                  pl.BlockSpec((bb, 1, d), lambda ib, il: (ib, 0, scale_idx))],
        out_specs=pl.BlockSpec((bb, bl, d), lambda ib, il: (ib, il, 0)),
        out_shape=jax.ShapeDtypeStruct((b, l, d), BF16),
        compiler_params=_cparams("arbitrary", "arbitrary"),
        name="norm_mod",
    )(x, g.reshape(1, d), mod, mod)


def _mm_kernel(*refs, n_lhs, nk, epilogue):
    lhs = refs[:n_lhs]
    ws = refs[n_lhs:2 * n_lhs]
    pos = 2 * n_lhs
    if epilogue == "resid":
        xres_ref, gate_ref = refs[pos], refs[pos + 1]
        pos += 2
    o_ref = refs[pos]

    def compute():
        acc = None
        for a_ref, w_ref in zip(lhs, ws):
            a = a_ref[...]
            a = a.reshape(-1, a.shape[-1])
            part = jnp.dot(a, w_ref[...], preferred_element_type=F32)
            acc = part if acc is None else acc + part
        return acc

    def finish(acc):
        if epilogue == "relu2":
            r = jnp.maximum(acc, 0.0)
            y = r * r
        elif epilogue == "resid":
            y = xres_ref[...] + gate_ref[...] * acc.reshape(o_ref.shape)
        else:
            y = acc
        o_ref[...] = y.reshape(o_ref.shape).astype(o_ref.dtype)

    if nk == 1:
        finish(compute())
    else:
        acc_ref = refs[pos + 1]
        k = pl.program_id(3)

        @pl.when(k == 0)
        def _():
            acc_ref[...] = compute()

        @pl.when(k > 0)
        def _():
            acc_ref[...] += compute()

        @pl.when(k == nk - 1)
        def _():
            finish(acc_ref[...])


def matmul(lhs_list, w_list, out_dtype, epilogue="none", xres=None, mod=None, gate_idx=0,
           rows=1024, cols=1024, bk=None):
    b, l, _ = lhs_list[0].shape
    n = w_list[0].shape[1]
    bb, bl = _token_tile(b, l, rows)
    bn = _pick(n, cols)
    ks = [a.shape[-1] for a in lhs_list]
    nk = 1 if bk is None else ks[0] // bk
    assert nk == 1 or len(lhs_list) == 1
    bks = ks if nk == 1 else [bk]
    in_specs = [pl.BlockSpec((bb, bl, kk), lambda ib, il, j, k: (ib, il, k)) for kk in bks]
    in_specs += [pl.BlockSpec((kk, bn), lambda ib, il, j, k: (k, j)) for kk in bks]
    args = list(lhs_list) + list(w_list)
    if epilogue == "resid":
        gate_off = gate_idx * (n // bn)
        in_specs += [pl.BlockSpec((bb, bl, bn), lambda ib, il, j, k: (ib, il, j)),
                     pl.BlockSpec((bb, 1, bn), lambda ib, il, j, k: (ib, 0, gate_off + j))]
        args += [xres, mod]
    scratch = [pltpu.VMEM((bb * bl, bn), F32)] if nk > 1 else []
    return pl.pallas_call(
        functools.partial(_mm_kernel, n_lhs=len(lhs_list), nk=nk, epilogue=epilogue),
        grid=(b // bb, l // bl, n // bn, nk),
        in_specs=in_specs,
        out_specs=pl.BlockSpec((bb, bl, bn), lambda ib, il, j, k: (ib, il, j)),
        out_shape=jax.ShapeDtypeStruct((b, l, n), out_dtype),
        scratch_shapes=scratch,
        compiler_params=_cparams("arbitrary", "arbitrary", "arbitrary", "arbitrary"),
        name="matmul_" + epilogue,
    )(*args)


def _prep_even_kernel(*refs, t, p, nl, heads, n_carry):
    proj_ref, prev_ref, c_ref, s1_ref, s2_ref, qg_ref, kg_ref, pw_ref, ps_ref = refs[:9]
    pool_ref, npool_ref, q_ref, k32_ref, kb_ref, v32_ref, vb_ref, full_ref = refs[9 + n_carry:]
    il = pl.program_id(1)

    @pl.when(il == 0)
    def _():
        full_ref[0:POOL_MAX, :] = prev_ref[...]

    @pl.when(il > 0)
    def _():
        full_ref[0:POOL_MAX, :] = full_ref[t:t + POOL_MAX, :]

    full_ref[POOL_MAX:POOL_MAX + t, :] = proj_ref[:, 0:POOL_DIM]
    pos1 = p + il * t + 1 + lax.broadcasted_iota(jnp.int32, (t, 1), 0)
    for g, w in enumerate(POOL_WINDOWS):
        sl = slice(g * POOL_GDIM, (g + 1) * POOL_GDIM)
        u = full_ref[POOL_MAX:POOL_MAX + t, sl]
        win = u
        for s in range(1, w):
            win = win + full_ref[POOL_MAX - s:POOL_MAX - s + t, sl]
        cnt = jnp.minimum(pos1, w).astype(F32)
        d = (win / cnt - u).astype(BF16)
        mixed = jnp.dot(d, pw_ref[g], preferred_element_type=F32) * ps_ref[:, sl]
        pool_ref[:, sl] = mixed.astype(pool_ref.dtype)

    @pl.when(il == nl - 1)
    def _():
        npool_ref[...] = full_ref[t + 1:t + POOL_MAX, :]

    seg = (lax.broadcasted_iota(jnp.int32, (HEAD_W, HEAD_W), 0) >> 6) == \
          (lax.broadcasted_iota(jnp.int32, (HEAD_W, HEAD_W), 1) >> 6)
    seg = seg.astype(F32).astype(BF16)
    rc, rs1, rs2 = c_ref[...], s1_ref[...], s2_ref[...]

    def norm_rope(x, g):
        x2 = x * x
        hi = x2.astype(BF16)
        lo = (x2 - hi.astype(F32)).astype(BF16)
        ss = (jnp.dot(lo, seg, preferred_element_type=F32)
              + jnp.dot(hi, seg, preferred_element_type=F32))
        y = x * lax.rsqrt(ss * (1.0 / DIFF_DH) + EPS) * g
        return y * rc + pltpu.roll(y, 8, 1) * rs2 + pltpu.roll(y, HEAD_W - 8, 1) * rs1

    width = heads * HEAD_W
    q_off, k_off, v_off = POOL_DIM, POOL_DIM + width, POOL_DIM + 2 * width
    qg, kg = qg_ref[...], kg_ref[...]
    ks, vs = [], []
    for h in range(heads):
        sl = slice(h * HEAD_W, (h + 1) * HEAD_W)
        rq = norm_rope(proj_ref[:, q_off + h * HEAD_W:q_off + (h + 1) * HEAD_W], qg)
        q_ref[:, sl] = (rq * (DIFF_DH ** -0.5 * LOG2E)).astype(BF16)
        rk = norm_rope(proj_ref[:, k_off + h * HEAD_W:k_off + (h + 1) * HEAD_W], kg)
        kb_ref[:, sl] = rk.astype(BF16)
        ks.append(rk)
        vs.append(proj_ref[:, v_off + h * HEAD_W:v_off + (h + 1) * HEAD_W])
    k32_ref[...] = pltpu.einshape("htd->thd", jnp.stack(ks, axis=0))
    v32_ref[...] = pltpu.einshape("htd->thd", jnp.stack(vs, axis=0))
    vb_ref[...] = proj_ref[:, v_off:v_off + width].astype(BF16)


def _diff_rope_tables(p, l):
    inv_freq = ROPE_THETA ** (-jnp.arange(0, ROPE_DIM, 2, dtype=F32) / ROPE_DIM)
    ang = (p + jnp.arange(l)).astype(F32)[:, None] * inv_freq[None, :]
    cos, sin = jnp.cos(ang), jnp.sin(ang)
    half = ROPE_DIM // 2
    rest = DIFF_DH - ROPE_DIM
    c = jnp.concatenate([cos, cos, jnp.ones((l, rest), F32)], axis=-1)
    s1 = jnp.concatenate([-sin, jnp.zeros((l, half + rest), F32)], axis=-1)
    s2 = jnp.concatenate([jnp.zeros((l, half), F32), sin, jnp.zeros((l, rest), F32)], axis=-1)
    return tuple(jnp.concatenate([a, a], axis=-1) for a in (c, s1, s2))


def prep_even(proj, pool_prev, p, qn, kn, pool_w, pool_scale, layer_slot, n_slots, carry):
    b, l, n_in = proj.shape
    width = (n_in - POOL_DIM) // 3
    heads = width // HEAD_W
    t = _pick(l, 256)
    nl = l // t
    prev16 = jnp.concatenate([jnp.zeros((b, 1, POOL_DIM), F32), pool_prev.astype(F32)], axis=1)
    rc, rs1, rs2 = _diff_rope_tables(p, l)
    qg = jnp.concatenate([qn, qn]).reshape(1, HEAD_W).astype(F32)
    kg = jnp.concatenate([kn, kn]).reshape(1, HEAD_W).astype(F32)
    tok = lambda w: pl.BlockSpec((None, t, w), lambda ib, il: (ib, il, 0))
    tab = pl.BlockSpec((t, HEAD_W), lambda ib, il: (il, 0))
    vec = pl.BlockSpec((1, HEAD_W), lambda ib, il: (0, 0))
    out_shapes = (
        jax.ShapeDtypeStruct((b, l, POOL_DIM), BF16),
        jax.ShapeDtypeStruct((b, POOL_MAX - 1, POOL_DIM), F32),
        jax.ShapeDtypeStruct((b, l, width), BF16),
        jax.ShapeDtypeStruct((n_slots, b, l, heads, HEAD_W), F32),
        jax.ShapeDtypeStruct((b, l, width), BF16),
        jax.ShapeDtypeStruct((n_slots, b, l, heads, HEAD_W), F32),
        jax.ShapeDtypeStruct((b, l, width), BF16),
    )
    stacked = pl.BlockSpec((None, None, t, heads, HEAD_W), lambda ib, il: (layer_slot, ib, il, 0, 0))
    return pl.pallas_call(
        functools.partial(_prep_even_kernel, t=t, p=p, nl=nl, heads=heads, n_carry=len(carry)),
        grid=(b, nl),
        in_specs=[tok(n_in),
                  pl.BlockSpec((None, POOL_MAX, POOL_DIM), lambda ib, il: (ib, 0, 0)),
                  tab, tab, tab, vec, vec,
                  pl.BlockSpec((len(POOL_WINDOWS), POOL_GDIM, POOL_GDIM), lambda ib, il: (0, 0, 0)),
                  pl.BlockSpec((1, POOL_DIM), lambda ib, il: (0, 0))]
                 + [pl.BlockSpec(memory_space=pl.ANY)] * len(carry),
        out_specs=(tok(POOL_DIM),
                   pl.BlockSpec((None, POOL_MAX - 1, POOL_DIM), lambda ib, il: (ib, 0, 0)),
                   tok(width), stacked, tok(width), stacked, tok(width)),
        out_shape=out_shapes,
        scratch_shapes=[pltpu.VMEM((POOL_MAX + t, POOL_DIM), F32)],
        input_output_aliases={9 + i: o for i, o in zip(range(len(carry)), (3, 5))},
        compiler_params=_cparams("arbitrary", "arbitrary"),
        name="prep_even",
    )(proj, prev16, rc, rs1, rs2, qg, kg, pool_w.astype(BF16), pool_scale.reshape(1, POOL_DIM), *carry)


def _prep_odd_kernel(*refs, heads, n_carry):
    fq_ref, fk_ref, fv_ref, fl_ref, qg_ref, kg_ref, fb_ref = refs[:7]
    q_ref, k32_ref, kb_ref, v32_ref, vb_ref, lf_ref = refs[7 + n_carry:]
    def rms(x, g):
        return x * lax.rsqrt(jnp.mean(x * x, axis=-1, keepdims=True) + EPS) * g

    qg, kg = qg_ref[...], kg_ref[...]
    ks, vs = [], []
    for h in range(heads):
        sl = slice(h * HEAD_W, (h + 1) * HEAD_W)
        q_ref[:, sl] = (rms(fq_ref[:, sl], qg) * (FOX_DH ** -0.5 * LOG2E)).astype(BF16)
        rk = rms(fk_ref[:, sl], kg)
        kb_ref[:, sl] = rk.astype(BF16)
        ks.append(rk)
        vs.append(fv_ref[:, sl])
    k32_ref[...] = pltpu.einshape("htd->thd", jnp.stack(ks, axis=0))
    v32_ref[...] = pltpu.einshape("htd->thd", jnp.stack(vs, axis=0))
    vb_ref[...] = fv_ref[...].astype(BF16)
    x = fl_ref[...] + fb_ref[...]
    logf = -(jnp.maximum(-x, 0.0) + jnp.log1p(jnp.exp(-jnp.abs(x))))
    lf_ref[...] = logf[:, 0:heads]


def prep_odd(proj, fl, fox_off, heads, qn, kn, f_bias, layer_slot, n_slots, carry):
    b, l, _ = proj.shape
    width = heads * HEAD_W
    t = _pick(l, 512)
    cb = fox_off // width
    tok = lambda w, c: pl.BlockSpec((None, t, w), lambda ib, il: (ib, il, c))
    vec = pl.BlockSpec((1, HEAD_W), lambda ib, il: (0, 0))
    fb = jnp.zeros((1, HEAD_W), F32).at[0, :heads].set(f_bias.astype(F32))
    out_shapes = (
        jax.ShapeDtypeStruct((b, l, width), BF16),
        jax.ShapeDtypeStruct((n_slots, b, l, heads, HEAD_W), F32),
        jax.ShapeDtypeStruct((b, l, width), BF16),
        jax.ShapeDtypeStruct((n_slots, b, l, heads, HEAD_W), F32),
        jax.ShapeDtypeStruct((b, l, width), BF16),
        jax.ShapeDtypeStruct((b, l, heads), F32),
    )
    stacked = pl.BlockSpec((None, None, t, heads, HEAD_W), lambda ib, il: (layer_slot, ib, il, 0, 0))
    return pl.pallas_call(
        functools.partial(_prep_odd_kernel, heads=heads, n_carry=len(carry)),
        grid=(b, l // t),
        in_specs=[tok(width, cb), tok(width, cb + 1), tok(width, cb + 2), tok(HEAD_W, 0),
                  vec, vec, vec] + [pl.BlockSpec(memory_space=pl.ANY)] * len(carry),
        out_specs=(tok(width, 0), stacked, tok(width, 0), stacked, tok(width, 0),
                   tok(heads, 0)),
        out_shape=out_shapes,
        input_output_aliases={7 + i: o for i, o in zip(range(len(carry)), (1, 3))},
        compiler_params=_cparams("arbitrary", "arbitrary"),
        name="prep_odd",
    )(proj, proj, proj, fl, qn.reshape(1, HEAD_W), kn.reshape(1, HEAD_W), fb, *carry)


CUMSUM_BLOCK = 256


def _cumsum_kernel(x_ref, o_ref, carry_ref):
    @pl.when(pl.program_id(0) == 0)
    def _():
        carry_ref[...] = jnp.zeros_like(carry_ref)

    x = x_ref[...]
    x1 = x.astype(BF16)
    r1 = x - x1.astype(F32)
    x2 = r1.astype(BF16)
    x3 = (r1 - x2.astype(F32)).astype(BF16)
    n = x.shape[-1]
    tri = lax.broadcasted_iota(jnp.int32, (n, n), 0) <= lax.broadcasted_iota(jnp.int32, (n, n), 1)
    tri = tri.astype(F32).astype(BF16)
    c = (jnp.dot(x3, tri, preferred_element_type=F32)
         + jnp.dot(x2, tri, preferred_element_type=F32)
         + jnp.dot(x1, tri, preferred_element_type=F32)) + carry_ref[...]
    o_ref[...] = c
    carry_ref[...] = c[:, n - 1:n]


def cumsum_lanes(x):
    r, n = x.shape
    return pl.pallas_call(
        _cumsum_kernel,
        grid=(n // CUMSUM_BLOCK,),
        in_specs=[pl.BlockSpec((r, CUMSUM_BLOCK), lambda i: (0, i))],
        out_specs=pl.BlockSpec((r, CUMSUM_BLOCK), lambda i: (0, i)),
        out_shape=jax.ShapeDtypeStruct((r, n), F32),
        scratch_shapes=[pltpu.VMEM((r, 1), F32)],
        compiler_params=_cparams("arbitrary"),
        name="cumsum",
    )(x)


def _flash_kernel(*refs, kind, tq, tk, p, lam_init, rq, single_tile, has_past):
    if has_past:
        pk_ref, pv_ref, *refs = refs
    if kind == "diff":
        (q_ref, k_ref, v_ref, lv_ref, gn_ref, o_ref,
         qzt_scr, s0_scr, s1_scr, mx0_scr, mx1_scr, p0_scr, p1_scr, acc_scr, l_scr) = refs
    else:
        (q_ref, k_ref, v_ref, cq_ref, ck_ref, o_ref,
         qzt_scr, s0_scr, s1_scr, mx0_scr, mx1_scr, p0_scr, p1_scr, acc_scr, l_scr,
         ckcol_ref) = refs
    s_scr, mx_scr, p_scr = (s0_scr, s1_scr), (mx0_scr, mx1_scr), (p0_scr, p1_scr)
    iq = pl.program_id(2)
    q0 = iq * tq
    qf = q_ref[...].astype(F32)
    if kind == "diff":
        lane = lax.broadcasted_iota(jnp.int32, (tq, HEAD_W), 1)
        qz = jnp.concatenate([jnp.where(lane < DIFF_DH, qf, 0.0),
                              jnp.where(lane >= DIFF_DH, qf, 0.0)], axis=0)
    else:
        qz = qf if rq == tq else jnp.concatenate([qf, jnp.zeros((rq - tq, HEAD_W), F32)], axis=0)
        cq = cq_ref[pl.ds(iq, 1), :] * LOG2E

        @pl.when(iq == 0)
        def _():
            def fill(c, carry):
                row = jnp.broadcast_to(ck_ref[pl.ds(c, 1), :] * LOG2E, (HEAD_W, HEAD_W))
                ckcol_ref[pl.ds(pl.multiple_of(c * HEAD_W, HEAD_W), HEAD_W), :] = row.T
                return carry
            n_chunks = ck_ref.shape[0]
            lax.fori_loop(0, n_chunks, fill, 0,
                          unroll=FILL_UNROLL if n_chunks % FILL_UNROLL == 0 else True)

    qzt_scr[...] = qz.T.astype(BF16)
    acc_scr[...] = jnp.zeros_like(acc_scr)

    def produce(slot, j, masked):
        start = pl.multiple_of(j * tk, tk)
        if has_past:
            s = jnp.concatenate(
                [jnp.dot(pk_ref[...].astype(BF16), qzt_scr[...], preferred_element_type=F32),
                 jnp.dot(k_ref[...], qzt_scr[...], preferred_element_type=F32)], axis=0)
        else:
            s = jnp.dot(k_ref[pl.ds(start, tk), :], qzt_scr[...], preferred_element_type=F32)
        if kind == "fox":
            ckc = ckcol_ref[pl.ds(start, tk), :]
            s = jnp.concatenate([s[:, c * HEAD_W:(c + 1) * HEAD_W] - ckc
                                 for c in range(rq // HEAD_W)], axis=1)
        if masked:
            kpos = start + lax.broadcasted_iota(jnp.int32, (tk, rq), 0)
            r = lax.broadcasted_iota(jnp.int32, (tk, rq), 1)
            if kind == "diff":
                qpos = p + q0 + jnp.where(r >= tq, r - tq, r)
                ok = (kpos >> 6) <= (qpos >> 6)
            else:
                ok = kpos <= p + q0 + r
            s = jnp.where(ok, s, -jnp.inf)
        s_scr[slot][...] = s
        mx_scr[slot][...] = jnp.max(s, axis=0, keepdims=True)

    def consume(slot, j, m, l):
        tmax = mx_scr[slot][...]
        if kind == "fox":
            m_new = jnp.maximum(m, tmax + cq)
            shift = cq - m_new
        else:
            m_new = jnp.maximum(m, tmax)
            shift = -m_new
        alpha = jnp.exp2(m - m_new)
        shift_b = jnp.broadcast_to(shift, (PACK_ROWS, rq))
        lsum = jnp.zeros((PACK_ROWS, rq), F32)
        for c in range(tk // PACK_ROWS):
            rows = slice(c * PACK_ROWS, (c + 1) * PACK_ROWS)
            pr = jnp.exp2(s_scr[slot][rows, :] + shift_b)
            lsum = lsum + pr
            p_scr[slot][rows, :] = pr.astype(BF16)
        start = pl.multiple_of(j * tk, tk)
        tn = (((0,), (0,)), ((), ()))
        if has_past:
            pv = (lax.dot_general(pv_ref[...].astype(BF16), p_scr[slot][0:p, :], tn,
                                  preferred_element_type=F32)
                  + lax.dot_general(v_ref[...], p_scr[slot][p:tk, :], tn,
                                    preferred_element_type=F32))
        else:
            pv = lax.dot_general(v_ref[pl.ds(start, tk), :], p_scr[slot][...], tn,
                                 preferred_element_type=F32)
        acc_scr[...] = acc_scr[...] * alpha + pv
        return m_new, alpha * l + jnp.sum(lsum, axis=0, keepdims=True)

    def run(j0, count, m, l, ends_masked):
        for i in range(count):
            slot = i % 2
            if i + 1 < count:
                produce(1 - slot, j0 + i + 1, ends_masked and i + 2 == count)
            elif not ends_masked:
                produce(1 - slot, j0 + count, False)
            m, l = consume(slot, j0 + i, m, l)
        return m, l

    m0 = jnp.full((1, rq), -jnp.inf, F32)
    l0 = jnp.zeros((1, rq), F32)
    if single_tile:
        produce(0, 0, True)
        l_scr[...] = run(0, 1, m0, l0, True)[1]
    else:
        n_full = lax.div(p + q0 + tq + tk - 1, tk) - 1
        n_iter = lax.div(jnp.maximum(n_full - 1, 0), FLASH_UNROLL)

        @pl.when(n_full > 0)
        def _():
            produce(0, 0, False)

        m, l = lax.fori_loop(
            0, n_iter, lambda t, c: run(FLASH_UNROLL * t, FLASH_UNROLL, c[0], c[1], False), (m0, l0))
        first = FLASH_UNROLL * n_iter
        rest = n_full - first

        @pl.when(n_full == 0)
        def _():
            produce(0, 0, True)
            l_scr[...] = run(0, 1, m, l, True)[1]

        for r in range(1, FLASH_UNROLL + 1):
            @pl.when(jnp.logical_and(n_full > 0, rest == r))
            def _(r=r):
                l_scr[...] = run(first, r + 1, m, l, True)[1]

    o = (acc_scr[...] / l_scr[...]).T
    if kind == "diff":
        lv = lv_ref[...]
        lam = (jnp.exp(jnp.sum(lv[0:1] * lv[1:2], axis=-1, keepdims=True))
               - jnp.exp(jnp.sum(lv[2:3] * lv[3:4], axis=-1, keepdims=True)) + lam_init)
        o = o[:tq] - lam * o[tq:]
        o = o * lax.rsqrt(jnp.mean(o * o, axis=-1, keepdims=True) + EPS) * gn_ref[...]
        o = o * (1.0 - lam_init)
    else:
        o = o[:tq]
    o_ref[...] = o.astype(o_ref.dtype)


def _head_major_kernel(x_ref, o_ref):
    o_ref[...] = pltpu.einshape("thd->htd", x_ref[...]).astype(o_ref.dtype)


def head_major(cache):
    n, b, p, heads, w = cache.shape
    pc = _pick(p, 512)
    return pl.pallas_call(
        _head_major_kernel,
        grid=(n, b, p // pc),
        in_specs=[pl.BlockSpec((None, None, pc, heads, w), lambda j, ib, ip: (j, ib, ip, 0, 0))],
        out_specs=pl.BlockSpec((None, None, heads, pc, w), lambda j, ib, ip: (j, ib, 0, ip, 0)),
        out_shape=jax.ShapeDtypeStruct((n, b, heads, p, w), BF16),
        compiler_params=_cparams("arbitrary", "arbitrary", "arbitrary"),
        name="head_major",
    )(cache)


def flash(kind, q, k_new, v_new, past_k, past_v, layer_slot, tq_pref, tk_pref, extra, lam_init=0.0):
    b, l, width = q.shape
    heads = width // HEAD_W
    p = past_k.shape[3]
    has_past = p > 0
    tq = _pick(l, tq_pref)
    tk = p + l if has_past else _pick(l, tk_pref)
    assert tk % PACK_ROWS == 0 and (tk == p + l or tk % tq == 0)
    qspec = pl.BlockSpec((None, tq, HEAD_W), lambda ib, ih, iq: (ib, iq, ih))
    kvspec = pl.BlockSpec((None, l, HEAD_W), lambda ib, ih, iq: (ib, 0, ih))
    pastspec = pl.BlockSpec((None, None, None, p, HEAD_W), lambda ib, ih, iq: (layer_slot, ib, ih, 0, 0))
    past_specs, past_args = ([pastspec, pastspec], [past_k, past_v]) if has_past else ([], [])
    rq = 2 * tq if kind == "diff" else max(tq, HEAD_W)
    scratch = ([pltpu.VMEM((HEAD_W, rq), BF16)]
               + [pltpu.VMEM((tk, rq), F32)] * 2
               + [pltpu.VMEM((1, rq), F32)] * 2
               + [pltpu.VMEM((tk, rq), BF16)] * 2
               + [pltpu.VMEM((HEAD_W, rq), F32),
                  pltpu.VMEM((1, rq), F32)])
    if kind == "diff":
        lam_vec, gn = extra
        especs = [pl.BlockSpec(lam_vec.shape, lambda ib, ih, iq: (0, 0)),
                  pl.BlockSpec((1, HEAD_W), lambda ib, ih, iq: (0, 0))]
        eargs = [lam_vec, gn.reshape(1, HEAD_W)]
    else:
        cum = extra
        lc = cum.shape[-1]
        assert lc % HEAD_W == 0 and lc >= p + l
        cq = cum[:, :, p:p + l].reshape(b, heads, l // tq, tq)
        if rq > tq:
            cq = jnp.pad(cq, ((0, 0), (0, 0), (0, 0), (0, rq - tq)))
        ck = cum.reshape(b, heads, lc // HEAD_W, HEAD_W)
        especs = [pl.BlockSpec((None, None, l // tq, rq), lambda ib, ih, iq: (ib, ih, 0, 0)),
                  pl.BlockSpec((None, None, lc // HEAD_W, HEAD_W), lambda ib, ih, iq: (ib, ih, 0, 0))]
        eargs = [cq, ck]
        scratch.append(pltpu.VMEM((lc, HEAD_W), F32))
    return pl.pallas_call(
        functools.partial(_flash_kernel, kind=kind, tq=tq, tk=tk, p=p, lam_init=lam_init, rq=rq,
                          single_tile=(tk == p + l), has_past=has_past),
        grid=(b, heads, l // tq),
        in_specs=past_specs + [qspec, kvspec, kvspec] + especs,
        out_specs=qspec,
        out_shape=jax.ShapeDtypeStruct((b, l, width), BF16),
        scratch_shapes=scratch,
        compiler_params=_cparams("arbitrary", "arbitrary", "arbitrary"),
        name="flash_" + kind,
    )(*past_args, q, k_new, v_new, *eargs)


def _ret_kernel(lg_ref, q_ref, k_ref, v_ref, g_ref, cos_ref, sin_ref, s0_ref, gn_ref,
                o_ref, sn_ref, s_scr, *, c, nc):
    t = pl.program_id(2)

    @pl.when(t == 0)
    def _():
        s_scr[...] = s0_ref[...]

    cos, sin = cos_ref[...], sin_ref[...]
    half = RET_DK // 2

    def rot(x):
        x1, x2 = x[:, :half], x[:, half:]
        return jnp.concatenate([x1 * cos - x2 * sin, x2 * cos + x1 * sin], axis=-1)

    q = rot(q_ref[...])
    k = rot(k_ref[...]) * RET_DK ** -0.5
    vb = v_ref[...].astype(BF16)
    lg = lg_ref[pl.program_id(1)]
    dist = (lax.broadcasted_iota(jnp.int32, (c, c), 0)
            - lax.broadcasted_iota(jnp.int32, (c, c), 1)).astype(F32)
    intra = jnp.where(dist >= 0, jnp.exp(lg * jnp.maximum(dist, 0.0)), 0.0)
    ic = lax.broadcasted_iota(jnp.int32, (c, 1), 0).astype(F32)
    q_dec = jnp.exp(lg * (ic + 1.0))
    k_dec = jnp.exp(lg * (c - 1.0 - ic))
    blk_dec = jnp.exp(lg * jnp.full((1, RET_DV), float(c), F32))
    att = lax.dot_general(q.astype(BF16), k.astype(BF16), (((1,), (1,)), ((), ())),
                          preferred_element_type=F32) * intra
    s = s_scr[...]
    o = (jnp.dot(att.astype(BF16), vb, preferred_element_type=F32)
         + jnp.dot((q * q_dec).astype(BF16), s.astype(BF16), preferred_element_type=F32))
    s_new = s * blk_dec + lax.dot_general((k * k_dec).astype(BF16), vb, (((0,), (0,)), ((), ())),
                                          preferred_element_type=F32)
    s_scr[...] = s_new
    mu = jnp.mean(o, axis=-1, keepdims=True)
    oc = o - mu
    var = jnp.mean(oc * oc, axis=-1, keepdims=True)
    y = oc * lax.rsqrt(var + EPS) * gn_ref[...]
    o_ref[...] = (y * _silu(g_ref[...])).astype(o_ref.dtype)

    @pl.when(t == nc - 1)
    def _():
        sn_ref[...] = s_new


def retention(proj, p, s0, ret_gn):
    b, l, _ = proj.shape
    heads = s0.shape[1]
    c = _pick(l, 256)
    nc = l // c
    inv_freq = RET_THETA ** (-jnp.arange(0, RET_DK, 2, dtype=F32) / RET_DK)
    ang = (p + jnp.arange(l)).astype(F32)[:, None] * inv_freq[None, :]
    cos, sin = jnp.cos(ang), jnp.sin(ang)
    lg = jnp.asarray([math.log(1.0 - 2.0 ** (-5 - h)) for h in range(heads)], F32)
    col = lambda off: pl.BlockSpec((None, c, RET_DK), lambda ib, ih, it: (ib, it, off + ih))
    tab = pl.BlockSpec((c, RET_DK // 2), lambda ib, ih, it: (it, 0))
    st = pl.BlockSpec((None, None, RET_DK, RET_DV), lambda ib, ih, it: (ib, ih, 0, 0))
    return pl.pallas_call(
        functools.partial(_ret_kernel, c=c, nc=nc),
        grid=(b, heads, nc),
        in_specs=[pl.BlockSpec(memory_space=pltpu.SMEM),
                  col(0), col(heads), col(2 * heads), col(3 * heads), tab, tab, st,
                  pl.BlockSpec((None, 1, RET_DV), lambda ib, ih, it: (ih, 0, 0))],
        out_specs=(pl.BlockSpec((None, c, RET_DV), lambda ib, ih, it: (ib, it, ih)), st),
        out_shape=(jax.ShapeDtypeStruct((b, l, heads * RET_DV), BF16),
                   jax.ShapeDtypeStruct(s0.shape, F32)),
        scratch_shapes=[pltpu.VMEM((RET_DK, RET_DV), F32)],
        compiler_params=_cparams("arbitrary", "arbitrary", "arbitrary"),
        name="retention",
    )(lg, proj, proj, proj, proj, cos, sin, s0.astype(F32), ret_gn.reshape(heads, 1, RET_DV))


def even_mixer(h, x, mod, pool_prev, past_k, past_v, w_in, w_out, pool_w, pool_scale,
               qn, kn, lam_vec, out_gn, lam_init, layer_slot, n_slots, carry):
    p = past_k.shape[3]
    proj = matmul([h], [w_in], F32)
    pool_out, new_pool, q, k32, kb, v32, vb = prep_even(
        proj, pool_prev, p, qn, kn, pool_w, pool_scale, layer_slot, n_slots, carry)
    o = flash("diff", q, kb, vb, past_k, past_v, layer_slot, 512, 512,
              (lam_vec.astype(F32), out_gn.astype(F32)), lam_init)
    x = matmul([pool_out, o], [w_out[:POOL_DIM], w_out[POOL_DIM:]], F32, "resid", x, mod, 2)
    return x, new_pool, k32, v32


def odd_mixer(h, x, mod, s0, past_k, past_v, past_logf, w_in, w_fl, w_out, ret_gn, qn, kn, f_bias,
              layer_slot, n_slots, carry):
    p = past_k.shape[3]
    b, l, _ = h.shape
    ret_heads = s0.shape[1]
    fox_heads = f_bias.shape[0]
    proj = matmul([h], [w_in], F32)
    fl = matmul([h], [w_fl], F32)
    ret, s_new = retention(proj, p, s0, ret_gn)
    fox_off = 2 * ret_heads * RET_DK + 2 * ret_heads * RET_DV
    q, k32, kb, v32, vb, logf = prep_odd(proj, fl, fox_off, fox_heads, qn, kn, f_bias,
                                         layer_slot, n_slots, carry)
    lc = -(-(p + l) // CUMSUM_BLOCK) * CUMSUM_BLOCK
    lf_all = jnp.concatenate([past_logf.astype(F32), logf,
                              jnp.zeros((b, lc - p - l, fox_heads), F32)], axis=1)
    cum = cumsum_lanes(jnp.swapaxes(lf_all, 1, 2).reshape(b * fox_heads, lc))
    o = flash("fox", q, kb, vb, past_k, past_v, layer_slot, 512, 512, cum.reshape(b, fox_heads, lc))
    x = matmul([ret, o], [w_out[:ret.shape[-1]], w_out[ret.shape[-1]:]], F32, "resid", x, mod, 2)
    return x, s_new, k32, v32, logf


def trunk(x, mods, pool_prev, diff_k, diff_v, ret_s, fox_k, fox_v, fox_logf, wts):
    (norm1, norm2, w_up, w_down, w_in_even, w_out_even, pool_w, pool_scale, diff_qn, diff_kn,
     diff_lam, diff_gn, w_in_odd, w_fl_odd, w_out_odd, ret_gn, fox_qn, fox_kn, fox_fbias) = wts
    depth = norm1.shape[0]
    n_pair = depth // 2
    b, seq, _ = x.shape
    n_pool, n_rs, n_fl = [], [], []
    diff_kv, fox_kv = (), ()
    for l in range(depth):
        j = l // 2
        mod = mods[l]
        h = norm_mod(x, norm1[l], mod, 0, 1)
        if l % 2 == 0:
            x, p_new, *diff_kv = even_mixer(
                h, x, mod, pool_prev[j], diff_k, diff_v, w_in_even[j], w_out_even[j], pool_w[j],
                pool_scale[j], diff_qn[j], diff_kn[j], diff_lam[j], diff_gn[j],
                0.8 - 0.6 * math.exp(-0.3 * l), j, n_pair, tuple(diff_kv))
            n_pool.append(p_new)
        else:
            x, s_new, *fox_kv, lf_new = odd_mixer(
                h, x, mod, ret_s[j], fox_k, fox_v, fox_logf[j], w_in_odd[j], w_fl_odd[j],
                w_out_odd[j], ret_gn[j], fox_qn[j], fox_kn[j], fox_fbias[j], j, n_pair, tuple(fox_kv))
            n_rs.append(s_new)
            n_fl.append(lf_new)
        h = norm_mod(x, norm2[l], mod, 3, 4)
        a = matmul([h], [w_up[l]], BF16, "relu2")
        x = matmul([a], [w_down[l]], F32, "resid", x, mod, 5, bk=_pick(a.shape[-1], 2048))
    st = jnp.stack
    return x, st(n_pool), diff_kv[0], diff_kv[1], st(n_rs), fox_kv[0], fox_kv[1], st(n_fl)


def kernel(x_prompt, x_sample, c_prompt, c_sample, cache_pool, cache_diff_k, cache_diff_v, state_ret,
           cache_fox_k, cache_fox_v, cache_fox_logf, w_ada, b_ada, norm1, norm2, w_up, w_down,
           w_in_even, w_out_even, pool_w, pool_scale, diff_qn, diff_kn, diff_lam, diff_gn,
           w_in_odd, w_out_odd, ret_gn, fox_qn, fox_kn, fox_fbias):
    bp, _, d = x_prompt.shape
    bs = x_sample.shape[0]
    n_pair = cache_pool.shape[0]
    depth = w_ada.shape[0]
    diff_heads = cache_diff_k.shape[3]
    ret_heads = state_ret.shape[2]
    fox_heads = cache_fox_k.shape[3]

    mc = -(-(bp + bs) // 16) * 16
    c_all = jnp.concatenate([c_prompt, c_sample, jnp.zeros((mc - bp - bs, d), F32)], axis=0)
    mod_all = ada_mod(c_all, w_ada, b_ada)
    mods_p = mod_all[:, :bp].reshape(depth, bp, 1, 6 * d)
    mods_s = mod_all[:, bp:bp + bs].reshape(depth, bs, 1, 6 * d)

    fox_main = w_in_odd.shape[-1] - fox_heads
    w_fl = jnp.pad(w_in_odd[:, :, fox_main:], ((0, 0), (0, 0), (0, HEAD_W - fox_heads))).astype(BF16)
    wts = (norm1, norm2, w_up.astype(BF16), w_down.astype(BF16), w_in_even.astype(BF16),
           w_out_even.astype(BF16), pool_w, pool_scale, diff_qn, diff_kn, diff_lam, diff_gn,
           w_in_odd[:, :, :fox_main].astype(BF16), w_fl, w_out_odd.astype(BF16),
           ret_gn, fox_qn, fox_kn, fox_fbias)

    dt = x_prompt.dtype
    y_p, pool_p, dk_p, dv_p, rs_p, fk_p, fv_p, fl_p = trunk(
        x_prompt, mods_p,
        jnp.zeros((n_pair, bp, POOL_MAX - 1, POOL_DIM), dt),
        jnp.zeros((n_pair, bp, diff_heads, 0, HEAD_W), BF16),
        jnp.zeros((n_pair, bp, diff_heads, 0, HEAD_W), BF16),
        jnp.zeros((n_pair, bp, ret_heads, RET_DK, RET_DV), F32),
        jnp.zeros((n_pair, bp, fox_heads, 0, HEAD_W), BF16),
        jnp.zeros((n_pair, bp, fox_heads, 0, HEAD_W), BF16),
        jnp.zeros((n_pair, bp, 0, fox_heads), F32),
        wts)

    def by_head(cache):
        if cache.shape[3] % 8:
            return jnp.swapaxes(cache, 2, 3)
        return head_major(cache)

    y_s, pool_s, dk_s, dv_s, rs_s, fk_s, fv_s, fl_s = trunk(
        x_sample, mods_s, cache_pool, by_head(cache_diff_k), by_head(cache_diff_v), state_ret,
        by_head(cache_fox_k), by_head(cache_fox_v), cache_fox_logf, wts)
    return (y_p, y_s, pool_p, pool_s, dk_p, dk_s, dv_p, dv_s, rs_p, rs_s,
            fk_p, fk_s, fv_p, fv_s, fl_p, fl_s)
```
